```python
import math
import jax, jax.numpy as jnp
from jax import lax
import numpy as np

D_MODEL = 2048
BATCH = 2
SEQ = 4096
DEPTH = 1

CHUNK = 64
Q_BLOCK = 128
N_ATTN_HEADS = 8
ATTN_HEAD_DIM = 64
ATTN_V_DIM = 2 * ATTN_HEAD_DIM
ATTN_WIDTH = N_ATTN_HEADS * ATTN_V_DIM
CONV_WIDTH = D_MODEL - ATTN_WIDTH
CONV_GROUPS = 8
CONV_K = 3
D_FF = (8 * D_MODEL + 3 * 256 - 1) // (3 * 256) * 256
ROPE_THETA = 10000.0
EPS = 1e-6
SUBLN_EPS = 1e-5
QK_COLS = N_ATTN_HEADS * 2 * ATTN_HEAD_DIM
IN_COLS = 2 * QK_COLS + ATTN_WIDTH + 3 * CONV_WIDTH

kernel_name = "hybrid_diffattn_shortconv_block"


def lambda_init(layer_idx):
    return 0.8 - 0.6 * math.exp(-0.3 * layer_idx)


def rms_norm(x, w, eps=EPS):
    xf = x.astype(jnp.float32)
    y = xf * lax.rsqrt(jnp.mean(xf * xf, axis=-1, keepdims=True) + eps)
    return (y * w.astype(jnp.float32)).astype(x.dtype)


def rope(t, cos, sin):
    tf = t.astype(jnp.float32)
    half = tf.shape[-1] // 2
    t1, t2 = tf[..., :half], tf[..., half:]
    c = cos[None, :, None, None, :]
    s = sin[None, :, None, None, :]
    out = jnp.concatenate([t1 * c - t2 * s, t2 * c + t1 * s], axis=-1)
    return out.astype(t.dtype)


def diff_attention(q, k, v, lam):
    b, s, h, _, dk = q.shape
    nb = s // Q_BLOCK
    scale = 1.0 / math.sqrt(dk)
    qb = q.reshape(b, nb, Q_BLOCK, h, 2, dk).transpose(1, 0, 2, 3, 4, 5)
    key_chunk = jnp.arange(s) // CHUNK

    def block(args):
        qi, i = args
        q_chunk = (i * Q_BLOCK + jnp.arange(Q_BLOCK)) // CHUNK
        mask = key_chunk[None, :] <= q_chunk[:, None]
        sc = jnp.einsum('bqhcd,bkhcd->bhcqk', qi, k).astype(jnp.float32) * scale
        sc = jnp.where(mask[None, None, None], sc, -1e30)
        p = jax.nn.softmax(sc, axis=-1)
        a = p[:, :, 0] - lam * p[:, :, 1]
        return jnp.einsum('bhqk,bkhd->bqhd', a.astype(v.dtype), v)

    out = lax.map(block, (qb, jnp.arange(nb)))
    return out.transpose(1, 0, 2, 3, 4).reshape(b, s, h, v.shape[-1])


def causal_depthwise_conv(u, w):
    c = u.shape[-1]
    return lax.conv_general_dilated(
        u, w[:, None, :].astype(u.dtype), window_strides=(1,),
        padding=[(CONV_K - 1, 0)], dimension_numbers=('NWC', 'WIO', 'NWC'),
        feature_group_count=c)


def setup_inputs(seed: int = 0) -> dict:
    key = jax.random.key(seed)
    ks = jax.random.split(key, 16)
    f32 = jnp.float32
    nrm = lambda k, shape, s: jax.random.normal(k, shape, f32) * s
    return {
        "x": jax.random.normal(ks[0], (BATCH, SEQ, D_MODEL), f32),
        "attn_norm_w": 1.0 + nrm(ks[1], (DEPTH, D_MODEL), 0.02),
        "w_in": nrm(ks[2], (DEPTH, D_MODEL, IN_COLS), D_MODEL ** -0.5),
        "q_norm_w": 1.0 + nrm(ks[3], (DEPTH, ATTN_HEAD_DIM), 0.02),
        "k_norm_w": 1.0 + nrm(ks[4], (DEPTH, ATTN_HEAD_DIM), 0.02),
        "lambda_q1": nrm(ks[5], (DEPTH, ATTN_HEAD_DIM), 0.1),
        "lambda_k1": nrm(ks[6], (DEPTH, ATTN_HEAD_DIM), 0.1),
        "lambda_q2": nrm(ks[7], (DEPTH, ATTN_HEAD_DIM), 0.1),
        "lambda_k2": nrm(ks[8], (DEPTH, ATTN_HEAD_DIM), 0.1),
        "subln_w": 1.0 + nrm(ks[9], (DEPTH, ATTN_V_DIM), 0.02),
        "conv_w": nrm(ks[10], (DEPTH, CONV_K, CONV_WIDTH), CONV_K ** -0.5),
        "w_out": nrm(ks[11], (DEPTH, D_MODEL, D_MODEL), D_MODEL ** -0.5),
        "ffn_norm_w": 1.0 + nrm(ks[12], (DEPTH, D_MODEL), 0.02),
        "w_gate": nrm(ks[13], (DEPTH, D_MODEL, D_FF), D_MODEL ** -0.5),
        "w_up": nrm(ks[14], (DEPTH, D_MODEL, D_FF), D_MODEL ** -0.5),
        "w_down": nrm(ks[15], (DEPTH, D_FF, D_MODEL), D_FF ** -0.5),
    }


def reference(x, attn_norm_w, w_in, q_norm_w, k_norm_w, lambda_q1, lambda_k1,
              lambda_q2, lambda_k2, subln_w, conv_w, w_out, ffn_norm_w,
              w_gate, w_up, w_down):
    b, s, _ = x.shape
    pos = jnp.arange(s, dtype=jnp.float32)
    inv_freq = ROPE_THETA ** (-jnp.arange(0, ATTN_HEAD_DIM, 2, dtype=jnp.float32) / ATTN_HEAD_DIM)
    ang = pos[:, None] * inv_freq[None, :]
    cos, sin = jnp.cos(ang), jnp.sin(ang)

    h_res = x
    for l in range(DEPTH):
        lam_init = lambda_init(l)
        xn = rms_norm(h_res, attn_norm_w[l])
        proj = jnp.einsum('bsd,de->bse', xn, w_in[l])
        o1 = QK_COLS
        o2 = o1 + QK_COLS
        o3 = o2 + ATTN_WIDTH
        o4 = o3 + CONV_WIDTH
        o5 = o4 + CONV_WIDTH
        q = proj[..., :o1].reshape(b, s, N_ATTN_HEADS, 2, ATTN_HEAD_DIM)
        k = proj[..., o1:o2].reshape(b, s, N_ATTN_HEADS, 2, ATTN_HEAD_DIM)
        v = proj[..., o2:o3].reshape(b, s, N_ATTN_HEADS, ATTN_V_DIM)
        gate_b = proj[..., o3:o4]
        gate_c = proj[..., o4:o5]
        hc = proj[..., o5:]

        q = rope(rms_norm(q, q_norm_w[l]), cos, sin)
        k = rope(rms_norm(k, k_norm_w[l]), cos, sin)
        lam = (jnp.exp(jnp.sum(lambda_q1[l].astype(jnp.float32) * lambda_k1[l].astype(jnp.float32)))
               - jnp.exp(jnp.sum(lambda_q2[l].astype(jnp.float32) * lambda_k2[l].astype(jnp.float32)))
               + lam_init)
        attn = diff_attention(q, k, v, lam)
        attn = rms_norm(attn, subln_w[l], SUBLN_EPS) * (1.0 - lam_init)
        attn = attn.reshape(b, s, ATTN_WIDTH)

        conv = gate_b * causal_depthwise_conv(gate_c * hc, conv_w[l])

        mixed = jnp.concatenate([attn.astype(conv.dtype), conv], axis=-1)
        h_res = h_res + jnp.einsum('bse,ed->bsd', mixed, w_out[l])

        fn = rms_norm(h_res, ffn_norm_w[l])
        g = jnp.einsum('bsd,df->bsf', fn, w_gate[l])
        u = jnp.einsum('bsd,df->bsf', fn, w_up[l])
        h_res = h_res + jnp.einsum('bsf,fd->bsd', jax.nn.silu(g) * u, w_down[l])
    return h_res
```

```python
import functools
import math

import jax
import jax.numpy as jnp
from jax import lax
from jax.experimental import pallas as pl
from jax.experimental.pallas import tpu as pltpu

F32 = jnp.float32
BF16 = jnp.bfloat16

CHUNK = 64
HEAD_DIM = 64
V_DIM = 2 * HEAD_DIM
N_HEADS = 8
CONV_K = 3
ROPE_THETA = 10000.0
EPS = 1e-6
SUBLN_EPS = 1e-5
LAMBDA_INIT = 0.8 - 0.6 * math.exp(-0.3 * 0)

LANES = 128
SUBLANES = 8
VMEM_LIMIT = 56 * 1024 * 1024


def _params(semantics):
    return pltpu.CompilerParams(dimension_semantics=semantics,
                                vmem_limit_bytes=VMEM_LIMIT)


def _mm(a_bf16, w_f32):
    return jnp.dot(a_bf16, w_f32.astype(BF16), preferred_element_type=F32)


def _rmsnorm_kernel(x_ref, w_ref, o_ref):
    x = x_ref[...]
    ms = jnp.mean(x * x, axis=-1, keepdims=True)
    o_ref[...] = ((x * lax.rsqrt(ms + EPS)) * w_ref[...]).astype(o_ref.dtype)


def _rmsnorm(x, w, tm):
    t, d = x.shape
    return pl.pallas_call(
        _rmsnorm_kernel,
        grid=(t // tm,),
        in_specs=[pl.BlockSpec((tm, d), lambda i: (i, 0)),
                  pl.BlockSpec((1, d), lambda i: (0, 0))],
        out_specs=pl.BlockSpec((tm, d), lambda i: (i, 0)),
        out_shape=jax.ShapeDtypeStruct((t, d), BF16),
        compiler_params=_params(("parallel",)),
        name="rmsnorm",
    )(x, w.reshape(1, d))


def _qk_kernel(xn_ref, w_ref, nw_ref, cos_ref, sin_ref, o_ref):
    acc = _mm(xn_ref[...], w_ref[...])
    lane = lax.broadcasted_iota(jnp.int32, (1, LANES), 1)
    lo_half = lane < HEAD_DIM
    first = (lane % HEAD_DIM) < (HEAD_DIM // 2)
    cos = cos_ref[...]
    sin = sin_ref[...]
    for c in range(acc.shape[1] // LANES):
        cols = slice(c * LANES, (c + 1) * LANES)
        y = acc[:, cols]
        sq = y * y
        s_lo = jnp.sum(jnp.where(lo_half, sq, 0.0), axis=-1, keepdims=True)
        s_hi = jnp.sum(jnp.where(lo_half, 0.0, sq), axis=-1, keepdims=True)
        ms = jnp.where(lo_half, s_lo, s_hi) * (1.0 / HEAD_DIM)
        yn = (y * lax.rsqrt(ms + EPS)) * nw_ref[:, cols]
        rot = jnp.where(first, pltpu.roll(yn, LANES - HEAD_DIM // 2, 1),
                        pltpu.roll(yn, HEAD_DIM // 2, 1))
        o_ref[:, cols] = (yn * cos + rot * sin).astype(o_ref.dtype)


def _qk_proj(xn, w_in, nw, cos, sin, seq, tm, tn):
    t, d = xn.shape
    n = nw.shape[1]
    tiles_per_seq = seq // tm
    return pl.pallas_call(
        _qk_kernel,
        grid=(t // tm, n // tn),
        in_specs=[pl.BlockSpec((tm, d), lambda i, j: (i, 0)),
                  pl.BlockSpec((d, tn), lambda i, j: (0, j)),
                  pl.BlockSpec((1, tn), lambda i, j: (0, j)),
                  pl.BlockSpec((tm, LANES), lambda i, j: (i % tiles_per_seq, 0)),
                  pl.BlockSpec((tm, LANES), lambda i, j: (i % tiles_per_seq, 0))],
        out_specs=pl.BlockSpec((tm, tn), lambda i, j: (i, j)),
        out_shape=jax.ShapeDtypeStruct((t, n), BF16),
        compiler_params=_params(("parallel", "parallel")),
        name="qk_proj",
    )(xn, w_in, nw, cos, sin)


def _v_kernel(xn_ref, w_ref, o_ref):
    o_ref[...] = _mm(xn_ref[...], w_ref[...]).astype(o_ref.dtype)


def _v_proj(xn, w_in, col0, n, tm, tn):
    t, d = xn.shape
    jb = col0 // tn
    return pl.pallas_call(
        _v_kernel,
        grid=(t // tm, n // tn),
        in_specs=[pl.BlockSpec((tm, d), lambda i, j: (i, 0)),
                  pl.BlockSpec((d, tn), lambda i, j: (0, jb + j))],
        out_specs=pl.BlockSpec((tm, tn), lambda i, j: (i, j)),
        out_shape=jax.ShapeDtypeStruct((t, n), BF16),
        compiler_params=_params(("parallel", "parallel")),
        name="v_proj",
    )(xn, w_in)


def _conv_kernel(xn_ref, wb_ref, wc_ref, wh_ref, cw_ref, o_ref, u_scr, carry_scr,
                 *, tiles_per_seq):
    i = pl.program_id(0)
    j = pl.program_id(1)
    tm = xn_ref.shape[0]
    xn = xn_ref[...]
    u = _mm(xn, wc_ref[...]) * _mm(xn, wh_ref[...])
    @pl.when(i % tiles_per_seq == 0)
    def _():
        u_scr[0:SUBLANES, :] = jnp.zeros((SUBLANES, u_scr.shape[1]), F32)

    @pl.when(i % tiles_per_seq != 0)
    def _():
        u_scr[0:SUBLANES, :] = carry_scr[j]

    u_scr[SUBLANES:, :] = u
    carry_scr[j] = u[tm - SUBLANES:, :]
    cw = cw_ref[...]
    y = (cw[2:3, :] * u
         + cw[1:2, :] * u_scr[SUBLANES - 1:SUBLANES - 1 + tm, :]
         + cw[0:1, :] * u_scr[SUBLANES - 2:SUBLANES - 2 + tm, :])
    o_ref[...] = (_mm(xn, wb_ref[...]) * y).astype(o_ref.dtype)


def _conv_proj(xn, w_in, conv_w, col_b, col_c, col_h, n, seq, tm, tn):
    t, d = xn.shape
    jb, jc, jh = col_b // tn, col_c // tn, col_h // tn
    nj = n // tn
    kern = functools.partial(_conv_kernel, tiles_per_seq=seq // tm)
    return pl.pallas_call(
        kern,
        grid=(t // tm, nj),
        in_specs=[pl.BlockSpec((tm, d), lambda i, j: (i, 0)),
                  pl.BlockSpec((d, tn), lambda i, j: (0, jb + j)),
                  pl.BlockSpec((d, tn), lambda i, j: (0, jc + j)),
                  pl.BlockSpec((d, tn), lambda i, j: (0, jh + j)),
                  pl.BlockSpec((CONV_K, tn), lambda i, j: (0, j))],
        out_specs=pl.BlockSpec((tm, tn), lambda i, j: (i, j)),
        out_shape=jax.ShapeDtypeStruct((t, n), BF16),
        scratch_shapes=[pltpu.VMEM((tm + SUBLANES, tn), F32),
                        pltpu.VMEM((nj, SUBLANES, tn), F32)],
        compiler_params=_params(("arbitrary", "arbitrary")),
        name="conv_proj",
    )(xn, w_in, w_in, w_in, conv_w)


def _attn_kernel(q_ref, k_ref, v_ref, lam_ref, sw_ref, o_ref,
                 qq_scr, m_scr, l_scr, acc_scr, *, tq, tk):
    qi = pl.program_id(2)
    lane = lax.broadcasted_iota(jnp.int32, (1, LANES), 1)
    q = q_ref[0]
    zero = jnp.zeros_like(q)
    qq_scr[0:tq, :] = jnp.where(lane < HEAD_DIM, q, zero)
    qq_scr[tq:, :] = jnp.where(lane < HEAD_DIM, zero, q)
    m_scr[...] = jnp.full(m_scr.shape, -jnp.inf, F32)
    l_scr[...] = jnp.zeros(l_scr.shape, F32)
    acc_scr[...] = jnp.zeros(acc_scr.shape, F32)

    def step(kj, masked):
        k = k_ref[0, pl.ds(kj * tk, tk), :]
        v = v_ref[0, pl.ds(kj * tk, tk), :]
        s = lax.dot_general(qq_scr[...], k, (((1,), (1,)), ((), ())),
                            preferred_element_type=F32)
        if masked:
            row = lax.broadcasted_iota(jnp.int32, s.shape, 0)
            col = lax.broadcasted_iota(jnp.int32, s.shape, 1)
            s = jnp.where(col // CHUNK <= (row % tq) // CHUNK, s, -1e30)
        m_prev = m_scr[...]
        m_new = jnp.maximum(m_prev, jnp.max(s, axis=-1, keepdims=True))
        alpha = jnp.exp(m_prev - m_new)
        p = jnp.concatenate(
            [jnp.exp(s[:, c * LANES:(c + 1) * LANES] - m_new)
             for c in range(tk // LANES)], axis=1)
        l_scr[...] = alpha * l_scr[...] + jnp.sum(p, axis=-1, keepdims=True)
        acc_scr[...] = alpha * acc_scr[...] + jnp.dot(
            p.astype(BF16), v, preferred_element_type=F32)
        m_scr[...] = m_new

    def body(kj, carry):
        step(kj, masked=False)
        return carry

    lax.fori_loop(0, qi, body, 0)
    step(qi, masked=True)

    lq1, lk1, lq2, lk2 = (lam_ref[r:r + 1, :] for r in range(4))
    lam = (jnp.exp(jnp.sum(lq1 * lk1, axis=-1, keepdims=True))
           - jnp.exp(jnp.sum(lq2 * lk2, axis=-1, keepdims=True)) + LAMBDA_INIT)
    o = acc_scr[...] / l_scr[...]
    a = o[0:tq, :] - lam * o[tq:, :]
    ms = jnp.mean(a * a, axis=-1, keepdims=True)
    o_ref[0] = (((a * lax.rsqrt(ms + SUBLN_EPS)) * sw_ref[...])
                * (1.0 - LAMBDA_INIT)).astype(o_ref.dtype)


def _attention(qk, v, lam_vecs, subln_w, tq, tk):
    b, s, w = v.shape
    h = w // V_DIM
    kern = functools.partial(_attn_kernel, tq=tq, tk=tk)
    return pl.pallas_call(
        kern,
        grid=(b, h, s // tq),
        in_specs=[pl.BlockSpec((1, tq, V_DIM), lambda bi, hi, qi: (bi, qi, hi)),
                  pl.BlockSpec((1, s, V_DIM), lambda bi, hi, qi: (bi, 0, h + hi)),
                  pl.BlockSpec((1, s, V_DIM), lambda bi, hi, qi: (bi, 0, hi)),
                  pl.BlockSpec((4, HEAD_DIM), lambda bi, hi, qi: (0, 0)),
                  pl.BlockSpec((1, V_DIM), lambda bi, hi, qi: (0, 0))],
        out_specs=pl.BlockSpec((1, tq, V_DIM), lambda bi, hi, qi: (bi, qi, hi)),
        out_shape=jax.ShapeDtypeStruct((b, s, w), BF16),
        scratch_shapes=[pltpu.VMEM((2 * tq, V_DIM), BF16),
                        pltpu.VMEM((2 * tq, LANES), F32),
                        pltpu.VMEM((2 * tq, LANES), F32),
                        pltpu.VMEM((2 * tq, V_DIM), F32)],
        compiler_params=_params(("parallel", "parallel", "arbitrary")),
        name="diff_attention",
    )(qk, qk, v, lam_vecs, subln_w)


def _out_kernel(a_ref, c_ref, wa_ref, wc_ref, x_ref, o_ref):
    o_ref[...] = (x_ref[...] + _mm(a_ref[...], wa_ref[...])
                  + _mm(c_ref[...], wc_ref[...]))


def _out_proj(attn, conv, w_out, x, tm, tn):
    t, ka = attn.shape
    kc = conv.shape[1]
    n = w_out.shape[1]
    assert ka == kc
    return pl.pallas_call(
        _out_kernel,
        grid=(t // tm, n // tn),
        in_specs=[pl.BlockSpec((tm, ka), lambda i, j: (i, 0)),
                  pl.BlockSpec((tm, kc), lambda i, j: (i, 0)),
                  pl.BlockSpec((ka, tn), lambda i, j: (0, j)),
                  pl.BlockSpec((kc, tn), lambda i, j: (1, j)),
                  pl.BlockSpec((tm, tn), lambda i, j: (i, j))],
        out_specs=pl.BlockSpec((tm, tn), lambda i, j: (i, j)),
        out_shape=jax.ShapeDtypeStruct((t, n), F32),
        compiler_params=_params(("parallel", "parallel")),
        name="out_proj",
    )(attn, conv, w_out, w_out, x)


def _gate_up_kernel(x_ref, wg_ref, wu_ref, o_ref):
    x = x_ref[...]
    g = _mm(x, wg_ref[...])
    u = _mm(x, wu_ref[...])
    o_ref[...] = ((g * jax.nn.sigmoid(g)) * u).astype(o_ref.dtype)


def _gate_up(fn, w_gate, w_up, tm, tn):
    t, d = fn.shape
    f = w_gate.shape[1]
    return pl.pallas_call(
        _gate_up_kernel,
        grid=(t // tm, f // tn),
        in_specs=[pl.BlockSpec((tm, d), lambda i, j: (i, 0)),
                  pl.BlockSpec((d, tn), lambda i, j: (0, j)),
                  pl.BlockSpec((d, tn), lambda i, j: (0, j))],
        out_specs=pl.BlockSpec((tm, tn), lambda i, j: (i, j)),
        out_shape=jax.ShapeDtypeStruct((t, f), BF16),
        compiler_params=_params(("parallel", "parallel")),
        name="gate_up",
    )(fn, w_gate, w_up)


def _down_kernel(h_ref, w_ref, r_ref, o_ref, acc_ref):
    kk = pl.program_id(2)

    @pl.when(kk == 0)
    def _():
        acc_ref[...] = r_ref[...]

    acc_ref[...] += _mm(h_ref[...], w_ref[...])

    @pl.when(kk == pl.num_programs(2) - 1)
    def _():
        o_ref[...] = acc_ref[...]


def _down(h, w_down, resid, tm, tn, tk):
    t, f = h.shape
    n = w_down.shape[1]
    return pl.pallas_call(
        _down_kernel,
        grid=(t // tm, n // tn, f // tk),
        in_specs=[pl.BlockSpec((tm, tk), lambda i, j, k: (i, k)),
                  pl.BlockSpec((tk, tn), lambda i, j, k: (k, j)),
                  pl.BlockSpec((tm, tn), lambda i, j, k: (i, j))],
        out_specs=pl.BlockSpec((tm, tn), lambda i, j, k: (i, j)),
        out_shape=jax.ShapeDtypeStruct((t, n), F32),
        scratch_shapes=[pltpu.VMEM((tm, tn), F32)],
        compiler_params=_params(("parallel", "parallel", "arbitrary")),
        name="down_proj",
    )(h, w_down, resid)


TILES = dict(
    norm_tm=512,
    proj_tm=1024, proj_tn=512,
    conv_tm=1024, conv_tn=256,
    attn_tq=256, attn_tk=256,
    out_tm=1024, out_tn=512,
    gu_tm=1024, gu_tn=512,
    down_tm=1024, down_tn=1024, down_tk=512,
)


def _rope_tables(seq):
    pos = jnp.arange(seq, dtype=F32)
    inv_freq = ROPE_THETA ** (-jnp.arange(0, HEAD_DIM, 2, dtype=F32) / HEAD_DIM)
    ang = pos[:, None] * inv_freq[None, :]
    cos, sin = jnp.cos(ang), jnp.sin(ang)
    reps = LANES // (HEAD_DIM // 2)
    cos_l = jnp.tile(cos, (1, reps))
    sin_l = jnp.tile(jnp.concatenate([-sin, sin], axis=1), (1, reps // 2))
    return cos_l, sin_l


def kernel(x, attn_norm_w, w_in, q_norm_w, k_norm_w, lambda_q1, lambda_k1,
           lambda_q2, lambda_k2, subln_w, conv_w, w_out, ffn_norm_w, w_gate,
           w_up, w_down):
    b, s, d = x.shape
    t = b * s
    attn_w = N_HEADS * V_DIM
    qk_cols = N_HEADS * 2 * HEAD_DIM
    conv_width = d - attn_w
    col_v = 2 * qk_cols
    col_b = col_v + attn_w
    col_c = col_b + conv_width
    col_h = col_c + conv_width
    tl = TILES
    assert w_in.shape[0] == 1, "single-layer block"

    x2 = x.reshape(t, d)
    cos_l, sin_l = _rope_tables(s)
    scale = 1.0 / math.sqrt(HEAD_DIM)
    nw = jnp.concatenate([jnp.tile(q_norm_w[0] * scale, qk_cols // HEAD_DIM),
                          jnp.tile(k_norm_w[0], qk_cols // HEAD_DIM)]).reshape(1, -1)
    lam_vecs = jnp.concatenate([lambda_q1, lambda_k1, lambda_q2, lambda_k2], axis=0)

    xn = _rmsnorm(x2, attn_norm_w[0], tl["norm_tm"])
    qk = _qk_proj(xn, w_in[0], nw, cos_l, sin_l, s, tl["proj_tm"], tl["proj_tn"])
    v = _v_proj(xn, w_in[0], col_v, attn_w, tl["proj_tm"], tl["proj_tn"])
    conv = _conv_proj(xn, w_in[0], conv_w[0], col_b, col_c, col_h, conv_width, s,
                      tl["conv_tm"], tl["conv_tn"])

    attn = _attention(qk.reshape(b, s, 2 * qk_cols), v.reshape(b, s, attn_w), lam_vecs,
                      subln_w[0].reshape(1, V_DIM), tl["attn_tq"], tl["attn_tk"])

    h1 = _out_proj(attn.reshape(t, attn_w), conv, w_out[0], x2,
                   tl["out_tm"], tl["out_tn"])
    fn = _rmsnorm(h1, ffn_norm_w[0], tl["norm_tm"])
    hid = _gate_up(fn, w_gate[0], w_up[0], tl["gu_tm"], tl["gu_tn"])
    out = _down(hid, w_down[0], h1, tl["down_tm"], tl["down_tn"], tl["down_tk"])
    return out.reshape(b, s, d)
```

```python
import functools
import math

import jax
import jax.numpy as jnp
from jax import lax
from jax.experimental import pallas as pl
from jax.experimental.pallas import tpu as pltpu

F32 = jnp.float32
BF16 = jnp.bfloat16

CHUNK = 64
HEAD_DIM = 64
V_DIM = 2 * HEAD_DIM
N_HEADS = 8
CONV_K = 3
ROPE_THETA = 10000.0
EPS = 1e-6
SUBLN_EPS = 1e-5
LAMBDA_INIT = 0.8 - 0.6 * math.exp(-0.3 * 0)

LANES = 128
SUBLANES = 8
VMEM_LIMIT = 56 * 1024 * 1024


def _params(semantics, flags=None):
    return pltpu.CompilerParams(dimension_semantics=semantics,
                                vmem_limit_bytes=VMEM_LIMIT, flags=flags)


def _mm(a_bf16, w_f32):
    return jnp.dot(a_bf16, w_f32.astype(BF16), preferred_element_type=F32)


def _rmsnorm_kernel(x_ref, w_ref, o_ref):
    x = x_ref[...]
    ms = jnp.mean(x * x, axis=-1, keepdims=True)
    o_ref[...] = ((x * lax.rsqrt(ms + EPS)) * w_ref[...]).astype(o_ref.dtype)


def _rmsnorm(x, w, tm):
    t, d = x.shape
    return pl.pallas_call(
        _rmsnorm_kernel,
        grid=(t // tm,),
        in_specs=[pl.BlockSpec((tm, d), lambda i: (i, 0)),
                  pl.BlockSpec((1, d), lambda i: (0, 0))],
        out_specs=pl.BlockSpec((tm, d), lambda i: (i, 0)),
        out_shape=jax.ShapeDtypeStruct((t, d), BF16),
        compiler_params=_params(("parallel",)),
        name="rmsnorm",
    )(x, w.reshape(1, d))


def _qk_kernel(xn_ref, w_ref, nw_ref, cos_ref, sin_ref, o_ref):
    acc = _mm(xn_ref[...], w_ref[...])
    lane = lax.broadcasted_iota(jnp.int32, (1, LANES), 1)
    lo_half = lane < HEAD_DIM
    first = (lane % HEAD_DIM) < (HEAD_DIM // 2)
    cos = cos_ref[...]
    sin = sin_ref[...]
    for c in range(acc.shape[1] // LANES):
        cols = slice(c * LANES, (c + 1) * LANES)
        y = acc[:, cols]
        sq = y * y
        s_lo = jnp.sum(jnp.where(lo_half, sq, 0.0), axis=-1, keepdims=True)
        s_hi = jnp.sum(jnp.where(lo_half, 0.0, sq), axis=-1, keepdims=True)
        ms = jnp.where(lo_half, s_lo, s_hi) * (1.0 / HEAD_DIM)
        yn = (y * lax.rsqrt(ms + EPS)) * nw_ref[:, cols]
        rot = jnp.where(first, pltpu.roll(yn, LANES - HEAD_DIM // 2, 1),
                        pltpu.roll(yn, HEAD_DIM // 2, 1))
        o_ref[:, cols] = (yn * cos + rot * sin).astype(o_ref.dtype)


def _qk_proj(xn, w_in, nw, cos, sin, seq, tm, tn):
    t, d = xn.shape
    n = nw.shape[1]
    tiles_per_seq = seq // tm
    return pl.pallas_call(
        _qk_kernel,
        grid=(t // tm, n // tn),
        in_specs=[pl.BlockSpec((tm, d), lambda i, j: (i, 0)),
                  pl.BlockSpec((d, tn), lambda i, j: (0, j)),
                  pl.BlockSpec((1, tn), lambda i, j: (0, j)),
                  pl.BlockSpec((tm, LANES), lambda i, j: (i % tiles_per_seq, 0)),
                  pl.BlockSpec((tm, LANES), lambda i, j: (i % tiles_per_seq, 0))],
        out_specs=pl.BlockSpec((tm, tn), lambda i, j: (i, j)),
        out_shape=jax.ShapeDtypeStruct((t, n), BF16),
        compiler_params=_params(("parallel", "parallel")),
        name="qk_proj",
    )(xn, w_in, nw, cos, sin)


def _v_kernel(xn_ref, w_ref, o_ref):
    o_ref[...] = _mm(xn_ref[...], w_ref[...]).astype(o_ref.dtype)


def _v_proj(xn, w_in, col0, n, tm, tn):
    t, d = xn.shape
    jb = col0 // tn
    return pl.pallas_call(
        _v_kernel,
        grid=(t // tm, n // tn),
        in_specs=[pl.BlockSpec((tm, d), lambda i, j: (i, 0)),
                  pl.BlockSpec((d, tn), lambda i, j: (0, jb + j))],
        out_specs=pl.BlockSpec((tm, tn), lambda i, j: (i, j)),
        out_shape=jax.ShapeDtypeStruct((t, n), BF16),
        compiler_params=_params(("parallel", "parallel")),
        name="v_proj",
    )(xn, w_in)


def _conv_kernel(xn_ref, wb_ref, wc_ref, wh_ref, cw_ref, o_ref, u_scr, carry_scr,
                 *, tiles_per_seq):
    i = pl.program_id(0)
    j = pl.program_id(1)
    tm = xn_ref.shape[0]
    xn = xn_ref[...]
    u = _mm(xn, wc_ref[...]) * _mm(xn, wh_ref[...])
    @pl.when(i % tiles_per_seq == 0)
    def _():
        u_scr[0:SUBLANES, :] = jnp.zeros((SUBLANES, u_scr.shape[1]), F32)

    @pl.when(i % tiles_per_seq != 0)
    def _():
        u_scr[0:SUBLANES, :] = carry_scr[j]

    u_scr[SUBLANES:, :] = u
    carry_scr[j] = u[tm - SUBLANES:, :]
    cw = cw_ref[...]
    y = (cw[2:3, :] * u
         + cw[1:2, :] * u_scr[SUBLANES - 1:SUBLANES - 1 + tm, :]
         + cw[0:1, :] * u_scr[SUBLANES - 2:SUBLANES - 2 + tm, :])
    o_ref[...] = (_mm(xn, wb_ref[...]) * y).astype(o_ref.dtype)


def _conv_proj(xn, w_in, conv_w, col_b, col_c, col_h, n, seq, tm, tn):
    t, d = xn.shape
    jb, jc, jh = col_b // tn, col_c // tn, col_h // tn
    nj = n // tn
    kern = functools.partial(_conv_kernel, tiles_per_seq=seq // tm)
    return pl.pallas_call(
        kern,
        grid=(t // tm, nj),
        in_specs=[pl.BlockSpec((tm, d), lambda i, j: (i, 0)),
                  pl.BlockSpec((d, tn), lambda i, j: (0, jb + j)),
                  pl.BlockSpec((d, tn), lambda i, j: (0, jc + j)),
                  pl.BlockSpec((d, tn), lambda i, j: (0, jh + j)),
                  pl.BlockSpec((CONV_K, tn), lambda i, j: (0, j))],
        out_specs=pl.BlockSpec((tm, tn), lambda i, j: (i, j)),
        out_shape=jax.ShapeDtypeStruct((t, n), BF16),
        scratch_shapes=[pltpu.VMEM((tm + SUBLANES, tn), F32),
                        pltpu.VMEM((nj, SUBLANES, tn), F32)],
        compiler_params=_params(("arbitrary", "arbitrary")),
        name="conv_proj",
    )(xn, w_in, w_in, w_in, conv_w)


def _attn_kernel(q_ref, k_ref, v_ref, lam_ref, sw_ref, o_ref,
                 qq_scr, vt_scr, m_scr, l_scr, acc_scr, sa_scr, sb_scr,
                 *, tq, tk, pw):
    qi = pl.program_id(2)
    seq = k_ref.shape[1]

    @pl.when(qi == 0)
    def _():
        for c in range(seq // LANES):
            blk = v_ref[0, c * LANES:(c + 1) * LANES, :].astype(F32)
            vt_scr[:, c * LANES:(c + 1) * LANES] = blk.T.astype(BF16)

    lane = lax.broadcasted_iota(jnp.int32, (1, LANES), 1)
    q = q_ref[0]
    zero = jnp.zeros_like(q)
    qq_scr[0:tq, :] = jnp.where(lane < HEAD_DIM, q, zero)
    qq_scr[tq:, :] = jnp.where(lane < HEAD_DIM, zero, q)
    m_scr[...] = jnp.full(m_scr.shape, -jnp.inf, F32)
    l_scr[...] = jnp.zeros(l_scr.shape, F32)
    acc_scr[...] = jnp.zeros(acc_scr.shape, F32)

    n_panels = 2 * tq // pw

    def scores(t, dst):
        k = k_ref[0, pl.ds(pl.multiple_of(t * tk, tk), tk), :]
        for c in range(n_panels):
            cols = slice(c * pw, (c + 1) * pw)
            dst[:, cols] = lax.dot_general(
                k, qq_scr[cols, :], (((1,), (1,)), ((), ())),
                preferred_element_type=F32)

    def update(t, src, diag_off=None):
        vt = vt_scr[:, pl.ds(pl.multiple_of(t * tk, tk), tk)]
        m_out, l_out, acc_out = [], [], []
        for c in range(n_panels):
            cols = slice(c * pw, (c + 1) * pw)
            s = src[:, cols]
            if diag_off is not None:
                key = lax.broadcasted_iota(jnp.int32, s.shape, 0) + diag_off
                qry = lax.broadcasted_iota(jnp.int32, s.shape, 1) + c * pw
                s = jnp.where(key // CHUNK <= (qry % tq) // CHUNK, s, -1e30)
            m_prev = m_scr[:, cols]
            m_new = jnp.maximum(m_prev, jnp.max(s, axis=0, keepdims=True))
            alpha = jnp.exp(m_prev - m_new)
            p = jnp.exp(s - m_new)
            m_out.append(m_new)
            l_out.append(alpha * l_scr[:, cols] + jnp.sum(p, axis=0, keepdims=True))
            acc_out.append(alpha * acc_scr[:, cols] + jnp.dot(
                vt, p.astype(BF16), preferred_element_type=F32))
        for c in range(n_panels):
            cols = slice(c * pw, (c + 1) * pw)
            m_scr[:, cols] = m_out[c]
            l_scr[:, cols] = l_out[c]
            acc_scr[:, cols] = acc_out[c]

    scores(0, sa_scr)

    def body(i, carry):
        scores(2 * i + 1, sb_scr)
        update(2 * i, sa_scr)
        scores(2 * i + 2, sa_scr)
        update(2 * i + 1, sb_scr)
        return carry

    lax.fori_loop(0, qi, body, 0)
    scores(2 * qi + 1, sb_scr)
    update(2 * qi, sa_scr, diag_off=0)
    update(2 * qi + 1, sb_scr, diag_off=tk)

    lq1, lk1, lq2, lk2 = (lam_ref[r:r + 1, :] for r in range(4))
    lam = (jnp.exp(jnp.sum(lq1 * lk1, axis=-1, keepdims=True))
           - jnp.exp(jnp.sum(lq2 * lk2, axis=-1, keepdims=True)) + LAMBDA_INIT)
    o = acc_scr[...] / l_scr[...]
    a = (o[:, 0:tq] - lam * o[:, tq:]).T
    ms = jnp.mean(a * a, axis=-1, keepdims=True)
    o_ref[0] = (((a * lax.rsqrt(ms + SUBLN_EPS)) * sw_ref[...])
                * (1.0 - LAMBDA_INIT)).astype(o_ref.dtype)


def _attention(qk, v, lam_vecs, subln_w, tq, tk, pw):
    b, s, w = v.shape
    h = w // V_DIM
    assert tq == 2 * tk, "the key-tile pipeline is written for pairs of key tiles"
    kern = functools.partial(_attn_kernel, tq=tq, tk=tk, pw=pw)
    return pl.pallas_call(
        kern,
        grid=(b, h, s // tq),
        in_specs=[pl.BlockSpec((1, tq, V_DIM), lambda bi, hi, qi: (bi, qi, hi)),
                  pl.BlockSpec((1, s, V_DIM), lambda bi, hi, qi: (bi, 0, h + hi)),
                  pl.BlockSpec((1, s, V_DIM), lambda bi, hi, qi: (bi, 0, hi)),
                  pl.BlockSpec((4, HEAD_DIM), lambda bi, hi, qi: (0, 0)),
                  pl.BlockSpec((1, V_DIM), lambda bi, hi, qi: (0, 0))],
        out_specs=pl.BlockSpec((1, tq, V_DIM), lambda bi, hi, qi: (bi, qi, hi)),
        out_shape=jax.ShapeDtypeStruct((b, s, w), BF16),
        scratch_shapes=[pltpu.VMEM((2 * tq, V_DIM), BF16),
                        pltpu.VMEM((V_DIM, s), BF16),
                        pltpu.VMEM((1, 2 * tq), F32),
                        pltpu.VMEM((1, 2 * tq), F32),
                        pltpu.VMEM((V_DIM, 2 * tq), F32),
                        pltpu.VMEM((tk, 2 * tq), F32),
                        pltpu.VMEM((tk, 2 * tq), F32)],
        compiler_params=_params(("parallel", "parallel", "arbitrary")),
        name="diff_attention",
    )(qk, qk, v, lam_vecs, subln_w)


def _out_kernel(a_ref, c_ref, wa_ref, wc_ref, x_ref, o_ref):
    o_ref[...] = (x_ref[...] + _mm(a_ref[...], wa_ref[...])
                  + _mm(c_ref[...], wc_ref[...]))


def _out_proj(attn, conv, w_out, x, tm, tn):
    t, ka = attn.shape
    kc = conv.shape[1]
    n = w_out.shape[1]
    assert ka == kc
    return pl.pallas_call(
        _out_kernel,
        grid=(t // tm, n // tn),
        in_specs=[pl.BlockSpec((tm, ka), lambda i, j: (i, 0)),
                  pl.BlockSpec((tm, kc), lambda i, j: (i, 0)),
                  pl.BlockSpec((ka, tn), lambda i, j: (0, j)),
                  pl.BlockSpec((kc, tn), lambda i, j: (1, j)),
                  pl.BlockSpec((tm, tn), lambda i, j: (i, j))],
        out_specs=pl.BlockSpec((tm, tn), lambda i, j: (i, j)),
        out_shape=jax.ShapeDtypeStruct((t, n), F32),
        compiler_params=_params(("parallel", "parallel")),
        name="out_proj",
    )(attn, conv, w_out, w_out, x)


def _gate_up_kernel(x_ref, wg_ref, wu_ref, o_ref):
    x = x_ref[...]
    g = _mm(x, wg_ref[...])
    u = _mm(x, wu_ref[...])
    o_ref[...] = ((g * jax.nn.sigmoid(g)) * u).astype(o_ref.dtype)


def _gate_up(fn, w_gate, w_up, tm, tn):
    t, d = fn.shape
    f = w_gate.shape[1]
    return pl.pallas_call(
        _gate_up_kernel,
        grid=(t // tm, f // tn),
        in_specs=[pl.BlockSpec((tm, d), lambda i, j: (i, 0)),
                  pl.BlockSpec((d, tn), lambda i, j: (0, j)),
                  pl.BlockSpec((d, tn), lambda i, j: (0, j))],
        out_specs=pl.BlockSpec((tm, tn), lambda i, j: (i, j)),
        out_shape=jax.ShapeDtypeStruct((t, f), BF16),
        compiler_params=_params(("parallel", "parallel")),
        name="gate_up",
    )(fn, w_gate, w_up)


def _down_kernel(h_ref, w_ref, r_ref, o_ref, acc_ref):
    kk = pl.program_id(2)

    @pl.when(kk == 0)
    def _():
        acc_ref[...] = r_ref[...]

    acc_ref[...] += _mm(h_ref[...], w_ref[...])

    @pl.when(kk == pl.num_programs(2) - 1)
    def _():
        o_ref[...] = acc_ref[...]


def _down(h, w_down, resid, tm, tn, tk):
    t, f = h.shape
    n = w_down.shape[1]
    return pl.pallas_call(
        _down_kernel,
        grid=(t // tm, n // tn, f // tk),
        in_specs=[pl.BlockSpec((tm, tk), lambda i, j, k: (i, k)),
                  pl.BlockSpec((tk, tn), lambda i, j, k: (k, j)),
                  pl.BlockSpec((tm, tn), lambda i, j, k: (i, j))],
        out_specs=pl.BlockSpec((tm, tn), lambda i, j, k: (i, j)),
        out_shape=jax.ShapeDtypeStruct((t, n), F32),
        scratch_shapes=[pltpu.VMEM((tm, tn), F32)],
        compiler_params=_params(("parallel", "parallel", "arbitrary")),
        name="down_proj",
    )(h, w_down, resid)


TILES = dict(
    norm_tm=512,
    proj_tm=1024, proj_tn=512,
    conv_tm=1024, conv_tn=256,
    attn_tq=512, attn_tk=256, attn_pw=256,
    out_tm=1024, out_tn=512,
    gu_tm=1024, gu_tn=512,
    down_tm=1024, down_tn=1024, down_tk=512,
)


def _rope_tables(seq):
    pos = jnp.arange(seq, dtype=F32)
    inv_freq = ROPE_THETA ** (-jnp.arange(0, HEAD_DIM, 2, dtype=F32) / HEAD_DIM)
    ang = pos[:, None] * inv_freq[None, :]
    cos, sin = jnp.cos(ang), jnp.sin(ang)
    reps = LANES // (HEAD_DIM // 2)
    cos_l = jnp.tile(cos, (1, reps))
    sin_l = jnp.tile(jnp.concatenate([-sin, sin], axis=1), (1, reps // 2))
    return cos_l, sin_l


def kernel(x, attn_norm_w, w_in, q_norm_w, k_norm_w, lambda_q1, lambda_k1,
           lambda_q2, lambda_k2, subln_w, conv_w, w_out, ffn_norm_w, w_gate,
           w_up, w_down):
    b, s, d = x.shape
    t = b * s
    attn_w = N_HEADS * V_DIM
    qk_cols = N_HEADS * 2 * HEAD_DIM
    conv_width = d - attn_w
    col_v = 2 * qk_cols
    col_b = col_v + attn_w
    col_c = col_b + conv_width
    col_h = col_c + conv_width
    tl = TILES
    assert w_in.shape[0] == 1, "single-layer block"

    x2 = x.reshape(t, d)
    cos_l, sin_l = _rope_tables(s)
    scale = 1.0 / math.sqrt(HEAD_DIM)
    nw = jnp.concatenate([jnp.tile(q_norm_w[0] * scale, qk_cols // HEAD_DIM),
                          jnp.tile(k_norm_w[0], qk_cols // HEAD_DIM)]).reshape(1, -1)
    lam_vecs = jnp.concatenate([lambda_q1, lambda_k1, lambda_q2, lambda_k2], axis=0)

    xn = _rmsnorm(x2, attn_norm_w[0], tl["norm_tm"])
    qk = _qk_proj(xn, w_in[0], nw, cos_l, sin_l, s, tl["proj_tm"], tl["proj_tn"])
    v = _v_proj(xn, w_in[0], col_v, attn_w, tl["proj_tm"], tl["proj_tn"])
    conv = _conv_proj(xn, w_in[0], conv_w[0], col_b, col_c, col_h, conv_width, s,
                      tl["conv_tm"], tl["conv_tn"])

    attn = _attention(qk.reshape(b, s, 2 * qk_cols), v.reshape(b, s, attn_w), lam_vecs,
                      subln_w[0].reshape(1, V_DIM), tl["attn_tq"], tl["attn_tk"],
                      tl["attn_pw"])

    h1 = _out_proj(attn.reshape(t, attn_w), conv, w_out[0], x2,
                   tl["out_tm"], tl["out_tn"])
    fn = _rmsnorm(h1, ffn_norm_w[0], tl["norm_tm"])
    hid = _gate_up(fn, w_gate[0], w_up[0], tl["gu_tm"], tl["gu_tn"])
    out = _down(hid, w_down[0], h1, tl["down_tm"], tl["down_tn"], tl["down_tk"])
    return out.reshape(b, s, d)
```

```python
import functools
import math

import jax
import jax.numpy as jnp
from jax import lax
from jax.experimental import pallas as pl
from jax.experimental.pallas import tpu as pltpu

F32 = jnp.float32
BF16 = jnp.bfloat16

CHUNK = 64
HEAD_DIM = 64
V_DIM = 2 * HEAD_DIM
N_HEADS = 8
CONV_K = 3
ROPE_THETA = 10000.0
EPS = 1e-6
SUBLN_EPS = 1e-5
LAMBDA_INIT = 0.8 - 0.6 * math.exp(-0.3 * 0)

LANES = 128
SUBLANES = 8
VMEM_LIMIT = 56 * 1024 * 1024


def _params(semantics, flags=None):
    return pltpu.CompilerParams(dimension_semantics=semantics,
                                vmem_limit_bytes=VMEM_LIMIT, flags=flags)


def _mm(a_bf16, w_f32):
    return jnp.dot(a_bf16, w_f32.astype(BF16), preferred_element_type=F32)


def _rmsnorm_kernel(x_ref, w_ref, o_ref):
    x = x_ref[...]
    ms = jnp.mean(x * x, axis=-1, keepdims=True)
    o_ref[...] = ((x * lax.rsqrt(ms + EPS)) * w_ref[...]).astype(o_ref.dtype)


def _rmsnorm(x, w, tm):
    t, d = x.shape
    return pl.pallas_call(
        _rmsnorm_kernel,
        grid=(t // tm,),
        in_specs=[pl.BlockSpec((tm, d), lambda i: (i, 0)),
                  pl.BlockSpec((1, d), lambda i: (0, 0))],
        out_specs=pl.BlockSpec((tm, d), lambda i: (i, 0)),
        out_shape=jax.ShapeDtypeStruct((t, d), BF16),
        compiler_params=_params(("parallel",)),
        name="rmsnorm",
    )(x, w.reshape(1, d))


def _row_block_pipeline(n_blocks, matmul, epilogue):
    acc = matmul(0)
    for r in range(n_blocks):
        nxt = matmul(r + 1) if r + 1 < n_blocks else None
        epilogue(r, acc)
        acc = nxt


def _qk_kernel(xn_ref, w_ref, nw_ref, cos_ref, sin_ref, o_ref, w_scr, *, rb):
    w_scr[...] = w_ref[...].astype(BF16)
    tm, tn = o_ref.shape
    lane = lax.broadcasted_iota(jnp.int32, (1, LANES), 1)
    lo_half = lane < HEAD_DIM
    first = (lane % HEAD_DIM) < (HEAD_DIM // 2)

    def matmul(r):
        return jnp.dot(xn_ref[r * rb:(r + 1) * rb, :], w_scr[...],
                       preferred_element_type=F32)

    def epilogue(r, acc):
        rows = slice(r * rb, (r + 1) * rb)
        cos = cos_ref[rows, :]
        sin = sin_ref[rows, :]
        for c in range(tn // LANES):
            cols = slice(c * LANES, (c + 1) * LANES)
            y = acc[:, cols]
            sq = y * y
            s_lo = jnp.sum(jnp.where(lo_half, sq, 0.0), axis=-1, keepdims=True)
            s_hi = jnp.sum(jnp.where(lo_half, 0.0, sq), axis=-1, keepdims=True)
            ms = jnp.where(lo_half, s_lo, s_hi) * (1.0 / HEAD_DIM)
            yn = (y * lax.rsqrt(ms + EPS)) * nw_ref[:, cols]
            rot = jnp.where(first, pltpu.roll(yn, LANES - HEAD_DIM // 2, 1),
                            pltpu.roll(yn, HEAD_DIM // 2, 1))
            o_ref[rows, cols] = (yn * cos + rot * sin).astype(o_ref.dtype)

    _row_block_pipeline(tm // rb, matmul, epilogue)


def _qk_proj(xn, w_in, nw, cos, sin, seq, tm, tn, rb):
    t, d = xn.shape
    n = nw.shape[1]
    tiles_per_seq = seq // tm
    return pl.pallas_call(
        functools.partial(_qk_kernel, rb=rb),
        grid=(t // tm, n // tn),
        scratch_shapes=[pltpu.VMEM((d, tn), BF16)],
        in_specs=[pl.BlockSpec((tm, d), lambda i, j: (i, 0)),
                  pl.BlockSpec((d, tn), lambda i, j: (0, j)),
                  pl.BlockSpec((1, tn), lambda i, j: (0, j)),
                  pl.BlockSpec((tm, LANES), lambda i, j: (i % tiles_per_seq, 0)),
                  pl.BlockSpec((tm, LANES), lambda i, j: (i % tiles_per_seq, 0))],
        out_specs=pl.BlockSpec((tm, tn), lambda i, j: (i, j)),
        out_shape=jax.ShapeDtypeStruct((t, n), BF16),
        compiler_params=_params(("parallel", "parallel")),
        name="qk_proj",
    )(xn, w_in, nw, cos, sin)


def _v_kernel(xn_ref, w_ref, o_ref):
    o_ref[...] = _mm(xn_ref[...], w_ref[...]).astype(o_ref.dtype)


def _v_proj(xn, w_in, col0, n, tm, tn):
    t, d = xn.shape
    jb = col0 // tn
    return pl.pallas_call(
        _v_kernel,
        grid=(t // tm, n // tn),
        in_specs=[pl.BlockSpec((tm, d), lambda i, j: (i, 0)),
                  pl.BlockSpec((d, tn), lambda i, j: (0, jb + j))],
        out_specs=pl.BlockSpec((tm, tn), lambda i, j: (i, j)),
        out_shape=jax.ShapeDtypeStruct((t, n), BF16),
        compiler_params=_params(("parallel", "parallel")),
        name="v_proj",
    )(xn, w_in)


def _conv_kernel(xn_ref, wb_ref, wc_ref, wh_ref, cw_ref, o_ref,
                 wb_scr, wc_scr, wh_scr, u_scr, carry_scr, *, tiles_per_seq, rb):
    i = pl.program_id(0)
    j = pl.program_id(1)
    tm = xn_ref.shape[0]
    wc_scr[...] = wc_ref[...].astype(BF16)
    wh_scr[...] = wh_ref[...].astype(BF16)
    wb_scr[...] = wb_ref[...].astype(BF16)

    @pl.when(i % tiles_per_seq == 0)
    def _():
        u_scr[0:SUBLANES, :] = jnp.zeros((SUBLANES, u_scr.shape[1]), F32)

    @pl.when(i % tiles_per_seq != 0)
    def _():
        u_scr[0:SUBLANES, :] = carry_scr[j]

    def xn_rows(r):
        return xn_ref[r * rb:(r + 1) * rb, :]

    def gate_matmuls(r):
        return (jnp.dot(xn_rows(r), wc_scr[...], preferred_element_type=F32),
                jnp.dot(xn_rows(r), wh_scr[...], preferred_element_type=F32))

    def store_u(r, acc):
        u_scr[SUBLANES + r * rb:SUBLANES + (r + 1) * rb, :] = acc[0] * acc[1]

    _row_block_pipeline(tm // rb, gate_matmuls, store_u)
    carry_scr[j] = u_scr[tm:tm + SUBLANES, :]
    cw = cw_ref[...]

    def b_matmul(r):
        return jnp.dot(xn_rows(r), wb_scr[...], preferred_element_type=F32)

    def conv_out(r, gate_b):
        def shifted(back):
            start = SUBLANES + r * rb - back
            return u_scr[start:start + rb, :]
        y = cw[2:3, :] * shifted(0) + cw[1:2, :] * shifted(1) + cw[0:1, :] * shifted(2)
        o_ref[r * rb:(r + 1) * rb, :] = (gate_b * y).astype(o_ref.dtype)

    _row_block_pipeline(tm // rb, b_matmul, conv_out)


def _conv_proj(xn, w_in, conv_w, col_b, col_c, col_h, n, seq, tm, tn, rb):
    t, d = xn.shape
    jb, jc, jh = col_b // tn, col_c // tn, col_h // tn
    nj = n // tn
    kern = functools.partial(_conv_kernel, tiles_per_seq=seq // tm, rb=rb)
    w_scratch = [pltpu.VMEM((d, tn), BF16)] * 3
    return pl.pallas_call(
        kern,
        grid=(t // tm, nj),
        in_specs=[pl.BlockSpec((tm, d), lambda i, j: (i, 0)),
                  pl.BlockSpec((d, tn), lambda i, j: (0, jb + j)),
                  pl.BlockSpec((d, tn), lambda i, j: (0, jc + j)),
                  pl.BlockSpec((d, tn), lambda i, j: (0, jh + j)),
                  pl.BlockSpec((CONV_K, tn), lambda i, j: (0, j))],
        out_specs=pl.BlockSpec((tm, tn), lambda i, j: (i, j)),
        out_shape=jax.ShapeDtypeStruct((t, n), BF16),
        scratch_shapes=w_scratch + [pltpu.VMEM((tm + SUBLANES, tn), F32),
                                    pltpu.VMEM((nj, SUBLANES, tn), F32)],
        compiler_params=_params(("arbitrary", "arbitrary")),
        name="conv_proj",
    )(xn, w_in, w_in, w_in, conv_w)


def _attn_kernel(q_ref, k_ref, v_ref, lam_ref, sw_ref, o_ref,
                 qq_scr, vt_scr, m_scr, l_scr, acc_scr, sa_scr, sb_scr,
                 *, tq, tk, pw):
    qi = pl.program_id(2)
    seq = k_ref.shape[1]

    @pl.when(qi == 0)
    def _():
        for c in range(seq // LANES):
            blk = v_ref[0, c * LANES:(c + 1) * LANES, :].astype(F32)
            vt_scr[:, c * LANES:(c + 1) * LANES] = blk.T.astype(BF16)

    lane = lax.broadcasted_iota(jnp.int32, (1, LANES), 1)
    q = q_ref[0]
    zero = jnp.zeros_like(q)
    qq_scr[0:tq, :] = jnp.where(lane < HEAD_DIM, q, zero)
    qq_scr[tq:, :] = jnp.where(lane < HEAD_DIM, zero, q)
    m_scr[...] = jnp.full(m_scr.shape, -jnp.inf, F32)
    l_scr[...] = jnp.zeros(l_scr.shape, F32)
    acc_scr[...] = jnp.zeros(acc_scr.shape, F32)

    n_panels = 2 * tq // pw

    def scores(t, dst):
        k = k_ref[0, pl.ds(pl.multiple_of(t * tk, tk), tk), :]
        for c in range(n_panels):
            cols = slice(c * pw, (c + 1) * pw)
            dst[:, cols] = lax.dot_general(
                k, qq_scr[cols, :], (((1,), (1,)), ((), ())),
                preferred_element_type=F32)

    def update(t, src, diag_off=None):
        vt = vt_scr[:, pl.ds(pl.multiple_of(t * tk, tk), tk)]
        m_out, l_out, acc_out = [], [], []
        for c in range(n_panels):
            cols = slice(c * pw, (c + 1) * pw)
            s = src[:, cols]
            if diag_off is not None:
                key = lax.broadcasted_iota(jnp.int32, s.shape, 0) + diag_off
                qry = lax.broadcasted_iota(jnp.int32, s.shape, 1) + c * pw
                s = jnp.where(key // CHUNK <= (qry % tq) // CHUNK, s, -1e30)
            m_prev = m_scr[:, cols]
            m_new = jnp.maximum(m_prev, jnp.max(s, axis=0, keepdims=True))
            alpha = jnp.exp(m_prev - m_new)
            p = jnp.exp(s - m_new)
            m_out.append(m_new)
            l_out.append(alpha * l_scr[:, cols] + jnp.sum(p, axis=0, keepdims=True))
            acc_out.append(alpha * acc_scr[:, cols] + jnp.dot(
                vt, p.astype(BF16), preferred_element_type=F32))
        for c in range(n_panels):
            cols = slice(c * pw, (c + 1) * pw)
            m_scr[:, cols] = m_out[c]
            l_scr[:, cols] = l_out[c]
            acc_scr[:, cols] = acc_out[c]

    scores(0, sa_scr)

    def body(i, carry):
        scores(2 * i + 1, sb_scr)
        update(2 * i, sa_scr)
        scores(2 * i + 2, sa_scr)
        update(2 * i + 1, sb_scr)
        return carry

    lax.fori_loop(0, qi, body, 0)
    scores(2 * qi + 1, sb_scr)
    update(2 * qi, sa_scr, diag_off=0)
    update(2 * qi + 1, sb_scr, diag_off=tk)

    lq1, lk1, lq2, lk2 = (lam_ref[r:r + 1, :] for r in range(4))
    lam = (jnp.exp(jnp.sum(lq1 * lk1, axis=-1, keepdims=True))
           - jnp.exp(jnp.sum(lq2 * lk2, axis=-1, keepdims=True)) + LAMBDA_INIT)
    o = acc_scr[...] / l_scr[...]
    a = (o[:, 0:tq] - lam * o[:, tq:]).T
    ms = jnp.mean(a * a, axis=-1, keepdims=True)
    o_ref[0] = (((a * lax.rsqrt(ms + SUBLN_EPS)) * sw_ref[...])
                * (1.0 - LAMBDA_INIT)).astype(o_ref.dtype)


def _attention(qk, v, lam_vecs, subln_w, tq, tk, pw):
    b, s, w = v.shape
    h = w // V_DIM
    assert tq == 2 * tk, "the key-tile pipeline is written for pairs of key tiles"
    kern = functools.partial(_attn_kernel, tq=tq, tk=tk, pw=pw)
    return pl.pallas_call(
        kern,
        grid=(b, h, s // tq),
        in_specs=[pl.BlockSpec((1, tq, V_DIM), lambda bi, hi, qi: (bi, qi, hi)),
                  pl.BlockSpec((1, s, V_DIM), lambda bi, hi, qi: (bi, 0, h + hi)),
                  pl.BlockSpec((1, s, V_DIM), lambda bi, hi, qi: (bi, 0, hi)),
                  pl.BlockSpec((4, HEAD_DIM), lambda bi, hi, qi: (0, 0)),
                  pl.BlockSpec((1, V_DIM), lambda bi, hi, qi: (0, 0))],
        out_specs=pl.BlockSpec((1, tq, V_DIM), lambda bi, hi, qi: (bi, qi, hi)),
        out_shape=jax.ShapeDtypeStruct((b, s, w), BF16),
        scratch_shapes=[pltpu.VMEM((2 * tq, V_DIM), BF16),
                        pltpu.VMEM((V_DIM, s), BF16),
                        pltpu.VMEM((1, 2 * tq), F32),
                        pltpu.VMEM((1, 2 * tq), F32),
                        pltpu.VMEM((V_DIM, 2 * tq), F32),
                        pltpu.VMEM((tk, 2 * tq), F32),
                        pltpu.VMEM((tk, 2 * tq), F32)],
        compiler_params=_params(("parallel", "parallel", "arbitrary")),
        name="diff_attention",
    )(qk, qk, v, lam_vecs, subln_w)


def _out_kernel(a_ref, c_ref, wa_ref, wc_ref, x_ref, o_ref):
    o_ref[...] = (x_ref[...] + _mm(a_ref[...], wa_ref[...])
                  + _mm(c_ref[...], wc_ref[...]))


def _out_proj(attn, conv, w_out, x, tm, tn):
    t, ka = attn.shape
    kc = conv.shape[1]
    n = w_out.shape[1]
    assert ka == kc
    return pl.pallas_call(
        _out_kernel,
        grid=(t // tm, n // tn),
        in_specs=[pl.BlockSpec((tm, ka), lambda i, j: (i, 0)),
                  pl.BlockSpec((tm, kc), lambda i, j: (i, 0)),
                  pl.BlockSpec((ka, tn), lambda i, j: (0, j)),
                  pl.BlockSpec((kc, tn), lambda i, j: (1, j)),
                  pl.BlockSpec((tm, tn), lambda i, j: (i, j))],
        out_specs=pl.BlockSpec((tm, tn), lambda i, j: (i, j)),
        out_shape=jax.ShapeDtypeStruct((t, n), F32),
        compiler_params=_params(("parallel", "parallel")),
        name="out_proj",
    )(attn, conv, w_out, w_out, x)


def _gate_up_kernel(x_ref, wg_ref, wu_ref, o_ref):
    x = x_ref[...]
    g = _mm(x, wg_ref[...])
    u = _mm(x, wu_ref[...])
    o_ref[...] = ((g * jax.nn.sigmoid(g)) * u).astype(o_ref.dtype)


def _gate_up(fn, w_gate, w_up, tm, tn):
    t, d = fn.shape
    f = w_gate.shape[1]
    return pl.pallas_call(
        _gate_up_kernel,
        grid=(t // tm, f // tn),
        in_specs=[pl.BlockSpec((tm, d), lambda i, j: (i, 0)),
                  pl.BlockSpec((d, tn), lambda i, j: (0, j)),
                  pl.BlockSpec((d, tn), lambda i, j: (0, j))],
        out_specs=pl.BlockSpec((tm, tn), lambda i, j: (i, j)),
        out_shape=jax.ShapeDtypeStruct((t, f), BF16),
        compiler_params=_params(("parallel", "parallel")),
        name="gate_up",
    )(fn, w_gate, w_up)


def _down_kernel(h_ref, w_ref, r_ref, o_ref):
    o_ref[...] = r_ref[...] + _mm(h_ref[...], w_ref[...])


def _down(h, w_down, resid, tm, tn):
    t, f = h.shape
    n = w_down.shape[1]
    return pl.pallas_call(
        _down_kernel,
        grid=(t // tm, n // tn),
        in_specs=[pl.BlockSpec((tm, f), lambda i, j: (i, 0)),
                  pl.BlockSpec((f, tn), lambda i, j: (0, j)),
                  pl.BlockSpec((tm, tn), lambda i, j: (i, j))],
        out_specs=pl.BlockSpec((tm, tn), lambda i, j: (i, j)),
        out_shape=jax.ShapeDtypeStruct((t, n), F32),
        compiler_params=_params(("parallel", "parallel")),
        name="down_proj",
    )(h, w_down, resid)


TILES = dict(
    norm_tm=512,
    proj_tm=1024, proj_tn=512, proj_rb=256,
    conv_tm=1024, conv_tn=512, conv_rb=256,
    attn_tq=512, attn_tk=256, attn_pw=256,
    out_tm=1024, out_tn=512,
    gu_tm=1024, gu_tn=512,
    down_tm=1024, down_tn=256,
)


def _rope_tables(seq):
    pos = jnp.arange(seq, dtype=F32)
    inv_freq = ROPE_THETA ** (-jnp.arange(0, HEAD_DIM, 2, dtype=F32) / HEAD_DIM)
    ang = pos[:, None] * inv_freq[None, :]
    cos, sin = jnp.cos(ang), jnp.sin(ang)
    reps = LANES // (HEAD_DIM // 2)
    cos_l = jnp.tile(cos, (1, reps))
    sin_l = jnp.tile(jnp.concatenate([-sin, sin], axis=1), (1, reps // 2))
    return cos_l, sin_l


def kernel(x, attn_norm_w, w_in, q_norm_w, k_norm_w, lambda_q1, lambda_k1,
           lambda_q2, lambda_k2, subln_w, conv_w, w_out, ffn_norm_w, w_gate,
           w_up, w_down):
    b, s, d = x.shape
    t = b * s
    attn_w = N_HEADS * V_DIM
    qk_cols = N_HEADS * 2 * HEAD_DIM
    conv_width = d - attn_w
    col_v = 2 * qk_cols
    col_b = col_v + attn_w
    col_c = col_b + conv_width
    col_h = col_c + conv_width
    tl = TILES
    assert w_in.shape[0] == 1, "single-layer block"

    x2 = x.reshape(t, d)
    cos_l, sin_l = _rope_tables(s)
    scale = 1.0 / math.sqrt(HEAD_DIM)
    nw = jnp.concatenate([jnp.tile(q_norm_w[0] * scale, qk_cols // HEAD_DIM),
                          jnp.tile(k_norm_w[0], qk_cols // HEAD_DIM)]).reshape(1, -1)
    lam_vecs = jnp.concatenate([lambda_q1, lambda_k1, lambda_q2, lambda_k2], axis=0)

    xn = _rmsnorm(x2, attn_norm_w[0], tl["norm_tm"])
    qk = _qk_proj(xn, w_in[0], nw, cos_l, sin_l, s, tl["proj_tm"], tl["proj_tn"],
                  tl["proj_rb"])
    v = _v_proj(xn, w_in[0], col_v, attn_w, tl["proj_tm"], tl["proj_tn"])
    conv = _conv_proj(xn, w_in[0], conv_w[0], col_b, col_c, col_h, conv_width, s,
                      tl["conv_tm"], tl["conv_tn"], tl["conv_rb"])

    attn = _attention(qk.reshape(b, s, 2 * qk_cols), v.reshape(b, s, attn_w), lam_vecs,
                      subln_w[0].reshape(1, V_DIM), tl["attn_tq"], tl["attn_tk"],
                      tl["attn_pw"])

    h1 = _out_proj(attn.reshape(t, attn_w), conv, w_out[0], x2,
                   tl["out_tm"], tl["out_tn"])
    fn = _rmsnorm(h1, ffn_norm_w[0], tl["norm_tm"])
    hid = _gate_up(fn, w_gate[0], w_up[0], tl["gu_tm"], tl["gu_tn"])
    out = _down(hid, w_down[0], h1, tl["down_tm"], tl["down_tn"])
    return out.reshape(b, s, d)
```

```python
import functools
import math

import jax
import jax.numpy as jnp
from jax import lax
from jax.experimental import pallas as pl
from jax.experimental.pallas import tpu as pltpu

F32 = jnp.float32
BF16 = jnp.bfloat16

CHUNK = 64
HEAD_DIM = 64
V_DIM = 2 * HEAD_DIM
N_HEADS = 8
CONV_K = 3
ROPE_THETA = 10000.0
EPS = 1e-6
SUBLN_EPS = 1e-5
LAMBDA_INIT = 0.8 - 0.6 * math.exp(-0.3 * 0)

LANES = 128
SUBLANES = 8
BF16_SUBLANES = 16
VMEM_LIMIT = 56 * 1024 * 1024


def _params(semantics, flags=None):
    return pltpu.CompilerParams(dimension_semantics=semantics,
                                vmem_limit_bytes=VMEM_LIMIT, flags=flags)


def _mm(a_bf16, w_f32):
    return jnp.dot(a_bf16, w_f32.astype(BF16), preferred_element_type=F32)


def _rmsnorm_kernel(x_ref, w_ref, o_ref):
    x = x_ref[...]
    ms = jnp.mean(x * x, axis=-1, keepdims=True)
    o_ref[...] = ((x * lax.rsqrt(ms + EPS)) * w_ref[...]).astype(o_ref.dtype)


def _rmsnorm(x, w, tm):
    t, d = x.shape
    return pl.pallas_call(
        _rmsnorm_kernel,
        grid=(t // tm,),
        in_specs=[pl.BlockSpec((tm, d), lambda i: (i, 0)),
                  pl.BlockSpec((1, d), lambda i: (0, 0))],
        out_specs=pl.BlockSpec((tm, d), lambda i: (i, 0)),
        out_shape=jax.ShapeDtypeStruct((t, d), BF16),
        compiler_params=_params(("parallel",)),
        name="rmsnorm",
    )(x, w.reshape(1, d))


def _row_block_pipeline(n_blocks, matmul, epilogue):
    acc = matmul(0)
    for r in range(n_blocks):
        nxt = matmul(r + 1) if r + 1 < n_blocks else None
        epilogue(r, acc)
        acc = nxt


def _qk_kernel(xn_ref, w_ref, nw_ref, cos_ref, sin_ref, o_ref, w_scr, *, rb):
    w_scr[...] = w_ref[...].astype(BF16)
    tm, tn = o_ref.shape
    lane = lax.broadcasted_iota(jnp.int32, (1, LANES), 1)
    lo_half = lane < HEAD_DIM
    first = (lane % HEAD_DIM) < (HEAD_DIM // 2)

    def matmul(r):
        return jnp.dot(xn_ref[r * rb:(r + 1) * rb, :], w_scr[...],
                       preferred_element_type=F32)

    def epilogue(r, acc):
        rows = slice(r * rb, (r + 1) * rb)
        cos = cos_ref[rows, :]
        sin = sin_ref[rows, :]
        for c in range(tn // LANES):
            cols = slice(c * LANES, (c + 1) * LANES)
            y = acc[:, cols]
            sq = y * y
            s_lo = jnp.sum(jnp.where(lo_half, sq, 0.0), axis=-1, keepdims=True)
            s_hi = jnp.sum(jnp.where(lo_half, 0.0, sq), axis=-1, keepdims=True)
            ms = jnp.where(lo_half, s_lo, s_hi) * (1.0 / HEAD_DIM)
            yn = (y * lax.rsqrt(ms + EPS)) * nw_ref[:, cols]
            rot = jnp.where(first, pltpu.roll(yn, LANES - HEAD_DIM // 2, 1),
                            pltpu.roll(yn, HEAD_DIM // 2, 1))
            o_ref[rows, cols] = (yn * cos + rot * sin).astype(o_ref.dtype)

    _row_block_pipeline(tm // rb, matmul, epilogue)


def _qk_proj(xn, w_in, nw, cos, sin, seq, tm, tn, rb):
    t, d = xn.shape
    n = nw.shape[1]
    tiles_per_seq = seq // tm
    return pl.pallas_call(
        functools.partial(_qk_kernel, rb=rb),
        grid=(t // tm, n // tn),
        scratch_shapes=[pltpu.VMEM((d, tn), BF16)],
        in_specs=[pl.BlockSpec((tm, d), lambda i, j: (i, 0)),
                  pl.BlockSpec((d, tn), lambda i, j: (0, j)),
                  pl.BlockSpec((1, tn), lambda i, j: (0, j)),
                  pl.BlockSpec((tm, LANES), lambda i, j: (i % tiles_per_seq, 0)),
                  pl.BlockSpec((tm, LANES), lambda i, j: (i % tiles_per_seq, 0))],
        out_specs=pl.BlockSpec((tm, tn), lambda i, j: (i, j)),
        out_shape=jax.ShapeDtypeStruct((t, n), BF16),
        compiler_params=_params(("parallel", "parallel")),
        name="qk_proj",
    )(xn, w_in, nw, cos, sin)


def _v_kernel(xn_ref, w_ref, o_ref):
    o_ref[...] = _mm(xn_ref[...], w_ref[...]).astype(o_ref.dtype)


def _v_proj(xn, w_in, col0, n, tm, tn):
    t, d = xn.shape
    jb = col0 // tn
    return pl.pallas_call(
        _v_kernel,
        grid=(t // tm, n // tn),
        in_specs=[pl.BlockSpec((tm, d), lambda i, j: (i, 0)),
                  pl.BlockSpec((d, tn), lambda i, j: (0, jb + j))],
        out_specs=pl.BlockSpec((tm, tn), lambda i, j: (i, j)),
        out_shape=jax.ShapeDtypeStruct((t, n), BF16),
        compiler_params=_params(("parallel", "parallel")),
        name="v_proj",
    )(xn, w_in)


def _conv_kernel(xn_ref, wb_ref, wc_ref, wh_ref, cw_ref, o_ref,
                 wb_scr, wc_scr, wh_scr, u_scr, carry_scr, *, tiles_per_seq, rb):
    i = pl.program_id(0)
    j = pl.program_id(1)
    tm = xn_ref.shape[0]
    wc_scr[...] = wc_ref[...].astype(BF16)
    wh_scr[...] = wh_ref[...].astype(BF16)
    wb_scr[...] = wb_ref[...].astype(BF16)

    @pl.when(i % tiles_per_seq == 0)
    def _():
        u_scr[0:SUBLANES, :] = jnp.zeros((SUBLANES, u_scr.shape[1]), F32)

    @pl.when(i % tiles_per_seq != 0)
    def _():
        u_scr[0:SUBLANES, :] = carry_scr[j]

    def xn_rows(r):
        return xn_ref[r * rb:(r + 1) * rb, :]

    def gate_matmuls(r):
        return (jnp.dot(xn_rows(r), wc_scr[...], preferred_element_type=F32),
                jnp.dot(xn_rows(r), wh_scr[...], preferred_element_type=F32))

    def store_u(r, acc):
        u_scr[SUBLANES + r * rb:SUBLANES + (r + 1) * rb, :] = acc[0] * acc[1]

    _row_block_pipeline(tm // rb, gate_matmuls, store_u)
    carry_scr[j] = u_scr[tm:tm + SUBLANES, :]
    cw = cw_ref[...]

    def b_matmul(r):
        return jnp.dot(xn_rows(r), wb_scr[...], preferred_element_type=F32)

    def conv_out(r, gate_b):
        def shifted(back):
            start = SUBLANES + r * rb - back
            return u_scr[start:start + rb, :]
        y = cw[2:3, :] * shifted(0) + cw[1:2, :] * shifted(1) + cw[0:1, :] * shifted(2)
        o_ref[r * rb:(r + 1) * rb, :] = (gate_b * y).astype(o_ref.dtype)

    _row_block_pipeline(tm // rb, b_matmul, conv_out)


def _conv_proj(xn, w_in, conv_w, col_b, col_c, col_h, n, seq, tm, tn, rb):
    t, d = xn.shape
    jb, jc, jh = col_b // tn, col_c // tn, col_h // tn
    nj = n // tn
    kern = functools.partial(_conv_kernel, tiles_per_seq=seq // tm, rb=rb)
    w_scratch = [pltpu.VMEM((d, tn), BF16)] * 3
    return pl.pallas_call(
        kern,
        grid=(t // tm, nj),
        in_specs=[pl.BlockSpec((tm, d), lambda i, j: (i, 0)),
                  pl.BlockSpec((d, tn), lambda i, j: (0, jb + j)),
                  pl.BlockSpec((d, tn), lambda i, j: (0, jc + j)),
                  pl.BlockSpec((d, tn), lambda i, j: (0, jh + j)),
                  pl.BlockSpec((CONV_K, tn), lambda i, j: (0, j))],
        out_specs=pl.BlockSpec((tm, tn), lambda i, j: (i, j)),
        out_shape=jax.ShapeDtypeStruct((t, n), BF16),
        scratch_shapes=w_scratch + [pltpu.VMEM((tm + SUBLANES, tn), F32),
                                    pltpu.VMEM((nj, SUBLANES, tn), F32)],
        compiler_params=_params(("arbitrary", "arbitrary")),
        name="conv_proj",
    )(xn, w_in, w_in, w_in, conv_w)


def _attn_kernel(q_ref, k_ref, v_ref, lam_ref, sw_ref, o_ref,
                 qq_scr, vt_scr, bias_scr, m_scr, acc_scr, sa_scr, sb_scr,
                 pa_scr, pb_scr, ala_scr, alb_scr, *, tq, tk, pw):
    qi = pl.program_id(2)
    seq = k_ref.shape[1]

    @pl.when(qi == 0)
    def _():
        for c in range(seq // LANES):
            blk = v_ref[0, c * LANES:(c + 1) * LANES, :].astype(F32)
            vt_scr[0:V_DIM, c * LANES:(c + 1) * LANES] = blk.T.astype(BF16)
        vt_scr[V_DIM:, :] = jnp.ones((vt_scr.shape[0] - V_DIM, seq), BF16)
        key = lax.broadcasted_iota(jnp.int32, bias_scr.shape, 0)
        qry = lax.broadcasted_iota(jnp.int32, bias_scr.shape, 1)
        bias_scr[...] = jnp.where(key // CHUNK <= qry // CHUNK, 0.0, -1e30)

    lane = lax.broadcasted_iota(jnp.int32, (1, LANES), 1)
    q = q_ref[0]
    zero = jnp.zeros_like(q)
    qq_scr[0:tq, :] = jnp.where(lane < HEAD_DIM, q, zero)
    qq_scr[tq:, :] = jnp.where(lane < HEAD_DIM, zero, q)
    m_scr[...] = jnp.full(m_scr.shape, -jnp.inf, F32)
    acc_scr[...] = jnp.zeros(acc_scr.shape, F32)

    n_panels = 2 * tq // pw
    all_visible = ("full",) * n_panels

    def diag_modes(key_off):
        modes = []
        for c in range(n_panels):
            q_off = (c * pw) % tq
            if key_off + tk <= q_off:
                modes.append("full")
            elif key_off >= q_off + pw:
                modes.append("skip")
            else:
                assert key_off == q_off and tk == pw
                modes.append("tri")
        return tuple(modes)

    def scores(t, dst, modes=all_visible):
        k = k_ref[0, pl.ds(pl.multiple_of(t * tk, tk), tk), :]
        for c in range(n_panels):
            if modes[c] == "skip":
                continue
            cols = slice(c * pw, (c + 1) * pw)
            dst[:, cols] = lax.dot_general(
                k, qq_scr[cols, :], (((1,), (1,)), ((), ())),
                preferred_element_type=F32)

    def softmax(src, p_dst, al_dst, modes=all_visible):
        for c in range(n_panels):
            if modes[c] == "skip":
                continue
            for w in range(pw // LANES):
                cols = slice(c * pw + w * LANES, c * pw + (w + 1) * LANES)
                s = src[:, cols]
                if modes[c] == "tri":
                    s = s + bias_scr[:, w * LANES:(w + 1) * LANES]
                m_prev = m_scr[:, cols]
                m_new = jnp.maximum(m_prev, jnp.max(s, axis=0, keepdims=True))
                al_dst[:, cols] = jnp.exp2(m_prev - m_new)
                p_dst[:, cols] = jnp.exp2(s - m_new).astype(BF16)
                m_scr[:, cols] = m_new

    def accumulate(t, p_src, al_src, modes=all_visible):
        vt = vt_scr[:, pl.ds(pl.multiple_of(t * tk, tk), tk)]
        for c in range(n_panels):
            if modes[c] == "skip":
                continue
            cols = slice(c * pw, (c + 1) * pw)
            acc_scr[:, cols] = al_src[:, cols] * acc_scr[:, cols] + jnp.dot(
                vt, p_src[:, cols], preferred_element_type=F32)

    n_diag = tq // tk
    n_full = n_diag * qi
    bufs = ((sa_scr, pa_scr, ala_scr), (sb_scr, pb_scr, alb_scr))
    scores(0, sa_scr)
    pb_scr[...] = jnp.zeros(pb_scr.shape, BF16)
    alb_scr[...] = jnp.ones(alb_scr.shape, F32)

    def body(i, carry):
        t = 2 * i
        accumulate(jnp.maximum(t - 1, 0), pb_scr, alb_scr)
        scores(t + 1, sb_scr)
        softmax(sa_scr, pa_scr, ala_scr)
        accumulate(t, pa_scr, ala_scr)
        scores(t + 2, sa_scr)
        softmax(sb_scr, pb_scr, alb_scr)
        return carry

    lax.fori_loop(0, n_full // 2, body, 0)
    prev_modes = all_visible
    for d in range(n_diag):
        s_cur, p_cur, al_cur = bufs[d % 2]
        s_nxt, p_prv, al_prv = bufs[(d + 1) % 2]
        modes = diag_modes(d * tk)
        accumulate(jnp.maximum(n_full + d - 1, 0), p_prv, al_prv, prev_modes)
        if d + 1 < n_diag:
            scores(n_full + d + 1, s_nxt, diag_modes((d + 1) * tk))
        softmax(s_cur, p_cur, al_cur, modes)
        prev_modes = modes
    _, p_last, al_last = bufs[(n_diag - 1) % 2]
    accumulate(n_full + n_diag - 1, p_last, al_last, prev_modes)

    lq1, lk1, lq2, lk2 = (lam_ref[r:r + 1, :] for r in range(4))
    lam = (jnp.exp(jnp.sum(lq1 * lk1, axis=-1, keepdims=True))
           - jnp.exp(jnp.sum(lq2 * lk2, axis=-1, keepdims=True)) + LAMBDA_INIT)
    acc = acc_scr[...]
    o = acc[0:V_DIM, :] / acc[V_DIM:V_DIM + 1, :]
    a = (o[:, 0:tq] - lam * o[:, tq:]).T
    ms = jnp.mean(a * a, axis=-1, keepdims=True)
    o_ref[0] = (((a * lax.rsqrt(ms + SUBLN_EPS)) * sw_ref[...])
                * (1.0 - LAMBDA_INIT)).astype(o_ref.dtype)


def _attention(qk, v, lam_vecs, subln_w, tq, tk, pw):
    b, s, w = v.shape
    h = w // V_DIM
    assert tq % (2 * tk) == 0, "the key-tile pipeline is written for pairs of key tiles"
    kern = functools.partial(_attn_kernel, tq=tq, tk=tk, pw=pw)
    return pl.pallas_call(
        kern,
        grid=(b, h, s // tq),
        in_specs=[pl.BlockSpec((1, tq, V_DIM), lambda bi, hi, qi: (bi, qi, hi)),
                  pl.BlockSpec((1, s, V_DIM), lambda bi, hi, qi: (bi, 0, h + hi)),
                  pl.BlockSpec((1, s, V_DIM), lambda bi, hi, qi: (bi, 0, hi)),
                  pl.BlockSpec((4, HEAD_DIM), lambda bi, hi, qi: (0, 0)),
                  pl.BlockSpec((1, V_DIM), lambda bi, hi, qi: (0, 0))],
        out_specs=pl.BlockSpec((1, tq, V_DIM), lambda bi, hi, qi: (bi, qi, hi)),
        out_shape=jax.ShapeDtypeStruct((b, s, w), BF16),
        scratch_shapes=[pltpu.VMEM((2 * tq, V_DIM), BF16),
                        pltpu.VMEM((V_DIM + BF16_SUBLANES, s), BF16),
                        pltpu.VMEM((tk, pw), F32),
                        pltpu.VMEM((1, 2 * tq), F32),
                        pltpu.VMEM((V_DIM + BF16_SUBLANES, 2 * tq), F32),
                        pltpu.VMEM((tk, 2 * tq), F32),
                        pltpu.VMEM((tk, 2 * tq), F32),
                        pltpu.VMEM((tk, 2 * tq), BF16),
                        pltpu.VMEM((tk, 2 * tq), BF16),
                        pltpu.VMEM((1, 2 * tq), F32),
                        pltpu.VMEM((1, 2 * tq), F32)],
        compiler_params=_params(("parallel", "parallel", "arbitrary")),
        name="diff_attention",
    )(qk, qk, v, lam_vecs, subln_w)


def _out_kernel(a_ref, c_ref, wa_ref, wc_ref, x_ref, o_ref):
    o_ref[...] = (x_ref[...] + _mm(a_ref[...], wa_ref[...])
                  + _mm(c_ref[...], wc_ref[...]))


def _out_proj(attn, conv, w_out, x, tm, tn):
    t, ka = attn.shape
    kc = conv.shape[1]
    n = w_out.shape[1]
    assert ka == kc
    return pl.pallas_call(
        _out_kernel,
        grid=(t // tm, n // tn),
        in_specs=[pl.BlockSpec((tm, ka), lambda i, j: (i, 0)),
                  pl.BlockSpec((tm, kc), lambda i, j: (i, 0)),
                  pl.BlockSpec((ka, tn), lambda i, j: (0, j)),
                  pl.BlockSpec((kc, tn), lambda i, j: (1, j)),
                  pl.BlockSpec((tm, tn), lambda i, j: (i, j))],
        out_specs=pl.BlockSpec((tm, tn), lambda i, j: (i, j)),
        out_shape=jax.ShapeDtypeStruct((t, n), F32),
        compiler_params=_params(("parallel", "parallel")),
        name="out_proj",
    )(attn, conv, w_out, w_out, x)


def _gate_up_kernel(x_ref, wg_ref, wu_ref, o_ref):
    x = x_ref[...]
    g = _mm(x, wg_ref[...])
    u = _mm(x, wu_ref[...])
    o_ref[...] = ((g * jax.nn.sigmoid(g)) * u).astype(o_ref.dtype)


def _gate_up(fn, w_gate, w_up, tm, tn):
    t, d = fn.shape
    f = w_gate.shape[1]
    return pl.pallas_call(
        _gate_up_kernel,
        grid=(t // tm, f // tn),
        in_specs=[pl.BlockSpec((tm, d), lambda i, j: (i, 0)),
                  pl.BlockSpec((d, tn), lambda i, j: (0, j)),
                  pl.BlockSpec((d, tn), lambda i, j: (0, j))],
        out_specs=pl.BlockSpec((tm, tn), lambda i, j: (i, j)),
        out_shape=jax.ShapeDtypeStruct((t, f), BF16),
        compiler_params=_params(("parallel", "parallel")),
        name="gate_up",
    )(fn, w_gate, w_up)


def _down_kernel(h_ref, w_ref, r_ref, o_ref):
    o_ref[...] = r_ref[...] + _mm(h_ref[...], w_ref[...])


def _down(h, w_down, resid, tm, tn):
    t, f = h.shape
    n = w_down.shape[1]
    return pl.pallas_call(
        _down_kernel,
        grid=(t // tm, n // tn),
        in_specs=[pl.BlockSpec((tm, f), lambda i, j: (i, 0)),
                  pl.BlockSpec((f, tn), lambda i, j: (0, j)),
                  pl.BlockSpec((tm, tn), lambda i, j: (i, j))],
        out_specs=pl.BlockSpec((tm, tn), lambda i, j: (i, j)),
        out_shape=jax.ShapeDtypeStruct((t, n), F32),
        compiler_params=_params(("parallel", "parallel")),
        name="down_proj",
    )(h, w_down, resid)


TILES = dict(
    norm_tm=512,
    proj_tm=1024, proj_tn=512, proj_rb=256,
    conv_tm=1024, conv_tn=512, conv_rb=256,
    attn_tq=512, attn_tk=256, attn_pw=256,
    out_tm=1024, out_tn=512,
    gu_tm=1024, gu_tn=512,
    down_tm=1024, down_tn=256,
)


def _rope_tables(seq):
    pos = jnp.arange(seq, dtype=F32)
    inv_freq = ROPE_THETA ** (-jnp.arange(0, HEAD_DIM, 2, dtype=F32) / HEAD_DIM)
    ang = pos[:, None] * inv_freq[None, :]
    cos, sin = jnp.cos(ang), jnp.sin(ang)
    reps = LANES // (HEAD_DIM // 2)
    cos_l = jnp.tile(cos, (1, reps))
    sin_l = jnp.tile(jnp.concatenate([-sin, sin], axis=1), (1, reps // 2))
    return cos_l, sin_l


def kernel(x, attn_norm_w, w_in, q_norm_w, k_norm_w, lambda_q1, lambda_k1,
           lambda_q2, lambda_k2, subln_w, conv_w, w_out, ffn_norm_w, w_gate,
           w_up, w_down):
    b, s, d = x.shape
    t = b * s
    attn_w = N_HEADS * V_DIM
    qk_cols = N_HEADS * 2 * HEAD_DIM
    conv_width = d - attn_w
    col_v = 2 * qk_cols
    col_b = col_v + attn_w
    col_c = col_b + conv_width
    col_h = col_c + conv_width
    tl = TILES
    assert w_in.shape[0] == 1, "single-layer block"

    x2 = x.reshape(t, d)
    cos_l, sin_l = _rope_tables(s)
    scale = math.log2(math.e) / math.sqrt(HEAD_DIM)
    nw = jnp.concatenate([jnp.tile(q_norm_w[0] * scale, qk_cols // HEAD_DIM),
                          jnp.tile(k_norm_w[0], qk_cols // HEAD_DIM)]).reshape(1, -1)
    lam_vecs = jnp.concatenate([lambda_q1, lambda_k1, lambda_q2, lambda_k2], axis=0)

    xn = _rmsnorm(x2, attn_norm_w[0], tl["norm_tm"])
    qk = _qk_proj(xn, w_in[0], nw, cos_l, sin_l, s, tl["proj_tm"], tl["proj_tn"],
                  tl["proj_rb"])
    v = _v_proj(xn, w_in[0], col_v, attn_w, tl["proj_tm"], tl["proj_tn"])
    conv = _conv_proj(xn, w_in[0], conv_w[0], col_b, col_c, col_h, conv_width, s,
                      tl["conv_tm"], tl["conv_tn"], tl["conv_rb"])

    attn = _attention(qk.reshape(b, s, 2 * qk_cols), v.reshape(b, s, attn_w), lam_vecs,
                      subln_w[0].reshape(1, V_DIM), tl["attn_tq"], tl["attn_tk"],
                      tl["attn_pw"])

    h1 = _out_proj(attn.reshape(t, attn_w), conv, w_out[0], x2,
                   tl["out_tm"], tl["out_tn"])
    fn = _rmsnorm(h1, ffn_norm_w[0], tl["norm_tm"])
    hid = _gate_up(fn, w_gate[0], w_up[0], tl["gu_tm"], tl["gu_tn"])
    out = _down(hid, w_down[0], h1, tl["down_tm"], tl["down_tn"])
    return out.reshape(b, s, d)
```

```python
import functools
import math

import jax
import jax.numpy as jnp
from jax import lax
from jax.experimental import pallas as pl
from jax.experimental.pallas import tpu as pltpu

F32 = jnp.float32
BF16 = jnp.bfloat16

CHUNK = 64
HEAD_DIM = 64
V_DIM = 2 * HEAD_DIM
N_HEADS = 8
CONV_K = 3
ROPE_THETA = 10000.0
EPS = 1e-6
SUBLN_EPS = 1e-5
LAMBDA_INIT = 0.8 - 0.6 * math.exp(-0.3 * 0)

LANES = 128
SUBLANES = 8
BF16_SUBLANES = 16
VMEM_LIMIT = 56 * 1024 * 1024


def _params(semantics, flags=None):
    return pltpu.CompilerParams(dimension_semantics=semantics,
                                vmem_limit_bytes=VMEM_LIMIT, flags=flags)


def _mm(a_bf16, w_f32):
    return jnp.dot(a_bf16, w_f32.astype(BF16), preferred_element_type=F32)


def _rmsnorm_kernel(x_ref, w_ref, o_ref):
    x = x_ref[...]
    ms = jnp.mean(x * x, axis=-1, keepdims=True)
    o_ref[...] = ((x * lax.rsqrt(ms + EPS)) * w_ref[...]).astype(o_ref.dtype)


def _rmsnorm(x, w, tm):
    t, d = x.shape
    return pl.pallas_call(
        _rmsnorm_kernel,
        grid=(t // tm,),
        in_specs=[pl.BlockSpec((tm, d), lambda i: (i, 0)),
                  pl.BlockSpec((1, d), lambda i: (0, 0))],
        out_specs=pl.BlockSpec((tm, d), lambda i: (i, 0)),
        out_shape=jax.ShapeDtypeStruct((t, d), BF16),
        compiler_params=_params(("parallel",)),
        name="rmsnorm",
    )(x, w.reshape(1, d))


def _row_block_pipeline(n_blocks, matmul, epilogue):
    acc = matmul(0)
    for r in range(n_blocks):
        nxt = matmul(r + 1) if r + 1 < n_blocks else None
        epilogue(r, acc)
        acc = nxt


def _qk_kernel(xn_ref, w_ref, nw_ref, cos_ref, sin_ref, o_ref, w_scr, *, rb):
    w_scr[...] = w_ref[...].astype(BF16)
    tm, tn = o_ref.shape
    lane = lax.broadcasted_iota(jnp.int32, (1, LANES), 1)
    lo_half = lane < HEAD_DIM
    first = (lane % HEAD_DIM) < (HEAD_DIM // 2)

    def matmul(r):
        return jnp.dot(xn_ref[r * rb:(r + 1) * rb, :], w_scr[...],
                       preferred_element_type=F32)

    def epilogue(r, acc):
        rows = slice(r * rb, (r + 1) * rb)
        cos = cos_ref[rows, :]
        sin = sin_ref[rows, :]
        for c in range(tn // LANES):
            cols = slice(c * LANES, (c + 1) * LANES)
            y = acc[:, cols]
            sq = y * y
            s_lo = jnp.sum(jnp.where(lo_half, sq, 0.0), axis=-1, keepdims=True)
            s_hi = jnp.sum(jnp.where(lo_half, 0.0, sq), axis=-1, keepdims=True)
            ms = jnp.where(lo_half, s_lo, s_hi) * (1.0 / HEAD_DIM)
            yn = (y * lax.rsqrt(ms + EPS)) * nw_ref[:, cols]
            rot = jnp.where(first, pltpu.roll(yn, LANES - HEAD_DIM // 2, 1),
                            pltpu.roll(yn, HEAD_DIM // 2, 1))
            o_ref[rows, cols] = (yn * cos + rot * sin).astype(o_ref.dtype)

    _row_block_pipeline(tm // rb, matmul, epilogue)


def _qk_proj(xn, w_in, nw, cos, sin, seq, tm, tn, rb):
    t, d = xn.shape
    n = nw.shape[1]
    tiles_per_seq = seq // tm
    return pl.pallas_call(
        functools.partial(_qk_kernel, rb=rb),
        grid=(t // tm, n // tn),
        scratch_shapes=[pltpu.VMEM((d, tn), BF16)],
        in_specs=[pl.BlockSpec((tm, d), lambda i, j: (i, 0)),
                  pl.BlockSpec((d, tn), lambda i, j: (0, j)),
                  pl.BlockSpec((1, tn), lambda i, j: (0, j)),
                  pl.BlockSpec((tm, LANES), lambda i, j: (i % tiles_per_seq, 0)),
                  pl.BlockSpec((tm, LANES), lambda i, j: (i % tiles_per_seq, 0))],
        out_specs=pl.BlockSpec((tm, tn), lambda i, j: (i, j)),
        out_shape=jax.ShapeDtypeStruct((t, n), BF16),
        compiler_params=_params(("parallel", "parallel")),
        name="qk_proj",
    )(xn, w_in, nw, cos, sin)


def _v_kernel(xn_ref, w_ref, o_ref):
    o_ref[...] = _mm(xn_ref[...], w_ref[...]).astype(o_ref.dtype)


def _v_proj(xn, w_in, col0, n, tm, tn):
    t, d = xn.shape
    jb = col0 // tn
    return pl.pallas_call(
        _v_kernel,
        grid=(t // tm, n // tn),
        in_specs=[pl.BlockSpec((tm, d), lambda i, j: (i, 0)),
                  pl.BlockSpec((d, tn), lambda i, j: (0, jb + j))],
        out_specs=pl.BlockSpec((tm, tn), lambda i, j: (i, j)),
        out_shape=jax.ShapeDtypeStruct((t, n), BF16),
        compiler_params=_params(("parallel", "parallel")),
        name="v_proj",
    )(xn, w_in)


def _conv_kernel(xn_ref, wb_ref, wc_ref, wh_ref, cw_ref, o_ref,
                 wb_scr, wc_scr, wh_scr, u_scr, carry_scr, *, tiles_per_seq, rb):
    i = pl.program_id(0)
    j = pl.program_id(1)
    tm = xn_ref.shape[0]
    wc_scr[...] = wc_ref[...].astype(BF16)
    wh_scr[...] = wh_ref[...].astype(BF16)
    wb_scr[...] = wb_ref[...].astype(BF16)

    @pl.when(i % tiles_per_seq == 0)
    def _():
        u_scr[0:SUBLANES, :] = jnp.zeros((SUBLANES, u_scr.shape[1]), F32)

    @pl.when(i % tiles_per_seq != 0)
    def _():
        u_scr[0:SUBLANES, :] = carry_scr[j]

    def xn_rows(r):
        return xn_ref[r * rb:(r + 1) * rb, :]

    def gate_matmuls(r):
        return (jnp.dot(xn_rows(r), wc_scr[...], preferred_element_type=F32),
                jnp.dot(xn_rows(r), wh_scr[...], preferred_element_type=F32))

    def store_u(r, acc):
        u_scr[SUBLANES + r * rb:SUBLANES + (r + 1) * rb, :] = acc[0] * acc[1]

    _row_block_pipeline(tm // rb, gate_matmuls, store_u)
    carry_scr[j] = u_scr[tm:tm + SUBLANES, :]
    cw = cw_ref[...]

    def b_matmul(r):
        return jnp.dot(xn_rows(r), wb_scr[...], preferred_element_type=F32)

    def conv_out(r, gate_b):
        def shifted(back):
            start = SUBLANES + r * rb - back
            return u_scr[start:start + rb, :]
        y = cw[2:3, :] * shifted(0) + cw[1:2, :] * shifted(1) + cw[0:1, :] * shifted(2)
        o_ref[r * rb:(r + 1) * rb, :] = (gate_b * y).astype(o_ref.dtype)

    _row_block_pipeline(tm // rb, b_matmul, conv_out)


def _conv_proj(xn, w_in, conv_w, col_b, col_c, col_h, n, seq, tm, tn, rb):
    t, d = xn.shape
    jb, jc, jh = col_b // tn, col_c // tn, col_h // tn
    nj = n // tn
    kern = functools.partial(_conv_kernel, tiles_per_seq=seq // tm, rb=rb)
    w_scratch = [pltpu.VMEM((d, tn), BF16)] * 3
    return pl.pallas_call(
        kern,
        grid=(t // tm, nj),
        in_specs=[pl.BlockSpec((tm, d), lambda i, j: (i, 0)),
                  pl.BlockSpec((d, tn), lambda i, j: (0, jb + j)),
                  pl.BlockSpec((d, tn), lambda i, j: (0, jc + j)),
                  pl.BlockSpec((d, tn), lambda i, j: (0, jh + j)),
                  pl.BlockSpec((CONV_K, tn), lambda i, j: (0, j))],
        out_specs=pl.BlockSpec((tm, tn), lambda i, j: (i, j)),
        out_shape=jax.ShapeDtypeStruct((t, n), BF16),
        scratch_shapes=w_scratch + [pltpu.VMEM((tm + SUBLANES, tn), F32),
                                    pltpu.VMEM((nj, SUBLANES, tn), F32)],
        compiler_params=_params(("arbitrary", "arbitrary")),
        name="conv_proj",
    )(xn, w_in, w_in, w_in, conv_w)


def _attn_kernel(q_ref, k_ref, v_ref, lam_ref, sw_ref, o_ref,
                 qq_scr, vt_scr, bias_scr, m_scr, acc_scr, sa_scr, sb_scr,
                 *, tq, tk, pw):
    qi = pl.program_id(2)
    seq = k_ref.shape[1]

    @pl.when(qi == 0)
    def _():
        for c in range(seq // LANES):
            blk = v_ref[0, c * LANES:(c + 1) * LANES, :].astype(F32)
            vt_scr[0:V_DIM, c * LANES:(c + 1) * LANES] = blk.T.astype(BF16)
        vt_scr[V_DIM:, :] = jnp.ones((vt_scr.shape[0] - V_DIM, seq), BF16)
        key = lax.broadcasted_iota(jnp.int32, bias_scr.shape, 0)
        qry = lax.broadcasted_iota(jnp.int32, bias_scr.shape, 1)
        bias_scr[...] = jnp.where(key // CHUNK <= qry // CHUNK, 0.0, -1e30)

    lane = lax.broadcasted_iota(jnp.int32, (1, LANES), 1)
    q = q_ref[0]
    zero = jnp.zeros_like(q)
    qq_scr[0:tq, :] = jnp.where(lane < HEAD_DIM, q, zero)
    qq_scr[tq:, :] = jnp.where(lane < HEAD_DIM, zero, q)
    m_scr[...] = jnp.full(m_scr.shape, -jnp.inf, F32)
    acc_scr[...] = jnp.zeros(acc_scr.shape, F32)

    n_panels = 2 * tq // pw
    all_visible = ("full",) * n_panels

    def diag_modes(key_off):
        modes = []
        for c in range(n_panels):
            q_off = (c * pw) % tq
            if key_off + tk <= q_off:
                modes.append("full")
            elif key_off >= q_off + pw:
                modes.append("skip")
            else:
                assert key_off == q_off and tk == pw
                modes.append("tri")
        return tuple(modes)

    def scores(t, dst, modes=all_visible):
        k = k_ref[0, pl.ds(pl.multiple_of(t * tk, tk), tk), :]
        for c in range(n_panels):
            if modes[c] == "skip":
                continue
            cols = slice(c * pw, (c + 1) * pw)
            dst[:, cols] = lax.dot_general(
                k, qq_scr[cols, :], (((1,), (1,)), ((), ())),
                preferred_element_type=F32)

    def update(t, src, modes=all_visible):
        vt = vt_scr[:, pl.ds(pl.multiple_of(t * tk, tk), tk)]
        for c in range(n_panels):
            if modes[c] == "skip":
                continue
            cols = slice(c * pw, (c + 1) * pw)
            s = src[:, cols]
            if modes[c] == "tri":
                s = s + bias_scr[...]
            m_prev = m_scr[:, cols]
            m_new = jnp.maximum(m_prev, jnp.max(s, axis=0, keepdims=True))
            alpha = jnp.exp2(m_prev - m_new)
            p = jnp.exp2(s - m_new).astype(BF16)
            m_scr[:, cols] = m_new
            acc_scr[:, cols] = alpha * acc_scr[:, cols] + jnp.dot(
                vt, p, preferred_element_type=F32)

    n_diag = tq // tk
    n_full = n_diag * qi
    bufs = (sa_scr, sb_scr)
    scores(0, sa_scr)

    def body(i, carry):
        t = 2 * i
        scores(t + 1, sb_scr)
        update(t, sa_scr)
        scores(t + 2, sa_scr)
        update(t + 1, sb_scr)
        return carry

    lax.fori_loop(0, n_full // 2, body, 0)
    for d in range(n_diag):
        if d + 1 < n_diag:
            scores(n_full + d + 1, bufs[(d + 1) % 2], diag_modes((d + 1) * tk))
        update(n_full + d, bufs[d % 2], diag_modes(d * tk))

    lq1, lk1, lq2, lk2 = (lam_ref[r:r + 1, :] for r in range(4))
    lam = (jnp.exp(jnp.sum(lq1 * lk1, axis=-1, keepdims=True))
           - jnp.exp(jnp.sum(lq2 * lk2, axis=-1, keepdims=True)) + LAMBDA_INIT)
    acc = acc_scr[...]
    o = acc[0:V_DIM, :] / acc[V_DIM:V_DIM + 1, :]
    a = (o[:, 0:tq] - lam * o[:, tq:]).T
    ms = jnp.mean(a * a, axis=-1, keepdims=True)
    o_ref[0] = (((a * lax.rsqrt(ms + SUBLN_EPS)) * sw_ref[...])
                * (1.0 - LAMBDA_INIT)).astype(o_ref.dtype)


def _attention(qk, v, lam_vecs, subln_w, tq, tk, pw):
    b, s, w = v.shape
    h = w // V_DIM
    assert tq % (2 * tk) == 0, "the key-tile pipeline is written for pairs of key tiles"
    kern = functools.partial(_attn_kernel, tq=tq, tk=tk, pw=pw)
    return pl.pallas_call(
        kern,
        grid=(b, h, s // tq),
        in_specs=[pl.BlockSpec((1, tq, V_DIM), lambda bi, hi, qi: (bi, qi, hi)),
                  pl.BlockSpec((1, s, V_DIM), lambda bi, hi, qi: (bi, 0, h + hi)),
                  pl.BlockSpec((1, s, V_DIM), lambda bi, hi, qi: (bi, 0, hi)),
                  pl.BlockSpec((4, HEAD_DIM), lambda bi, hi, qi: (0, 0)),
                  pl.BlockSpec((1, V_DIM), lambda bi, hi, qi: (0, 0))],
        out_specs=pl.BlockSpec((1, tq, V_DIM), lambda bi, hi, qi: (bi, qi, hi)),
        out_shape=jax.ShapeDtypeStruct((b, s, w), BF16),
        scratch_shapes=[pltpu.VMEM((2 * tq, V_DIM), BF16),
                        pltpu.VMEM((V_DIM + BF16_SUBLANES, s), BF16),
                        pltpu.VMEM((tk, pw), F32),
                        pltpu.VMEM((1, 2 * tq), F32),
                        pltpu.VMEM((V_DIM + BF16_SUBLANES, 2 * tq), F32),
                        pltpu.VMEM((tk, 2 * tq), F32),
                        pltpu.VMEM((tk, 2 * tq), F32)],
        compiler_params=_params(("parallel", "parallel", "arbitrary")),
        name="diff_attention",
    )(qk, qk, v, lam_vecs, subln_w)


def _out_kernel(a_ref, c_ref, wa_ref, wc_ref, x_ref, o_ref):
    o_ref[...] = (x_ref[...] + _mm(a_ref[...], wa_ref[...])
                  + _mm(c_ref[...], wc_ref[...]))


def _out_proj(attn, conv, w_out, x, tm, tn):
    t, ka = attn.shape
    kc = conv.shape[1]
    n = w_out.shape[1]
    assert ka == kc
    return pl.pallas_call(
        _out_kernel,
        grid=(t // tm, n // tn),
        in_specs=[pl.BlockSpec((tm, ka), lambda i, j: (i, 0)),
                  pl.BlockSpec((tm, kc), lambda i, j: (i, 0)),
                  pl.BlockSpec((ka, tn), lambda i, j: (0, j)),
                  pl.BlockSpec((kc, tn), lambda i, j: (1, j)),
                  pl.BlockSpec((tm, tn), lambda i, j: (i, j))],
        out_specs=pl.BlockSpec((tm, tn), lambda i, j: (i, j)),
        out_shape=jax.ShapeDtypeStruct((t, n), F32),
        compiler_params=_params(("parallel", "parallel")),
        name="out_proj",
    )(attn, conv, w_out, w_out, x)


def _gate_up_kernel(x_ref, wg_ref, wu_ref, o_ref):
    x = x_ref[...]
    g = _mm(x, wg_ref[...])
    u = _mm(x, wu_ref[...])
    o_ref[...] = ((g * jax.nn.sigmoid(g)) * u).astype(o_ref.dtype)


def _gate_up(fn, w_gate, w_up, tm, tn):
    t, d = fn.shape
    f = w_gate.shape[1]
    return pl.pallas_call(
        _gate_up_kernel,
        grid=(t // tm, f // tn),
        in_specs=[pl.BlockSpec((tm, d), lambda i, j: (i, 0)),
                  pl.BlockSpec((d, tn), lambda i, j: (0, j)),
                  pl.BlockSpec((d, tn), lambda i, j: (0, j))],
        out_specs=pl.BlockSpec((tm, tn), lambda i, j: (i, j)),
        out_shape=jax.ShapeDtypeStruct((t, f), BF16),
        compiler_params=_params(("parallel", "parallel")),
        name="gate_up",
    )(fn, w_gate, w_up)


def _down_kernel(h_ref, w_ref, r_ref, o_ref):
    o_ref[...] = r_ref[...] + _mm(h_ref[...], w_ref[...])


def _down(h, w_down, resid, tm, tn):
    t, f = h.shape
    n = w_down.shape[1]
    return pl.pallas_call(
        _down_kernel,
        grid=(t // tm, n // tn),
        in_specs=[pl.BlockSpec((tm, f), lambda i, j: (i, 0)),
                  pl.BlockSpec((f, tn), lambda i, j: (0, j)),
                  pl.BlockSpec((tm, tn), lambda i, j: (i, j))],
        out_specs=pl.BlockSpec((tm, tn), lambda i, j: (i, j)),
        out_shape=jax.ShapeDtypeStruct((t, n), F32),
        compiler_params=_params(("parallel", "parallel")),
        name="down_proj",
    )(h, w_down, resid)


TILES = dict(
    norm_tm=512,
    proj_tm=1024, proj_tn=512, proj_rb=256,
    conv_tm=1024, conv_tn=512, conv_rb=256,
    attn_tq=1024, attn_tk=256, attn_pw=256,
    out_tm=1024, out_tn=512,
    gu_tm=1024, gu_tn=512,
    down_tm=1024, down_tn=256,
)


def _rope_tables(seq):
    pos = jnp.arange(seq, dtype=F32)
    inv_freq = ROPE_THETA ** (-jnp.arange(0, HEAD_DIM, 2, dtype=F32) / HEAD_DIM)
    ang = pos[:, None] * inv_freq[None, :]
    cos, sin = jnp.cos(ang), jnp.sin(ang)
    reps = LANES // (HEAD_DIM // 2)
    cos_l = jnp.tile(cos, (1, reps))
    sin_l = jnp.tile(jnp.concatenate([-sin, sin], axis=1), (1, reps // 2))
    return cos_l, sin_l


def kernel(x, attn_norm_w, w_in, q_norm_w, k_norm_w, lambda_q1, lambda_k1,
           lambda_q2, lambda_k2, subln_w, conv_w, w_out, ffn_norm_w, w_gate,
           w_up, w_down):
    b, s, d = x.shape
    t = b * s
    attn_w = N_HEADS * V_DIM
    qk_cols = N_HEADS * 2 * HEAD_DIM
    conv_width = d - attn_w
    col_v = 2 * qk_cols
    col_b = col_v + attn_w
    col_c = col_b + conv_width
    col_h = col_c + conv_width
    tl = TILES
    assert w_in.shape[0] == 1, "single-layer block"

    x2 = x.reshape(t, d)
    cos_l, sin_l = _rope_tables(s)
    scale = math.log2(math.e) / math.sqrt(HEAD_DIM)
    nw = jnp.concatenate([jnp.tile(q_norm_w[0] * scale, qk_cols // HEAD_DIM),
                          jnp.tile(k_norm_w[0], qk_cols // HEAD_DIM)]).reshape(1, -1)
    lam_vecs = jnp.concatenate([lambda_q1, lambda_k1, lambda_q2, lambda_k2], axis=0)

    xn = _rmsnorm(x2, attn_norm_w[0], tl["norm_tm"])
    qk = _qk_proj(xn, w_in[0], nw, cos_l, sin_l, s, tl["proj_tm"], tl["proj_tn"],
                  tl["proj_rb"])
    v = _v_proj(xn, w_in[0], col_v, attn_w, tl["proj_tm"], tl["proj_tn"])
    conv = _conv_proj(xn, w_in[0], conv_w[0], col_b, col_c, col_h, conv_width, s,
                      tl["conv_tm"], tl["conv_tn"], tl["conv_rb"])

    attn = _attention(qk.reshape(b, s, 2 * qk_cols), v.reshape(b, s, attn_w), lam_vecs,
                      subln_w[0].reshape(1, V_DIM), tl["attn_tq"], tl["attn_tk"],
                      tl["attn_pw"])

    h1 = _out_proj(attn.reshape(t, attn_w), conv, w_out[0], x2,
                   tl["out_tm"], tl["out_tn"])
    fn = _rmsnorm(h1, ffn_norm_w[0], tl["norm_tm"])
    hid = _gate_up(fn, w_gate[0], w_up[0], tl["gu_tm"], tl["gu_tn"])
    out = _down(hid, w_down[0], h1, tl["down_tm"], tl["down_tn"])
    return out.reshape(b, s, d)
```

```python
import functools
import math

import jax
import jax.numpy as jnp
from jax import lax
from jax.experimental import pallas as pl
from jax.experimental.pallas import tpu as pltpu

F32 = jnp.float32
BF16 = jnp.bfloat16

CHUNK = 64
HEAD_DIM = 64
V_DIM = 2 * HEAD_DIM
N_HEADS = 8
CONV_K = 3
ROPE_THETA = 10000.0
EPS = 1e-6
SUBLN_EPS = 1e-5
LAMBDA_INIT = 0.8 - 0.6 * math.exp(-0.3 * 0)

LANES = 128
SUBLANES = 8
BF16_SUBLANES = 16
VMEM_LIMIT = 56 * 1024 * 1024


def _params(semantics, flags=None):
    return pltpu.CompilerParams(dimension_semantics=semantics,
                                vmem_limit_bytes=VMEM_LIMIT, flags=flags)


def _mm(a_bf16, w_f32):
    return jnp.dot(a_bf16, w_f32.astype(BF16), preferred_element_type=F32)


def _rms_scale(x, w, eps):
    ms = jnp.mean(x * x, axis=-1, keepdims=True)
    return (x * lax.rsqrt(ms + eps)) * w


def _row_block_pipeline(n_blocks, matmul, epilogue):
    acc = matmul(0)
    for r in range(n_blocks):
        nxt = matmul(r + 1) if r + 1 < n_blocks else None
        epilogue(r, acc)
        acc = nxt


def _qkv_kernel(x_ref, anw_ref, w_ref, nw_ref, cos_ref, sin_ref, o_ref, xn_ref,
                w_scr, *, rb, n_qk_tiles):
    j = pl.program_id(1)
    tm, tn = o_ref.shape

    @pl.when(j == 0)
    def _():
        for r in range(tm // rb):
            rows = slice(r * rb, (r + 1) * rb)
            xn_ref[rows, :] = _rms_scale(x_ref[rows, :], anw_ref[...], EPS).astype(BF16)

    w_scr[...] = w_ref[...].astype(BF16)
    lane = lax.broadcasted_iota(jnp.int32, (1, LANES), 1)
    lo_half = lane < HEAD_DIM
    first = (lane % HEAD_DIM) < (HEAD_DIM // 2)

    def matmul(r):
        return jnp.dot(xn_ref[r * rb:(r + 1) * rb, :], w_scr[...],
                       preferred_element_type=F32)

    def plain(r, acc):
        o_ref[r * rb:(r + 1) * rb, :] = acc.astype(o_ref.dtype)

    def norm_rope(r, acc):
        rows = slice(r * rb, (r + 1) * rb)
        cos = cos_ref[rows, :]
        sin = sin_ref[rows, :]
        for c in range(tn // LANES):
            cols = slice(c * LANES, (c + 1) * LANES)
            y = acc[:, cols]
            sq = y * y
            s_lo = jnp.sum(jnp.where(lo_half, sq, 0.0), axis=-1, keepdims=True)
            s_hi = jnp.sum(jnp.where(lo_half, 0.0, sq), axis=-1, keepdims=True)
            ms = jnp.where(lo_half, s_lo, s_hi) * (1.0 / HEAD_DIM)
            yn = (y * lax.rsqrt(ms + EPS)) * nw_ref[:, cols]
            rot = jnp.where(first, pltpu.roll(yn, LANES - HEAD_DIM // 2, 1),
                            pltpu.roll(yn, HEAD_DIM // 2, 1))
            o_ref[rows, cols] = (yn * cos + rot * sin).astype(o_ref.dtype)

    @pl.when(j < n_qk_tiles)
    def _():
        _row_block_pipeline(tm // rb, matmul, norm_rope)

    @pl.when(j >= n_qk_tiles)
    def _():
        _row_block_pipeline(tm // rb, matmul, plain)


def _qkv_proj(x, attn_norm_w, w_in, nw, cos, sin, seq, n_out, tm, tn, rb):
    t, d = x.shape
    n_qk_tiles = nw.shape[1] // tn
    tiles_per_seq = seq // tm
    kern = functools.partial(_qkv_kernel, rb=rb, n_qk_tiles=n_qk_tiles)
    return pl.pallas_call(
        kern,
        grid=(t // tm, n_out // tn),
        in_specs=[pl.BlockSpec((tm, d), lambda i, j: (i, 0)),
                  pl.BlockSpec((1, d), lambda i, j: (0, 0)),
                  pl.BlockSpec((d, tn), lambda i, j: (0, j)),
                  pl.BlockSpec((1, tn), lambda i, j: (0, jnp.minimum(j, n_qk_tiles - 1))),
                  pl.BlockSpec((tm, LANES), lambda i, j: (i % tiles_per_seq, 0)),
                  pl.BlockSpec((tm, LANES), lambda i, j: (i % tiles_per_seq, 0))],
        out_specs=[pl.BlockSpec((tm, tn), lambda i, j: (i, j)),
                   pl.BlockSpec((tm, d), lambda i, j: (i, 0))],
        out_shape=[jax.ShapeDtypeStruct((t, n_out), BF16),
                   jax.ShapeDtypeStruct((t, d), BF16)],
        scratch_shapes=[pltpu.VMEM((d, tn), BF16)],
        compiler_params=_params(("arbitrary", "arbitrary")),
        name="qkv_proj",
    )(x, attn_norm_w.reshape(1, d), w_in, nw, cos, sin)


def _conv_kernel(xn_ref, wb_ref, wc_ref, wh_ref, cw_ref, o_ref,
                 wb_scr, wc_scr, wh_scr, u_scr, carry_scr, *, tiles_per_seq, rb):
    i = pl.program_id(0)
    j = pl.program_id(1)
    tm = xn_ref.shape[0]
    wc_scr[...] = wc_ref[...].astype(BF16)
    wh_scr[...] = wh_ref[...].astype(BF16)
    wb_scr[...] = wb_ref[...].astype(BF16)

    @pl.when(i % tiles_per_seq == 0)
    def _():
        u_scr[0:SUBLANES, :] = jnp.zeros((SUBLANES, u_scr.shape[1]), F32)

    @pl.when(i % tiles_per_seq != 0)
    def _():
        u_scr[0:SUBLANES, :] = carry_scr[j]

    def xn_rows(r):
        return xn_ref[r * rb:(r + 1) * rb, :]

    def gate_matmuls(r):
        return (jnp.dot(xn_rows(r), wc_scr[...], preferred_element_type=F32),
                jnp.dot(xn_rows(r), wh_scr[...], preferred_element_type=F32))

    def store_u(r, acc):
        u_scr[SUBLANES + r * rb:SUBLANES + (r + 1) * rb, :] = acc[0] * acc[1]

    _row_block_pipeline(tm // rb, gate_matmuls, store_u)
    carry_scr[j] = u_scr[tm:tm + SUBLANES, :]
    cw = cw_ref[...]

    def b_matmul(r):
        return jnp.dot(xn_rows(r), wb_scr[...], preferred_element_type=F32)

    def conv_out(r, gate_b):
        def shifted(back):
            start = SUBLANES + r * rb - back
            return u_scr[start:start + rb, :]
        y = cw[2:3, :] * shifted(0) + cw[1:2, :] * shifted(1) + cw[0:1, :] * shifted(2)
        o_ref[r * rb:(r + 1) * rb, :] = (gate_b * y).astype(o_ref.dtype)

    _row_block_pipeline(tm // rb, b_matmul, conv_out)


def _conv_proj(xn, w_in, conv_w, col_b, col_c, col_h, n, seq, tm, tn, rb):
    t, d = xn.shape
    jb, jc, jh = col_b // tn, col_c // tn, col_h // tn
    nj = n // tn
    kern = functools.partial(_conv_kernel, tiles_per_seq=seq // tm, rb=rb)
    w_scratch = [pltpu.VMEM((d, tn), BF16)] * 3
    return pl.pallas_call(
        kern,
        grid=(t // tm, nj),
        in_specs=[pl.BlockSpec((tm, d), lambda i, j: (i, 0)),
                  pl.BlockSpec((d, tn), lambda i, j: (0, jb + j)),
                  pl.BlockSpec((d, tn), lambda i, j: (0, jc + j)),
                  pl.BlockSpec((d, tn), lambda i, j: (0, jh + j)),
                  pl.BlockSpec((CONV_K, tn), lambda i, j: (0, j))],
        out_specs=pl.BlockSpec((tm, tn), lambda i, j: (i, j)),
        out_shape=jax.ShapeDtypeStruct((t, n), BF16),
        scratch_shapes=w_scratch + [pltpu.VMEM((tm + SUBLANES, tn), F32),
                                    pltpu.VMEM((nj, SUBLANES, tn), F32)],
        compiler_params=_params(("arbitrary", "arbitrary")),
        name="conv_proj",
    )(xn, w_in, w_in, w_in, conv_w)


def _attn_kernel(q_ref, k_ref, v_ref, lam_ref, sw_ref, o_ref,
                 qq_scr, vt_scr, bias_scr, m_scr, acc_scr, sa_scr, sb_scr,
                 *, tq, tk, pw):
    qi = pl.program_id(2)
    seq = k_ref.shape[1]

    @pl.when(qi == 0)
    def _():
        for c in range(seq // LANES):
            blk = v_ref[0, c * LANES:(c + 1) * LANES, :].astype(F32)
            vt_scr[0:V_DIM, c * LANES:(c + 1) * LANES] = blk.T.astype(BF16)
        vt_scr[V_DIM:, :] = jnp.ones((vt_scr.shape[0] - V_DIM, seq), BF16)
        key = lax.broadcasted_iota(jnp.int32, bias_scr.shape, 0)
        qry = lax.broadcasted_iota(jnp.int32, bias_scr.shape, 1)
        bias_scr[...] = jnp.where(key // CHUNK <= qry // CHUNK, 0.0, -1e30)

    lane = lax.broadcasted_iota(jnp.int32, (1, LANES), 1)
    q = q_ref[0]
    zero = jnp.zeros_like(q)
    qq_scr[0:tq, :] = jnp.where(lane < HEAD_DIM, q, zero)
    qq_scr[tq:, :] = jnp.where(lane < HEAD_DIM, zero, q)
    m_scr[...] = jnp.full(m_scr.shape, -jnp.inf, F32)
    acc_scr[...] = jnp.zeros(acc_scr.shape, F32)

    n_panels = 2 * tq // pw
    all_visible = ("full",) * n_panels

    def diag_modes(key_off):
        modes = []
        for c in range(n_panels):
            q_off = (c * pw) % tq
            if key_off + tk <= q_off:
                modes.append("full")
            elif key_off >= q_off + pw:
                modes.append("skip")
            else:
                assert key_off == q_off and tk == pw
                modes.append("tri")
        return tuple(modes)

    def scores(t, dst, modes=all_visible):
        k = k_ref[0, pl.ds(pl.multiple_of(t * tk, tk), tk), :]
        for c in range(n_panels):
            if modes[c] == "skip":
                continue
            cols = slice(c * pw, (c + 1) * pw)
            dst[:, cols] = lax.dot_general(
                k, qq_scr[cols, :], (((1,), (1,)), ((), ())),
                preferred_element_type=F32)

    def update(t, src, modes=all_visible):
        vt = vt_scr[:, pl.ds(pl.multiple_of(t * tk, tk), tk)]
        for c in range(n_panels):
            if modes[c] == "skip":
                continue
            cols = slice(c * pw, (c + 1) * pw)
            s = src[:, cols]
            if modes[c] == "tri":
                s = s + bias_scr[...]
            m_prev = m_scr[:, cols]
            m_new = jnp.maximum(m_prev, jnp.max(s, axis=0, keepdims=True))
            alpha = jnp.exp2(m_prev - m_new)
            p = jnp.exp2(s - m_new).astype(BF16)
            m_scr[:, cols] = m_new
            acc_scr[:, cols] = alpha * acc_scr[:, cols] + jnp.dot(
                vt, p, preferred_element_type=F32)

    n_diag = tq // tk
    n_full = n_diag * qi
    bufs = (sa_scr, sb_scr)
    scores(0, sa_scr)

    def body(i, carry):
        t = 2 * i
        scores(t + 1, sb_scr)
        update(t, sa_scr)
        scores(t + 2, sa_scr)
        update(t + 1, sb_scr)
        return carry

    lax.fori_loop(0, n_full // 2, body, 0)
    for d in range(n_diag):
        if d + 1 < n_diag:
            scores(n_full + d + 1, bufs[(d + 1) % 2], diag_modes((d + 1) * tk))
        update(n_full + d, bufs[d % 2], diag_modes(d * tk))

    lq1, lk1, lq2, lk2 = (lam_ref[r:r + 1, :] for r in range(4))
    lam = (jnp.exp(jnp.sum(lq1 * lk1, axis=-1, keepdims=True))
           - jnp.exp(jnp.sum(lq2 * lk2, axis=-1, keepdims=True)) + LAMBDA_INIT)
    acc = acc_scr[...]
    o = acc[0:V_DIM, :] / acc[V_DIM:V_DIM + 1, :]
    a = (o[:, 0:tq] - lam * o[:, tq:]).T
    ms = jnp.mean(a * a, axis=-1, keepdims=True)
    o_ref[0] = (((a * lax.rsqrt(ms + SUBLN_EPS)) * sw_ref[...])
                * (1.0 - LAMBDA_INIT)).astype(o_ref.dtype)


def _attention(qkv, lam_vecs, subln_w, tq, tk, pw):
    b, s, w3 = qkv.shape
    w = w3 // 3
    h = w // V_DIM
    assert tq % (2 * tk) == 0, "the key-tile pipeline is written for pairs of key tiles"
    kern = functools.partial(_attn_kernel, tq=tq, tk=tk, pw=pw)
    return pl.pallas_call(
        kern,
        grid=(b, h, s // tq),
        in_specs=[pl.BlockSpec((1, tq, V_DIM), lambda bi, hi, qi: (bi, qi, hi)),
                  pl.BlockSpec((1, s, V_DIM), lambda bi, hi, qi: (bi, 0, h + hi)),
                  pl.BlockSpec((1, s, V_DIM), lambda bi, hi, qi: (bi, 0, 2 * h + hi)),
                  pl.BlockSpec((4, HEAD_DIM), lambda bi, hi, qi: (0, 0)),
                  pl.BlockSpec((1, V_DIM), lambda bi, hi, qi: (0, 0))],
        out_specs=pl.BlockSpec((1, tq, V_DIM), lambda bi, hi, qi: (bi, qi, hi)),
        out_shape=jax.ShapeDtypeStruct((b, s, w), BF16),
        scratch_shapes=[pltpu.VMEM((2 * tq, V_DIM), BF16),
                        pltpu.VMEM((V_DIM + BF16_SUBLANES, s), BF16),
                        pltpu.VMEM((tk, pw), F32),
                        pltpu.VMEM((1, 2 * tq), F32),
                        pltpu.VMEM((V_DIM + BF16_SUBLANES, 2 * tq), F32),
                        pltpu.VMEM((tk, 2 * tq), F32),
                        pltpu.VMEM((tk, 2 * tq), F32)],
        compiler_params=_params(("parallel", "parallel", "arbitrary")),
        name="diff_attention",
    )(qkv, qkv, qkv, lam_vecs, subln_w)


def _out_kernel(a_ref, c_ref, wa_ref, wc_ref, x_ref, fnw_ref, o_ref, fn_ref, h_scr,
                *, rb):
    j = pl.program_id(1)
    n_j, tm, tn = h_scr.shape
    h = x_ref[...] + _mm(a_ref[...], wa_ref[...]) + _mm(c_ref[...], wc_ref[...])
    o_ref[...] = h
    h_scr[j] = h

    @pl.when(j == n_j - 1)
    def _():
        for r in range(tm // rb):
            rows = slice(r * rb, (r + 1) * rb)
            ssq = sum(jnp.sum(h_scr[jj, rows, :] * h_scr[jj, rows, :], axis=-1, keepdims=True)
                      for jj in range(n_j))
            scale = lax.rsqrt(ssq * (1.0 / (n_j * tn)) + EPS)
            for jj in range(n_j):
                cols = slice(jj * tn, (jj + 1) * tn)
                fn_ref[rows, cols] = ((h_scr[jj, rows, :] * scale)
                                      * fnw_ref[:, cols]).astype(fn_ref.dtype)


def _out_proj(attn, conv, w_out, x, ffn_norm_w, tm, tn, rb):
    t, ka = attn.shape
    kc = conv.shape[1]
    n = w_out.shape[1]
    assert ka == kc
    return pl.pallas_call(
        functools.partial(_out_kernel, rb=rb),
        grid=(t // tm, n // tn),
        in_specs=[pl.BlockSpec((tm, ka), lambda i, j: (i, 0)),
                  pl.BlockSpec((tm, kc), lambda i, j: (i, 0)),
                  pl.BlockSpec((ka, tn), lambda i, j: (0, j)),
                  pl.BlockSpec((kc, tn), lambda i, j: (1, j)),
                  pl.BlockSpec((tm, tn), lambda i, j: (i, j)),
                  pl.BlockSpec((1, n), lambda i, j: (0, 0))],
        out_specs=[pl.BlockSpec((tm, tn), lambda i, j: (i, j)),
                   pl.BlockSpec((tm, n), lambda i, j: (i, 0))],
        out_shape=[jax.ShapeDtypeStruct((t, n), F32),
                   jax.ShapeDtypeStruct((t, n), BF16)],
        scratch_shapes=[pltpu.VMEM((n // tn, tm, tn), F32)],
        compiler_params=_params(("arbitrary", "arbitrary")),
        name="out_proj",
    )(attn, conv, w_out, w_out, x, ffn_norm_w.reshape(1, n))


def _gate_up_kernel(x_ref, wg_ref, wu_ref, o_ref):
    x = x_ref[...]
    g = _mm(x, wg_ref[...])
    u = _mm(x, wu_ref[...])
    o_ref[...] = ((g * jax.nn.sigmoid(g)) * u).astype(o_ref.dtype)


def _gate_up(fn, w_gate, w_up, tm, tn):
    t, d = fn.shape
    f = w_gate.shape[1]
    return pl.pallas_call(
        _gate_up_kernel,
        grid=(t // tm, f // tn),
        in_specs=[pl.BlockSpec((tm, d), lambda i, j: (i, 0)),
                  pl.BlockSpec((d, tn), lambda i, j: (0, j)),
                  pl.BlockSpec((d, tn), lambda i, j: (0, j))],
        out_specs=pl.BlockSpec((tm, tn), lambda i, j: (i, j)),
        out_shape=jax.ShapeDtypeStruct((t, f), BF16),
        compiler_params=_params(("parallel", "parallel")),
        name="gate_up",
    )(fn, w_gate, w_up)


def _down_kernel(h_ref, w_ref, r_ref, o_ref):
    o_ref[...] = r_ref[...] + _mm(h_ref[...], w_ref[...])


def _down(h, w_down, resid, tm, tn):
    t, f = h.shape
    n = w_down.shape[1]
    return pl.pallas_call(
        _down_kernel,
        grid=(t // tm, n // tn),
        in_specs=[pl.BlockSpec((tm, f), lambda i, j: (i, 0)),
                  pl.BlockSpec((f, tn), lambda i, j: (0, j)),
                  pl.BlockSpec((tm, tn), lambda i, j: (i, j))],
        out_specs=pl.BlockSpec((tm, tn), lambda i, j: (i, j)),
        out_shape=jax.ShapeDtypeStruct((t, n), F32),
        compiler_params=_params(("parallel", "parallel")),
        name="down_proj",
    )(h, w_down, resid)


TILES = dict(
    proj_tm=1024, proj_tn=512, proj_rb=256,
    conv_tm=1024, conv_tn=512, conv_rb=256,
    attn_tq=1024, attn_tk=256, attn_pw=256,
    out_tm=1024, out_tn=512, out_rb=256,
    gu_tm=1024, gu_tn=512,
    down_tm=1024, down_tn=256,
)


def _rope_tables(seq):
    pos = jnp.arange(seq, dtype=F32)
    inv_freq = ROPE_THETA ** (-jnp.arange(0, HEAD_DIM, 2, dtype=F32) / HEAD_DIM)
    ang = pos[:, None] * inv_freq[None, :]
    cos, sin = jnp.cos(ang), jnp.sin(ang)
    reps = LANES // (HEAD_DIM // 2)
    cos_l = jnp.tile(cos, (1, reps))
    sin_l = jnp.tile(jnp.concatenate([-sin, sin], axis=1), (1, reps // 2))
    return cos_l, sin_l


def kernel(x, attn_norm_w, w_in, q_norm_w, k_norm_w, lambda_q1, lambda_k1,
           lambda_q2, lambda_k2, subln_w, conv_w, w_out, ffn_norm_w, w_gate,
           w_up, w_down):
    b, s, d = x.shape
    t = b * s
    attn_w = N_HEADS * V_DIM
    qk_cols = N_HEADS * 2 * HEAD_DIM
    conv_width = d - attn_w
    col_v = 2 * qk_cols
    col_b = col_v + attn_w
    col_c = col_b + conv_width
    col_h = col_c + conv_width
    tl = TILES
    assert w_in.shape[0] == 1, "single-layer block"

    x2 = x.reshape(t, d)
    cos_l, sin_l = _rope_tables(s)
    scale = math.log2(math.e) / math.sqrt(HEAD_DIM)
    nw = jnp.concatenate([jnp.tile(q_norm_w[0] * scale, qk_cols // HEAD_DIM),
                          jnp.tile(k_norm_w[0], qk_cols // HEAD_DIM)]).reshape(1, -1)
    lam_vecs = jnp.concatenate([lambda_q1, lambda_k1, lambda_q2, lambda_k2], axis=0)

    qkv, xn = _qkv_proj(x2, attn_norm_w[0], w_in[0], nw, cos_l, sin_l, s, col_b,
                        tl["proj_tm"], tl["proj_tn"], tl["proj_rb"])
    conv = _conv_proj(xn, w_in[0], conv_w[0], col_b, col_c, col_h, conv_width, s,
                      tl["conv_tm"], tl["conv_tn"], tl["conv_rb"])

    attn = _attention(qkv.reshape(b, s, col_b), lam_vecs, subln_w[0].reshape(1, V_DIM),
                      tl["attn_tq"], tl["attn_tk"], tl["attn_pw"])

    h1, fn = _out_proj(attn.reshape(t, attn_w), conv, w_out[0], x2, ffn_norm_w[0],
                       tl["out_tm"], tl["out_tn"], tl["out_rb"])
    hid = _gate_up(fn, w_gate[0], w_up[0], tl["gu_tm"], tl["gu_tn"])
    out = _down(hid, w_down[0], h1, tl["down_tm"], tl["down_tn"])
    return out.reshape(b, s, d)
```

```python
import functools
import math

import jax
import jax.numpy as jnp
from jax import lax
from jax.experimental import pallas as pl
from jax.experimental.pallas import tpu as pltpu

F32 = jnp.float32
BF16 = jnp.bfloat16

CHUNK = 64
HEAD_DIM = 64
V_DIM = 2 * HEAD_DIM
N_HEADS = 8
CONV_K = 3
ROPE_THETA = 10000.0
EPS = 1e-6
SUBLN_EPS = 1e-5
LAMBDA_INIT = 0.8 - 0.6 * math.exp(-0.3 * 0)

LANES = 128
SUBLANES = 8
BF16_SUBLANES = 16
VMEM_LIMIT = 56 * 1024 * 1024


def _params(semantics, flags=None):
    return pltpu.CompilerParams(dimension_semantics=semantics,
                                vmem_limit_bytes=VMEM_LIMIT, flags=flags)


def _mm(a_bf16, w_f32):
    return jnp.dot(a_bf16, w_f32.astype(BF16), preferred_element_type=F32)


def _rms_scale(x, w, eps):
    ms = jnp.mean(x * x, axis=-1, keepdims=True)
    return (x * lax.rsqrt(ms + eps)) * w


def _row_block_pipeline(n_blocks, matmul, epilogue):
    acc = matmul(0)
    for r in range(n_blocks):
        nxt = matmul(r + 1) if r + 1 < n_blocks else None
        epilogue(r, acc)
        acc = nxt


def _qkv_kernel(x_ref, anw_ref, w_ref, nw_ref, cos_ref, sin_ref, o_ref, xn_ref,
                w_scr, *, rb, n_qk_tiles):
    j = pl.program_id(1)
    tm, tn = o_ref.shape

    @pl.when(j == 0)
    def _():
        for r in range(tm // rb):
            rows = slice(r * rb, (r + 1) * rb)
            xn_ref[rows, :] = _rms_scale(x_ref[rows, :], anw_ref[...], EPS).astype(BF16)

    w_scr[...] = w_ref[...].astype(BF16)
    lane = lax.broadcasted_iota(jnp.int32, (1, LANES), 1)
    lo_half = lane < HEAD_DIM
    first = (lane % HEAD_DIM) < (HEAD_DIM // 2)

    def matmul(r):
        return jnp.dot(xn_ref[r * rb:(r + 1) * rb, :], w_scr[...],
                       preferred_element_type=F32)

    def plain(r, acc):
        o_ref[r * rb:(r + 1) * rb, :] = acc.astype(o_ref.dtype)

    def norm_rope(r, acc):
        rows = slice(r * rb, (r + 1) * rb)
        cos = cos_ref[rows, :]
        sin = sin_ref[rows, :]
        for c in range(tn // LANES):
            cols = slice(c * LANES, (c + 1) * LANES)
            y = acc[:, cols]
            sq = y * y
            s_lo = jnp.sum(jnp.where(lo_half, sq, 0.0), axis=-1, keepdims=True)
            s_hi = jnp.sum(jnp.where(lo_half, 0.0, sq), axis=-1, keepdims=True)
            ms = jnp.where(lo_half, s_lo, s_hi) * (1.0 / HEAD_DIM)
            yn = (y * lax.rsqrt(ms + EPS)) * nw_ref[:, cols]
            rot = jnp.where(first, pltpu.roll(yn, LANES - HEAD_DIM // 2, 1),
                            pltpu.roll(yn, HEAD_DIM // 2, 1))
            o_ref[rows, cols] = (yn * cos + rot * sin).astype(o_ref.dtype)

    @pl.when(j < n_qk_tiles)
    def _():
        _row_block_pipeline(tm // rb, matmul, norm_rope)

    @pl.when(j >= n_qk_tiles)
    def _():
        _row_block_pipeline(tm // rb, matmul, plain)


def _qkv_proj(x, attn_norm_w, w_in, nw, cos, sin, seq, n_out, tm, tn, rb):
    t, d = x.shape
    n_qk_tiles = nw.shape[1] // tn
    tiles_per_seq = seq // tm
    kern = functools.partial(_qkv_kernel, rb=rb, n_qk_tiles=n_qk_tiles)
    return pl.pallas_call(
        kern,
        grid=(t // tm, n_out // tn),
        in_specs=[pl.BlockSpec((tm, d), lambda i, j: (i, 0)),
                  pl.BlockSpec((1, d), lambda i, j: (0, 0)),
                  pl.BlockSpec((d, tn), lambda i, j: (0, j)),
                  pl.BlockSpec((1, tn), lambda i, j: (0, jnp.minimum(j, n_qk_tiles - 1))),
                  pl.BlockSpec((tm, LANES), lambda i, j: (i % tiles_per_seq, 0)),
                  pl.BlockSpec((tm, LANES), lambda i, j: (i % tiles_per_seq, 0))],
        out_specs=[pl.BlockSpec((tm, tn), lambda i, j: (i, j)),
                   pl.BlockSpec((tm, d), lambda i, j: (i, 0))],
        out_shape=[jax.ShapeDtypeStruct((t, n_out), BF16),
                   jax.ShapeDtypeStruct((t, d), BF16)],
        scratch_shapes=[pltpu.VMEM((d, tn), BF16)],
        compiler_params=_params(("arbitrary", "arbitrary")),
        name="qkv_proj",
    )(x, attn_norm_w.reshape(1, d), w_in, nw, cos, sin)


def _conv_kernel(xn_ref, wb_ref, wc_ref, wh_ref, cw_ref, o_ref,
                 wb_scr, wc_scr, wh_scr, u_scr, carry_scr, *, tiles_per_seq, rb):
    i = pl.program_id(0)
    j = pl.program_id(1)
    tm = xn_ref.shape[0]
    wc_scr[...] = wc_ref[...].astype(BF16)
    wh_scr[...] = wh_ref[...].astype(BF16)
    wb_scr[...] = wb_ref[...].astype(BF16)

    @pl.when(i % tiles_per_seq == 0)
    def _():
        u_scr[0:SUBLANES, :] = jnp.zeros((SUBLANES, u_scr.shape[1]), F32)

    @pl.when(i % tiles_per_seq != 0)
    def _():
        u_scr[0:SUBLANES, :] = carry_scr[j]

    def xn_rows(r):
        return xn_ref[r * rb:(r + 1) * rb, :]

    def gate_matmuls(r):
        return (jnp.dot(xn_rows(r), wc_scr[...], preferred_element_type=F32),
                jnp.dot(xn_rows(r), wh_scr[...], preferred_element_type=F32))

    def store_u(r, acc):
        u_scr[SUBLANES + r * rb:SUBLANES + (r + 1) * rb, :] = acc[0] * acc[1]

    _row_block_pipeline(tm // rb, gate_matmuls, store_u)
    carry_scr[j] = u_scr[tm:tm + SUBLANES, :]
    cw = cw_ref[...]

    def b_matmul(r):
        return jnp.dot(xn_rows(r), wb_scr[...], preferred_element_type=F32)

    def conv_out(r, gate_b):
        def shifted(back):
            start = SUBLANES + r * rb - back
            return u_scr[start:start + rb, :]
        y = cw[2:3, :] * shifted(0) + cw[1:2, :] * shifted(1) + cw[0:1, :] * shifted(2)
        o_ref[r * rb:(r + 1) * rb, :] = (gate_b * y).astype(o_ref.dtype)

    _row_block_pipeline(tm // rb, b_matmul, conv_out)


def _conv_proj(xn, w_in, conv_w, col_b, col_c, col_h, n, seq, tm, tn, rb):
    t, d = xn.shape
    jb, jc, jh = col_b // tn, col_c // tn, col_h // tn
    nj = n // tn
    kern = functools.partial(_conv_kernel, tiles_per_seq=seq // tm, rb=rb)
    w_scratch = [pltpu.VMEM((d, tn), BF16)] * 3
    return pl.pallas_call(
        kern,
        grid=(t // tm, nj),
        in_specs=[pl.BlockSpec((tm, d), lambda i, j: (i, 0)),
                  pl.BlockSpec((d, tn), lambda i, j: (0, jb + j)),
                  pl.BlockSpec((d, tn), lambda i, j: (0, jc + j)),
                  pl.BlockSpec((d, tn), lambda i, j: (0, jh + j)),
                  pl.BlockSpec((CONV_K, tn), lambda i, j: (0, j))],
        out_specs=pl.BlockSpec((tm, tn), lambda i, j: (i, j)),
        out_shape=jax.ShapeDtypeStruct((t, n), BF16),
        scratch_shapes=w_scratch + [pltpu.VMEM((tm + SUBLANES, tn), F32),
                                    pltpu.VMEM((nj, SUBLANES, tn), F32)],
        compiler_params=_params(("arbitrary", "arbitrary")),
        name="conv_proj",
    )(xn, w_in, w_in, w_in, conv_w)


def _attn_kernel(q_ref, k_ref, v_ref, lam_ref, sw_ref, o_ref,
                 qq_scr, vt_scr, bias_scr, m_scr, acc_scr, sa_scr, sb_scr,
                 *, tq, tk, pw):
    qi = pl.program_id(2)
    seq = k_ref.shape[1]

    @pl.when(qi == 0)
    def _():
        for c in range(seq // LANES):
            blk = v_ref[0, c * LANES:(c + 1) * LANES, :].astype(F32)
            vt_scr[0:V_DIM, c * LANES:(c + 1) * LANES] = blk.T.astype(BF16)
        vt_scr[V_DIM:, :] = jnp.ones((vt_scr.shape[0] - V_DIM, seq), BF16)
        key = lax.broadcasted_iota(jnp.int32, bias_scr.shape, 0)
        qry = lax.broadcasted_iota(jnp.int32, bias_scr.shape, 1)
        bias_scr[...] = jnp.where(key // CHUNK <= qry // CHUNK, 0.0, -1e30)

    lane = lax.broadcasted_iota(jnp.int32, (1, LANES), 1)
    q = q_ref[0]
    zero = jnp.zeros_like(q)
    qq_scr[0:tq, :] = jnp.where(lane < HEAD_DIM, q, zero)
    qq_scr[tq:, :] = jnp.where(lane < HEAD_DIM, zero, q)
    m_scr[...] = jnp.full(m_scr.shape, -jnp.inf, F32)
    acc_scr[...] = jnp.zeros(acc_scr.shape, F32)

    n_panels = 2 * tq // pw
    all_visible = ("full",) * n_panels

    def diag_modes(key_off):
        modes = []
        for c in range(n_panels):
            q_off = (c * pw) % tq
            if key_off + tk <= q_off:
                modes.append("full")
            elif key_off >= q_off + pw:
                modes.append("skip")
            else:
                assert key_off == q_off and tk == pw
                modes.append("tri")
        return tuple(modes)

    def scores(t, dst, modes=all_visible):
        k = k_ref[0, pl.ds(pl.multiple_of(t * tk, tk), tk), :]
        for c in range(n_panels):
            if modes[c] == "skip":
                continue
            cols = slice(c * pw, (c + 1) * pw)
            dst[:, cols] = lax.dot_general(
                k, qq_scr[cols, :], (((1,), (1,)), ((), ())),
                preferred_element_type=F32)

    def update(t, src, modes=all_visible):
        vt = vt_scr[:, pl.ds(pl.multiple_of(t * tk, tk), tk)]
        for c in range(n_panels):
            if modes[c] == "skip":
                continue
            cols = slice(c * pw, (c + 1) * pw)
            s = src[:, cols]
            if modes[c] == "tri":
                s = s + bias_scr[...]
            m_prev = m_scr[:, cols]
            m_new = jnp.maximum(m_prev, jnp.max(s, axis=0, keepdims=True))
            alpha = jnp.exp2(m_prev - m_new)
            p = jnp.exp2(s - m_new).astype(BF16)
            m_scr[:, cols] = m_new
            acc_scr[:, cols] = alpha * acc_scr[:, cols] + jnp.dot(
                vt, p, preferred_element_type=F32)

    n_diag = tq // tk
    n_full = n_diag * qi
    bufs = (sa_scr, sb_scr)
    scores(0, sa_scr)

    def body(i, carry):
        t = 2 * i
        scores(t + 1, sb_scr)
        update(t, sa_scr)
        scores(t + 2, sa_scr)
        update(t + 1, sb_scr)
        return carry

    lax.fori_loop(0, n_full // 2, body, 0)
    for d in range(n_diag):
        if d + 1 < n_diag:
            scores(n_full + d + 1, bufs[(d + 1) % 2], diag_modes((d + 1) * tk))
        update(n_full + d, bufs[d % 2], diag_modes(d * tk))

    lq1, lk1, lq2, lk2 = (lam_ref[r:r + 1, :] for r in range(4))
    lam = (jnp.exp(jnp.sum(lq1 * lk1, axis=-1, keepdims=True))
           - jnp.exp(jnp.sum(lq2 * lk2, axis=-1, keepdims=True)) + LAMBDA_INIT)
    acc = acc_scr[...]
    o = acc[0:V_DIM, :] / acc[V_DIM:V_DIM + 1, :]
    a = (o[:, 0:tq] - lam * o[:, tq:]).T
    ms = jnp.mean(a * a, axis=-1, keepdims=True)
    o_ref[0] = (((a * lax.rsqrt(ms + SUBLN_EPS)) * sw_ref[...])
                * (1.0 - LAMBDA_INIT)).astype(o_ref.dtype)


def _attention(qkv, lam_vecs, subln_w, tq, tk, pw):
    b, s, w3 = qkv.shape
    w = w3 // 3
    h = w // V_DIM
    assert tq % (2 * tk) == 0, "the key-tile pipeline is written for pairs of key tiles"
    kern = functools.partial(_attn_kernel, tq=tq, tk=tk, pw=pw)
    return pl.pallas_call(
        kern,
        grid=(b, h, s // tq),
        in_specs=[pl.BlockSpec((1, tq, V_DIM), lambda bi, hi, qi: (bi, qi, hi)),
                  pl.BlockSpec((1, s, V_DIM), lambda bi, hi, qi: (bi, 0, h + hi)),
                  pl.BlockSpec((1, s, V_DIM), lambda bi, hi, qi: (bi, 0, 2 * h + hi)),
                  pl.BlockSpec((4, HEAD_DIM), lambda bi, hi, qi: (0, 0)),
                  pl.BlockSpec((1, V_DIM), lambda bi, hi, qi: (0, 0))],
        out_specs=pl.BlockSpec((1, tq, V_DIM), lambda bi, hi, qi: (bi, qi, hi)),
        out_shape=jax.ShapeDtypeStruct((b, s, w), BF16),
        scratch_shapes=[pltpu.VMEM((2 * tq, V_DIM), BF16),
                        pltpu.VMEM((V_DIM + BF16_SUBLANES, s), BF16),
                        pltpu.VMEM((tk, pw), F32),
                        pltpu.VMEM((1, 2 * tq), F32),
                        pltpu.VMEM((V_DIM + BF16_SUBLANES, 2 * tq), F32),
                        pltpu.VMEM((tk, 2 * tq), F32),
                        pltpu.VMEM((tk, 2 * tq), F32)],
        compiler_params=_params(("parallel", "parallel", "arbitrary")),
        name="diff_attention",
    )(qkv, qkv, qkv, lam_vecs, subln_w)


def _out_kernel(a_ref, c_ref, wa_ref, wc_ref, x_ref, fnw_ref, o_ref, fn_ref, h_scr,
                *, rb):
    j = pl.program_id(1)
    n_j, tm, tn = h_scr.shape
    h = x_ref[...] + _mm(a_ref[...], wa_ref[...]) + _mm(c_ref[...], wc_ref[...])
    o_ref[...] = h
    h_scr[j] = h

    @pl.when(j == n_j - 1)
    def _():
        for r in range(tm // rb):
            rows = slice(r * rb, (r + 1) * rb)
            ssq = sum(jnp.sum(h_scr[jj, rows, :] * h_scr[jj, rows, :], axis=-1, keepdims=True)
                      for jj in range(n_j))
            scale = lax.rsqrt(ssq * (1.0 / (n_j * tn)) + EPS)
            for jj in range(n_j):
                cols = slice(jj * tn, (jj + 1) * tn)
                fn_ref[rows, cols] = ((h_scr[jj, rows, :] * scale)
                                      * fnw_ref[:, cols]).astype(fn_ref.dtype)


def _out_proj(attn, conv, w_out, x, ffn_norm_w, tm, tn, rb):
    t, ka = attn.shape
    kc = conv.shape[1]
    n = w_out.shape[1]
    assert ka == kc
    return pl.pallas_call(
        functools.partial(_out_kernel, rb=rb),
        grid=(t // tm, n // tn),
        in_specs=[pl.BlockSpec((tm, ka), lambda i, j: (i, 0)),
                  pl.BlockSpec((tm, kc), lambda i, j: (i, 0)),
                  pl.BlockSpec((ka, tn), lambda i, j: (0, j)),
                  pl.BlockSpec((kc, tn), lambda i, j: (1, j)),
                  pl.BlockSpec((tm, tn), lambda i, j: (i, j)),
                  pl.BlockSpec((1, n), lambda i, j: (0, 0))],
        out_specs=[pl.BlockSpec((tm, tn), lambda i, j: (i, j)),
                   pl.BlockSpec((tm, n), lambda i, j: (i, 0))],
        out_shape=[jax.ShapeDtypeStruct((t, n), F32),
                   jax.ShapeDtypeStruct((t, n), BF16)],
        scratch_shapes=[pltpu.VMEM((n // tn, tm, tn), F32)],
        compiler_params=_params(("arbitrary", "arbitrary")),
        name="out_proj",
    )(attn, conv, w_out, w_out, x, ffn_norm_w.reshape(1, n))


def _gate_up_kernel(x_ref, wg_ref, wu_ref, o_ref):
    x = x_ref[...]
    g = _mm(x, wg_ref[...])
    u = _mm(x, wu_ref[...])
    o_ref[...] = ((g * jax.nn.sigmoid(g)) * u).astype(o_ref.dtype)


def _gate_up(fn, w_gate, w_up, tm, tn):
    t, d = fn.shape
    f = w_gate.shape[1]
    return pl.pallas_call(
        _gate_up_kernel,
        grid=(t // tm, f // tn),
        in_specs=[pl.BlockSpec((tm, d), lambda i, j: (i, 0)),
                  pl.BlockSpec((d, tn), lambda i, j: (0, j)),
                  pl.BlockSpec((d, tn), lambda i, j: (0, j))],
        out_specs=pl.BlockSpec((tm, tn), lambda i, j: (i, j)),
        out_shape=jax.ShapeDtypeStruct((t, f), BF16),
        compiler_params=_params(("parallel", "parallel")),
        name="gate_up",
    )(fn, w_gate, w_up)


def _down_kernel(h_ref, w_ref, r_ref, o_ref):
    o_ref[...] = r_ref[...] + _mm(h_ref[...], w_ref[...])


def _down(h, w_down, resid, tm, tn):
    t, f = h.shape
    n = w_down.shape[1]
    return pl.pallas_call(
        _down_kernel,
        grid=(t // tm, n // tn),
        in_specs=[pl.BlockSpec((tm, f), lambda i, j: (i, 0)),
                  pl.BlockSpec((f, tn), lambda i, j: (0, j)),
                  pl.BlockSpec((tm, tn), lambda i, j: (i, j))],
        out_specs=pl.BlockSpec((tm, tn), lambda i, j: (i, j)),
        out_shape=jax.ShapeDtypeStruct((t, n), F32),
        compiler_params=_params(("parallel", "parallel")),
        name="down_proj",
    )(h, w_down, resid)


TILES = dict(
    proj_tm=1024, proj_tn=512, proj_rb=256,
    conv_tm=1024, conv_tn=512, conv_rb=256,
    attn_tq=1024, attn_tk=256, attn_pw=256,
    out_tm=1024, out_tn=512, out_rb=256,
    gu_tm=1024, gu_tn=512,
    down_tm=1024, down_tn=256,
)


def _rope_tables(seq):
    pos = jnp.arange(seq, dtype=F32)
    inv_freq = ROPE_THETA ** (-jnp.arange(0, HEAD_DIM, 2, dtype=F32) / HEAD_DIM)
    ang = pos[:, None] * inv_freq[None, :]
    cos, sin = jnp.cos(ang), jnp.sin(ang)
    reps = LANES // (HEAD_DIM // 2)
    cos_l = jnp.tile(cos, (1, reps))
    sin_l = jnp.tile(jnp.concatenate([-sin, sin], axis=1), (1, reps // 2))
    return cos_l, sin_l


def kernel(x, attn_norm_w, w_in, q_norm_w, k_norm_w, lambda_q1, lambda_k1,
           lambda_q2, lambda_k2, subln_w, conv_w, w_out, ffn_norm_w, w_gate,
           w_up, w_down):
    b, s, d = x.shape
    t = b * s
    attn_w = N_HEADS * V_DIM
    qk_cols = N_HEADS * 2 * HEAD_DIM
    conv_width = d - attn_w
    col_v = 2 * qk_cols
    col_b = col_v + attn_w
    col_c = col_b + conv_width
    col_h = col_c + conv_width
    tl = TILES
    assert w_in.shape[0] == 1, "single-layer block"

    x2 = x.reshape(t, d)
    cos_l, sin_l = _rope_tables(s)
    scale = math.log2(math.e) / math.sqrt(HEAD_DIM)
    nw = jnp.concatenate([jnp.tile(q_norm_w[0] * scale, qk_cols // HEAD_DIM),
                          jnp.tile(k_norm_w[0], qk_cols // HEAD_DIM)]).reshape(1, -1)
    lam_vecs = jnp.concatenate([lambda_q1, lambda_k1, lambda_q2, lambda_k2], axis=0)

    qkv, xn = _qkv_proj(x2, attn_norm_w[0], w_in[0], nw, cos_l, sin_l, s, col_b,
                        tl["proj_tm"], tl["proj_tn"], tl["proj_rb"])
    conv = _conv_proj(xn, w_in[0], conv_w[0], col_b, col_c, col_h, conv_width, s,
                      tl["conv_tm"], tl["conv_tn"], tl["conv_rb"])

    attn = _attention(qkv.reshape(b, s, col_b), lam_vecs, subln_w[0].reshape(1, V_DIM),
                      tl["attn_tq"], tl["attn_tk"], tl["attn_pw"])

    h1, fn = _out_proj(attn.reshape(t, attn_w), conv, w_out[0].astype(BF16), x2,
                       ffn_norm_w[0],
                       tl["out_tm"], tl["out_tn"], tl["out_rb"])
    hid = _gate_up(fn, w_gate[0], w_up[0], tl["gu_tm"], tl["gu_tn"])
    out = _down(hid, w_down[0].astype(BF16), h1, tl["down_tm"], tl["down_tn"])
    return out.reshape(b, s, d)
```

```python
import functools
import math

import jax
import jax.numpy as jnp
from jax import lax
from jax.experimental import pallas as pl
from jax.experimental.pallas import tpu as pltpu

F32 = jnp.float32
BF16 = jnp.bfloat16

CHUNK = 64
HEAD_DIM = 64
V_DIM = 2 * HEAD_DIM
N_HEADS = 8
CONV_K = 3
ROPE_THETA = 10000.0
EPS = 1e-6
SUBLN_EPS = 1e-5
LAMBDA_INIT = 0.8 - 0.6 * math.exp(-0.3 * 0)

LANES = 128
SUBLANES = 8
BF16_SUBLANES = 16
VMEM_LIMIT = 56 * 1024 * 1024


def _params(semantics, flags=None):
    return pltpu.CompilerParams(dimension_semantics=semantics,
                                vmem_limit_bytes=VMEM_LIMIT, flags=flags)


def _mm(a_bf16, w_f32):
    return jnp.dot(a_bf16, w_f32.astype(BF16), preferred_element_type=F32)


def _rms_scale(x, w, eps):
    ms = jnp.mean(x * x, axis=-1, keepdims=True)
    return (x * lax.rsqrt(ms + eps)) * w


def _row_block_pipeline(n_blocks, matmul, epilogue):
    acc = matmul(0)
    for r in range(n_blocks):
        nxt = matmul(r + 1) if r + 1 < n_blocks else None
        epilogue(r, acc)
        acc = nxt


def _qkv_kernel(x_ref, anw_ref, w_ref, nw_ref, cos_ref, sin_ref, o_ref, xn_ref,
                w_scr, *, rb, n_qk_tiles):
    j = pl.program_id(1)
    tm, tn = o_ref.shape

    @pl.when(j == 0)
    def _():
        for r in range(tm // rb):
            rows = slice(r * rb, (r + 1) * rb)
            xn_ref[rows, :] = _rms_scale(x_ref[rows, :], anw_ref[...], EPS).astype(BF16)

    w_scr[...] = w_ref[...].astype(BF16)
    lane = lax.broadcasted_iota(jnp.int32, (1, LANES), 1)
    lo_half = lane < HEAD_DIM
    first = (lane % HEAD_DIM) < (HEAD_DIM // 2)

    def matmul(r):
        return jnp.dot(xn_ref[r * rb:(r + 1) * rb, :], w_scr[...],
                       preferred_element_type=F32)

    def plain(r, acc):
        o_ref[r * rb:(r + 1) * rb, :] = acc.astype(o_ref.dtype)

    def norm_rope(r, acc):
        rows = slice(r * rb, (r + 1) * rb)
        cos = cos_ref[rows, :]
        sin = sin_ref[rows, :]
        for c in range(tn // LANES):
            cols = slice(c * LANES, (c + 1) * LANES)
            y = acc[:, cols]
            sq = y * y
            s_lo = jnp.sum(jnp.where(lo_half, sq, 0.0), axis=-1, keepdims=True)
            s_hi = jnp.sum(jnp.where(lo_half, 0.0, sq), axis=-1, keepdims=True)
            ms = jnp.where(lo_half, s_lo, s_hi) * (1.0 / HEAD_DIM)
            yn = (y * lax.rsqrt(ms + EPS)) * nw_ref[:, cols]
            rot = jnp.where(first, pltpu.roll(yn, LANES - HEAD_DIM // 2, 1),
                            pltpu.roll(yn, HEAD_DIM // 2, 1))
            o_ref[rows, cols] = (yn * cos + rot * sin).astype(o_ref.dtype)

    @pl.when(j < n_qk_tiles)
    def _():
        _row_block_pipeline(tm // rb, matmul, norm_rope)

    @pl.when(j >= n_qk_tiles)
    def _():
        _row_block_pipeline(tm // rb, matmul, plain)


def _qkv_proj(x, attn_norm_w, w_in, nw, cos, sin, seq, n_out, tm, tn, rb):
    t, d = x.shape
    n_qk_tiles = nw.shape[1] // tn
    tiles_per_seq = seq // tm
    kern = functools.partial(_qkv_kernel, rb=rb, n_qk_tiles=n_qk_tiles)
    return pl.pallas_call(
        kern,
        grid=(t // tm, n_out // tn),
        in_specs=[pl.BlockSpec((tm, d), lambda i, j: (i, 0)),
                  pl.BlockSpec((1, d), lambda i, j: (0, 0)),
                  pl.BlockSpec((d, tn), lambda i, j: (0, j)),
                  pl.BlockSpec((1, tn), lambda i, j: (0, jnp.minimum(j, n_qk_tiles - 1))),
                  pl.BlockSpec((tm, LANES), lambda i, j: (i % tiles_per_seq, 0)),
                  pl.BlockSpec((tm, LANES), lambda i, j: (i % tiles_per_seq, 0))],
        out_specs=[pl.BlockSpec((tm, tn), lambda i, j: (i, j)),
                   pl.BlockSpec((tm, d), lambda i, j: (i, 0))],
        out_shape=[jax.ShapeDtypeStruct((t, n_out), BF16),
                   jax.ShapeDtypeStruct((t, d), BF16)],
        scratch_shapes=[pltpu.VMEM((d, tn), BF16)],
        compiler_params=_params(("arbitrary", "arbitrary")),
        name="qkv_proj",
    )(x, attn_norm_w.reshape(1, d), w_in, nw, cos, sin)


def _conv_kernel(xn_ref, wb_ref, wc_ref, wh_ref, cw_ref, o_ref,
                 wb_scr, wc_scr, wh_scr, u_scr, carry_scr, *, tiles_per_seq, rb):
    i = pl.program_id(0)
    j = pl.program_id(1)
    tm = xn_ref.shape[0]
    wc_scr[...] = wc_ref[...].astype(BF16)
    wh_scr[...] = wh_ref[...].astype(BF16)
    wb_scr[...] = wb_ref[...].astype(BF16)

    @pl.when(i % tiles_per_seq == 0)
    def _():
        u_scr[0:SUBLANES, :] = jnp.zeros((SUBLANES, u_scr.shape[1]), F32)

    @pl.when(i % tiles_per_seq != 0)
    def _():
        u_scr[0:SUBLANES, :] = carry_scr[j]

    def xn_rows(r):
        return xn_ref[r * rb:(r + 1) * rb, :]

    def gate_matmuls(r):
        return (jnp.dot(xn_rows(r), wc_scr[...], preferred_element_type=F32),
                jnp.dot(xn_rows(r), wh_scr[...], preferred_element_type=F32))

    def store_u(r, acc):
        u_scr[SUBLANES + r * rb:SUBLANES + (r + 1) * rb, :] = acc[0] * acc[1]

    _row_block_pipeline(tm // rb, gate_matmuls, store_u)
    carry_scr[j] = u_scr[tm:tm + SUBLANES, :]
    cw = cw_ref[...]

    def b_matmul(r):
        return jnp.dot(xn_rows(r), wb_scr[...], preferred_element_type=F32)

    def conv_out(r, gate_b):
        def shifted(back):
            start = SUBLANES + r * rb - back
            return u_scr[start:start + rb, :]
        y = cw[2:3, :] * shifted(0) + cw[1:2, :] * shifted(1) + cw[0:1, :] * shifted(2)
        o_ref[r * rb:(r + 1) * rb, :] = (gate_b * y).astype(o_ref.dtype)

    _row_block_pipeline(tm // rb, b_matmul, conv_out)


def _conv_proj(xn, w_in, conv_w, col_b, col_c, col_h, n, seq, tm, tn, rb):
    t, d = xn.shape
    jb, jc, jh = col_b // tn, col_c // tn, col_h // tn
    nj = n // tn
    kern = functools.partial(_conv_kernel, tiles_per_seq=seq // tm, rb=rb)
    w_scratch = [pltpu.VMEM((d, tn), BF16)] * 3
    return pl.pallas_call(
        kern,
        grid=(t // tm, nj),
        in_specs=[pl.BlockSpec((tm, d), lambda i, j: (i, 0)),
                  pl.BlockSpec((d, tn), lambda i, j: (0, jb + j)),
                  pl.BlockSpec((d, tn), lambda i, j: (0, jc + j)),
                  pl.BlockSpec((d, tn), lambda i, j: (0, jh + j)),
                  pl.BlockSpec((CONV_K, tn), lambda i, j: (0, j))],
        out_specs=pl.BlockSpec((tm, tn), lambda i, j: (i, j)),
        out_shape=jax.ShapeDtypeStruct((t, n), BF16),
        scratch_shapes=w_scratch + [pltpu.VMEM((tm + SUBLANES, tn), F32),
                                    pltpu.VMEM((nj, SUBLANES, tn), F32)],
        compiler_params=_params(("arbitrary", "arbitrary")),
        name="conv_proj",
    )(xn, w_in, w_in, w_in, conv_w)


def _attn_kernel(q_ref, k_ref, v_ref, lam_ref, sw_ref, o_ref,
                 qq_scr, vt_scr, bias_scr, m_scr, acc_scr, *s_bufs,
                 tq, tk, pw, ahead):
    qi = pl.program_id(2)
    seq = k_ref.shape[1]

    @pl.when(qi == 0)
    def _():
        for c in range(seq // LANES):
            blk = v_ref[0, c * LANES:(c + 1) * LANES, :].astype(F32)
            vt_scr[0:V_DIM, c * LANES:(c + 1) * LANES] = blk.T.astype(BF16)
        vt_scr[V_DIM:, 0:seq] = jnp.ones((vt_scr.shape[0] - V_DIM, seq), BF16)
        key = lax.broadcasted_iota(jnp.int32, bias_scr.shape, 0)
        qry = lax.broadcasted_iota(jnp.int32, bias_scr.shape, 1)
        bias_scr[...] = jnp.where(key // CHUNK <= qry // CHUNK, 0.0, -1e30)

    dim = lax.broadcasted_iota(jnp.int32, (V_DIM, 1), 0)
    for c in range(tq // LANES):
        qt = q_ref[0, c * LANES:(c + 1) * LANES, :].astype(F32).T
        cols = slice(c * LANES, (c + 1) * LANES)
        qq_scr[:, cols] = jnp.where(dim < HEAD_DIM, qt, 0.0).astype(BF16)
        cols = slice(tq + c * LANES, tq + (c + 1) * LANES)
        qq_scr[:, cols] = jnp.where(dim < HEAD_DIM, 0.0, qt).astype(BF16)
    m_scr[...] = jnp.full(m_scr.shape, -jnp.inf, F32)
    acc_scr[...] = jnp.zeros(acc_scr.shape, F32)

    n_panels = 2 * tq // pw
    all_visible = ("full",) * n_panels

    def diag_modes(key_off):
        modes = []
        for c in range(n_panels):
            q_off = (c * pw) % tq
            if key_off + tk <= q_off:
                modes.append("full")
            elif key_off >= q_off + pw:
                modes.append("skip")
            else:
                assert key_off == q_off and tk == pw
                modes.append("tri")
        return tuple(modes)

    def scores(t, dst, modes=all_visible):
        k = k_ref[0, pl.ds(pl.multiple_of(t * tk, tk), tk), :]
        for c in range(n_panels):
            if modes[c] == "skip":
                continue
            cols = slice(c * pw, (c + 1) * pw)
            dst[:, cols] = jnp.dot(k, qq_scr[:, cols], preferred_element_type=F32)

    def update(t, src, modes=all_visible):
        vt = vt_scr[:, pl.ds(pl.multiple_of(t * tk, tk), tk)]
        for c in range(n_panels):
            if modes[c] == "skip":
                continue
            cols = slice(c * pw, (c + 1) * pw)
            s = src[:, cols]
            if modes[c] == "tri":
                s = s + bias_scr[...]
            m_prev = m_scr[:, cols]
            m_new = jnp.maximum(m_prev, jnp.max(s, axis=0, keepdims=True))
            alpha = jnp.exp2(m_prev - m_new)
            p = jnp.exp2(s - m_new).astype(BF16)
            m_scr[:, cols] = m_new
            acc_scr[:, cols] = alpha * acc_scr[:, cols] + jnp.dot(
                vt, p, preferred_element_type=F32)

    n_diag = tq // tk
    n_full = n_diag * qi
    n_buf = len(s_bufs)
    assert n_buf == n_diag and ahead < n_buf
    for u in range(ahead):
        scores(u, s_bufs[u])

    def body(i, carry):
        t = n_buf * i
        for u in range(ahead, n_buf):
            scores(t + u, s_bufs[u])
        for u in range(n_buf):
            update(t + u, s_bufs[u])
            if u < ahead:
                scores(t + n_buf + u, s_bufs[u])
        return carry

    lax.fori_loop(0, qi, body, 0)
    for d in range(ahead, n_diag):
        scores(n_full + d, s_bufs[d], diag_modes(d * tk))
    for d in range(n_diag):
        update(n_full + d, s_bufs[d], diag_modes(d * tk))

    lq1, lk1, lq2, lk2 = (lam_ref[r:r + 1, :] for r in range(4))
    lam = (jnp.exp(jnp.sum(lq1 * lk1, axis=-1, keepdims=True))
           - jnp.exp(jnp.sum(lq2 * lk2, axis=-1, keepdims=True)) + LAMBDA_INIT)
    acc = acc_scr[:, 0:2 * tq]
    o = acc[0:V_DIM, :] / acc[V_DIM:V_DIM + 1, :]
    a = (o[:, 0:tq] - lam * o[:, tq:]).T
    ms = jnp.mean(a * a, axis=-1, keepdims=True)
    o_ref[0] = (((a * lax.rsqrt(ms + SUBLN_EPS)) * sw_ref[...])
                * (1.0 - LAMBDA_INIT)).astype(o_ref.dtype)


def _attention(qkv, lam_vecs, subln_w, tq, tk, pw, ahead):
    b, s, w3 = qkv.shape
    w = w3 // 3
    h = w // V_DIM
    kern = functools.partial(_attn_kernel, tq=tq, tk=tk, pw=pw, ahead=ahead)
    return pl.pallas_call(
        kern,
        grid=(b, h, s // tq),
        in_specs=[pl.BlockSpec((1, tq, V_DIM), lambda bi, hi, qi: (bi, qi, hi)),
                  pl.BlockSpec((1, s, V_DIM), lambda bi, hi, qi: (bi, 0, h + hi)),
                  pl.BlockSpec((1, s, V_DIM), lambda bi, hi, qi: (bi, 0, 2 * h + hi)),
                  pl.BlockSpec((4, HEAD_DIM), lambda bi, hi, qi: (0, 0)),
                  pl.BlockSpec((1, V_DIM), lambda bi, hi, qi: (0, 0))],
        out_specs=pl.BlockSpec((1, tq, V_DIM), lambda bi, hi, qi: (bi, qi, hi)),
        out_shape=jax.ShapeDtypeStruct((b, s, w), BF16),
        scratch_shapes=[pltpu.VMEM((V_DIM, 2 * tq), BF16),
                        pltpu.VMEM((V_DIM + BF16_SUBLANES, s + LANES), BF16),
                        pltpu.VMEM((tk, pw), F32),
                        pltpu.VMEM((1, 2 * tq), F32),
                        pltpu.VMEM((V_DIM + BF16_SUBLANES, 2 * tq + LANES), F32)]
        + [pltpu.VMEM((tk, 2 * tq + LANES), F32)] * (tq // tk),
        compiler_params=_params(("parallel", "parallel", "arbitrary")),
        name="diff_attention",
    )(qkv, qkv, qkv, lam_vecs, subln_w)


def _out_kernel(a_ref, c_ref, wa_ref, wc_ref, x_ref, fnw_ref, o_ref, fn_ref, h_scr,
                *, rb):
    j = pl.program_id(1)
    n_j, tm, tn = h_scr.shape
    h = x_ref[...] + _mm(a_ref[...], wa_ref[...]) + _mm(c_ref[...], wc_ref[...])
    o_ref[...] = h
    h_scr[j] = h

    @pl.when(j == n_j - 1)
    def _():
        for r in range(tm // rb):
            rows = slice(r * rb, (r + 1) * rb)
            ssq = sum(jnp.sum(h_scr[jj, rows, :] * h_scr[jj, rows, :], axis=-1, keepdims=True)
                      for jj in range(n_j))
            scale = lax.rsqrt(ssq * (1.0 / (n_j * tn)) + EPS)
            for jj in range(n_j):
                cols = slice(jj * tn, (jj + 1) * tn)
                fn_ref[rows, cols] = ((h_scr[jj, rows, :] * scale)
                                      * fnw_ref[:, cols]).astype(fn_ref.dtype)


def _out_proj(attn, conv, w_out, x, ffn_norm_w, tm, tn, rb):
    t, ka = attn.shape
    kc = conv.shape[1]
    n = w_out.shape[1]
    assert ka == kc
    return pl.pallas_call(
        functools.partial(_out_kernel, rb=rb),
        grid=(t // tm, n // tn),
        in_specs=[pl.BlockSpec((tm, ka), lambda i, j: (i, 0)),
                  pl.BlockSpec((tm, kc), lambda i, j: (i, 0)),
                  pl.BlockSpec((ka, tn), lambda i, j: (0, j)),
                  pl.BlockSpec((kc, tn), lambda i, j: (1, j)),
                  pl.BlockSpec((tm, tn), lambda i, j: (i, j)),
                  pl.BlockSpec((1, n), lambda i, j: (0, 0))],
        out_specs=[pl.BlockSpec((tm, tn), lambda i, j: (i, j)),
                   pl.BlockSpec((tm, n), lambda i, j: (i, 0))],
        out_shape=[jax.ShapeDtypeStruct((t, n), F32),
                   jax.ShapeDtypeStruct((t, n), BF16)],
        scratch_shapes=[pltpu.VMEM((n // tn, tm, tn), F32)],
        compiler_params=_params(("arbitrary", "arbitrary")),
        name="out_proj",
    )(attn, conv, w_out, w_out, x, ffn_norm_w.reshape(1, n))


def _gate_up_kernel(x_ref, wg_ref, wu_ref, o_ref):
    x = x_ref[...]
    g = _mm(x, wg_ref[...])
    u = _mm(x, wu_ref[...])
    o_ref[...] = ((g * jax.nn.sigmoid(g)) * u).astype(o_ref.dtype)


def _gate_up(fn, w_gate, w_up, tm, tn):
    t, d = fn.shape
    f = w_gate.shape[1]
    return pl.pallas_call(
        _gate_up_kernel,
        grid=(t // tm, f // tn),
        in_specs=[pl.BlockSpec((tm, d), lambda i, j: (i, 0)),
                  pl.BlockSpec((d, tn), lambda i, j: (0, j)),
                  pl.BlockSpec((d, tn), lambda i, j: (0, j))],
        out_specs=pl.BlockSpec((tm, tn), lambda i, j: (i, j)),
        out_shape=jax.ShapeDtypeStruct((t, f), BF16),
        compiler_params=_params(("parallel", "parallel")),
        name="gate_up",
    )(fn, w_gate, w_up)


def _down_kernel(h_ref, w_ref, r_ref, o_ref):
    o_ref[...] = r_ref[...] + _mm(h_ref[...], w_ref[...])


def _down(h, w_down, resid, tm, tn):
    t, f = h.shape
    n = w_down.shape[1]
    return pl.pallas_call(
        _down_kernel,
        grid=(t // tm, n // tn),
        in_specs=[pl.BlockSpec((tm, f), lambda i, j: (i, 0)),
                  pl.BlockSpec((f, tn), lambda i, j: (0, j)),
                  pl.BlockSpec((tm, tn), lambda i, j: (i, j))],
        out_specs=pl.BlockSpec((tm, tn), lambda i, j: (i, j)),
        out_shape=jax.ShapeDtypeStruct((t, n), F32),
        compiler_params=_params(("parallel", "parallel")),
        name="down_proj",
    )(h, w_down, resid)


TILES = dict(
    proj_tm=1024, proj_tn=512, proj_rb=256,
    conv_tm=1024, conv_tn=512, conv_rb=256,
    attn_tq=1024, attn_tk=256, attn_pw=256, attn_ahead=2,
    out_tm=1024, out_tn=512, out_rb=256,
    gu_tm=1024, gu_tn=512,
    down_tm=1024, down_tn=256,
)


def _rope_tables(seq):
    pos = jnp.arange(seq, dtype=F32)
    inv_freq = ROPE_THETA ** (-jnp.arange(0, HEAD_DIM, 2, dtype=F32) / HEAD_DIM)
    ang = pos[:, None] * inv_freq[None, :]
    cos, sin = jnp.cos(ang), jnp.sin(ang)
    reps = LANES // (HEAD_DIM // 2)
    cos_l = jnp.tile(cos, (1, reps))
    sin_l = jnp.tile(jnp.concatenate([-sin, sin], axis=1), (1, reps // 2))
    return cos_l, sin_l


def kernel(x, attn_norm_w, w_in, q_norm_w, k_norm_w, lambda_q1, lambda_k1,
           lambda_q2, lambda_k2, subln_w, conv_w, w_out, ffn_norm_w, w_gate,
           w_up, w_down):
    b, s, d = x.shape
    t = b * s
    attn_w = N_HEADS * V_DIM
    qk_cols = N_HEADS * 2 * HEAD_DIM
    conv_width = d - attn_w
    col_v = 2 * qk_cols
    col_b = col_v + attn_w
    col_c = col_b + conv_width
    col_h = col_c + conv_width
    tl = TILES
    assert w_in.shape[0] == 1, "single-layer block"

    x2 = x.reshape(t, d)
    cos_l, sin_l = _rope_tables(s)
    scale = math.log2(math.e) / math.sqrt(HEAD_DIM)
    nw = jnp.concatenate([jnp.tile(q_norm_w[0] * scale, qk_cols // HEAD_DIM),
                          jnp.tile(k_norm_w[0], qk_cols // HEAD_DIM)]).reshape(1, -1)
    lam_vecs = jnp.concatenate([lambda_q1, lambda_k1, lambda_q2, lambda_k2], axis=0)

    qkv, xn = _qkv_proj(x2, attn_norm_w[0], w_in[0], nw, cos_l, sin_l, s, col_b,
                        tl["proj_tm"], tl["proj_tn"], tl["proj_rb"])
    conv = _conv_proj(xn, w_in[0], conv_w[0], col_b, col_c, col_h, conv_width, s,
                      tl["conv_tm"], tl["conv_tn"], tl["conv_rb"])

    attn = _attention(qkv.reshape(b, s, col_b), lam_vecs, subln_w[0].reshape(1, V_DIM),
                      tl["attn_tq"], tl["attn_tk"], tl["attn_pw"], tl["attn_ahead"])

    h1, fn = _out_proj(attn.reshape(t, attn_w), conv, w_out[0], x2, ffn_norm_w[0],
                       tl["out_tm"], tl["out_tn"], tl["out_rb"])
    hid = _gate_up(fn, w_gate[0], w_up[0], tl["gu_tm"], tl["gu_tn"])
    out = _down(hid, w_down[0], h1, tl["down_tm"], tl["down_tn"])
    return out.reshape(b, s, d)
```

```python
import functools
import math

import jax
import jax.numpy as jnp
from jax import lax
from jax.experimental import pallas as pl
from jax.experimental.pallas import tpu as pltpu

F32 = jnp.float32
BF16 = jnp.bfloat16

CHUNK = 64
HEAD_DIM = 64
V_DIM = 2 * HEAD_DIM
N_HEADS = 8
CONV_K = 3
ROPE_THETA = 10000.0
EPS = 1e-6
SUBLN_EPS = 1e-5
LAMBDA_INIT = 0.8 - 0.6 * math.exp(-0.3 * 0)

LANES = 128
SUBLANES = 8
BF16_SUBLANES = 16
VMEM_LIMIT = 56 * 1024 * 1024


def _params(semantics, flags=None):
    return pltpu.CompilerParams(dimension_semantics=semantics,
                                vmem_limit_bytes=VMEM_LIMIT, flags=flags)


def _mm(a_bf16, w_f32):
    return jnp.dot(a_bf16, w_f32.astype(BF16), preferred_element_type=F32)


def _rms_scale(x, w, eps):
    ms = jnp.mean(x * x, axis=-1, keepdims=True)
    return (x * lax.rsqrt(ms + eps)) * w


def _row_block_pipeline(n_blocks, matmul, epilogue):
    acc = matmul(0)
    for r in range(n_blocks):
        nxt = matmul(r + 1) if r + 1 < n_blocks else None
        epilogue(r, acc)
        acc = nxt


def _qkv_kernel(x_ref, anw_ref, w_ref, nw_ref, cos_ref, sin_ref, o_ref, xn_ref,
                w_scr, *, rb, n_qk_tiles):
    j = pl.program_id(1)
    tm, tn = o_ref.shape

    @pl.when(j == 0)
    def _():
        for r in range(tm // rb):
            rows = slice(r * rb, (r + 1) * rb)
            xn_ref[rows, :] = _rms_scale(x_ref[rows, :], anw_ref[...], EPS).astype(BF16)

    w_scr[...] = w_ref[...].astype(BF16)
    lane = lax.broadcasted_iota(jnp.int32, (1, LANES), 1)
    lo_half = lane < HEAD_DIM
    first = (lane % HEAD_DIM) < (HEAD_DIM // 2)

    def matmul(r):
        return jnp.dot(xn_ref[r * rb:(r + 1) * rb, :], w_scr[...],
                       preferred_element_type=F32)

    def plain(r, acc):
        o_ref[r * rb:(r + 1) * rb, :] = acc.astype(o_ref.dtype)

    def norm_rope(r, acc):
        rows = slice(r * rb, (r + 1) * rb)
        cos = cos_ref[rows, :]
        sin = sin_ref[rows, :]
        for c in range(tn // LANES):
            cols = slice(c * LANES, (c + 1) * LANES)
            y = acc[:, cols]
            sq = y * y
            s_lo = jnp.sum(jnp.where(lo_half, sq, 0.0), axis=-1, keepdims=True)
            s_hi = jnp.sum(jnp.where(lo_half, 0.0, sq), axis=-1, keepdims=True)
            ms = jnp.where(lo_half, s_lo, s_hi) * (1.0 / HEAD_DIM)
            yn = (y * lax.rsqrt(ms + EPS)) * nw_ref[:, cols]
            rot = jnp.where(first, pltpu.roll(yn, LANES - HEAD_DIM // 2, 1),
                            pltpu.roll(yn, HEAD_DIM // 2, 1))
            o_ref[rows, cols] = (yn * cos + rot * sin).astype(o_ref.dtype)

    @pl.when(j < n_qk_tiles)
    def _():
        _row_block_pipeline(tm // rb, matmul, norm_rope)

    @pl.when(j >= n_qk_tiles)
    def _():
        _row_block_pipeline(tm // rb, matmul, plain)


def _qkv_proj(x, attn_norm_w, w_in, nw, cos, sin, seq, n_out, tm, tn, rb):
    t, d = x.shape
    n_qk_tiles = nw.shape[1] // tn
    tiles_per_seq = seq // tm
    kern = functools.partial(_qkv_kernel, rb=rb, n_qk_tiles=n_qk_tiles)
    return pl.pallas_call(
        kern,
        grid=(t // tm, n_out // tn),
        in_specs=[pl.BlockSpec((tm, d), lambda i, j: (i, 0)),
                  pl.BlockSpec((1, d), lambda i, j: (0, 0)),
                  pl.BlockSpec((d, tn), lambda i, j: (0, j)),
                  pl.BlockSpec((1, tn), lambda i, j: (0, jnp.minimum(j, n_qk_tiles - 1))),
                  pl.BlockSpec((tm, LANES), lambda i, j: (i % tiles_per_seq, 0)),
                  pl.BlockSpec((tm, LANES), lambda i, j: (i % tiles_per_seq, 0))],
        out_specs=[pl.BlockSpec((tm, tn), lambda i, j: (i, j)),
                   pl.BlockSpec((tm, d), lambda i, j: (i, 0))],
        out_shape=[jax.ShapeDtypeStruct((t, n_out), BF16),
                   jax.ShapeDtypeStruct((t, d), BF16)],
        scratch_shapes=[pltpu.VMEM((d, tn), BF16)],
        compiler_params=_params(("arbitrary", "arbitrary")),
        name="qkv_proj",
    )(x, attn_norm_w.reshape(1, d), w_in, nw, cos, sin)


def _conv_kernel(xn_ref, wb_ref, wc_ref, wh_ref, cw_ref, o_ref,
                 wb_scr, wc_scr, wh_scr, u_scr, carry_scr, *, tiles_per_seq, rb):
    i = pl.program_id(0)
    j = pl.program_id(1)
    tm = xn_ref.shape[0]
    wc_scr[...] = wc_ref[...].astype(BF16)
    wh_scr[...] = wh_ref[...].astype(BF16)
    wb_scr[...] = wb_ref[...].astype(BF16)

    @pl.when(i % tiles_per_seq == 0)
    def _():
        u_scr[0:SUBLANES, :] = jnp.zeros((SUBLANES, u_scr.shape[1]), F32)

    @pl.when(i % tiles_per_seq != 0)
    def _():
        u_scr[0:SUBLANES, :] = carry_scr[j]

    def xn_rows(r):
        return xn_ref[r * rb:(r + 1) * rb, :]

    def gate_matmuls(r):
        return (jnp.dot(xn_rows(r), wc_scr[...], preferred_element_type=F32),
                jnp.dot(xn_rows(r), wh_scr[...], preferred_element_type=F32))

    def store_u(r, acc):
        u_scr[SUBLANES + r * rb:SUBLANES + (r + 1) * rb, :] = acc[0] * acc[1]

    _row_block_pipeline(tm // rb, gate_matmuls, store_u)
    carry_scr[j] = u_scr[tm:tm + SUBLANES, :]
    cw = cw_ref[...]

    def b_matmul(r):
        return jnp.dot(xn_rows(r), wb_scr[...], preferred_element_type=F32)

    def conv_out(r, gate_b):
        def shifted(back):
            start = SUBLANES + r * rb - back
            return u_scr[start:start + rb, :]
        y = cw[2:3, :] * shifted(0) + cw[1:2, :] * shifted(1) + cw[0:1, :] * shifted(2)
        o_ref[r * rb:(r + 1) * rb, :] = (gate_b * y).astype(o_ref.dtype)

    _row_block_pipeline(tm // rb, b_matmul, conv_out)


def _conv_proj(xn, w_in, conv_w, col_b, col_c, col_h, n, seq, tm, tn, rb):
    t, d = xn.shape
    jb, jc, jh = col_b // tn, col_c // tn, col_h // tn
    nj = n // tn
    kern = functools.partial(_conv_kernel, tiles_per_seq=seq // tm, rb=rb)
    w_scratch = [pltpu.VMEM((d, tn), BF16)] * 3
    return pl.pallas_call(
        kern,
        grid=(t // tm, nj),
        in_specs=[pl.BlockSpec((tm, d), lambda i, j: (i, 0)),
                  pl.BlockSpec((d, tn), lambda i, j: (0, jb + j)),
                  pl.BlockSpec((d, tn), lambda i, j: (0, jc + j)),
                  pl.BlockSpec((d, tn), lambda i, j: (0, jh + j)),
                  pl.BlockSpec((CONV_K, tn), lambda i, j: (0, j))],
        out_specs=pl.BlockSpec((tm, tn), lambda i, j: (i, j)),
        out_shape=jax.ShapeDtypeStruct((t, n), BF16),
        scratch_shapes=w_scratch + [pltpu.VMEM((tm + SUBLANES, tn), F32),
                                    pltpu.VMEM((nj, SUBLANES, tn), F32)],
        compiler_params=_params(("arbitrary", "arbitrary")),
        name="conv_proj",
    )(xn, w_in, w_in, w_in, conv_w)


def _attn_kernel(q_ref, k_ref, v_ref, lam_ref, sw_ref, o_ref,
                 vt_scr, bias_scr, qq_a, qq_b, m_a, m_b, acc_a, acc_b, *s_bufs,
                 tq, tk, pw, ahead):
    seq = k_ref.shape[1]
    n_q = seq // tq
    n_diag = tq // tk
    n_buf = len(s_bufs)
    n_panels = 2 * tq // pw
    assert n_buf == n_diag and ahead < n_buf
    all_visible = ("full",) * n_panels
    state = ((qq_a, m_a, acc_a), (qq_b, m_b, acc_b))

    for c in range(seq // LANES):
        blk = v_ref[0, c * LANES:(c + 1) * LANES, :].astype(F32)
        vt_scr[0:V_DIM, c * LANES:(c + 1) * LANES] = blk.T.astype(BF16)
    vt_scr[V_DIM:, 0:seq] = jnp.ones((vt_scr.shape[0] - V_DIM, seq), BF16)
    key = lax.broadcasted_iota(jnp.int32, bias_scr.shape, 0)
    qry = lax.broadcasted_iota(jnp.int32, bias_scr.shape, 1)
    bias_scr[...] = jnp.where(key // CHUNK <= qry // CHUNK, 0.0, -1e30)

    lq1, lk1, lq2, lk2 = (lam_ref[r:r + 1, :] for r in range(4))
    lam = (jnp.exp(jnp.sum(lq1 * lk1, axis=-1, keepdims=True))
           - jnp.exp(jnp.sum(lq2 * lk2, axis=-1, keepdims=True)) + LAMBDA_INIT)

    def diag_modes(key_off):
        modes = []
        for c in range(n_panels):
            q_off = (c * pw) % tq
            if key_off + tk <= q_off:
                modes.append("full")
            elif key_off >= q_off + pw:
                modes.append("skip")
            else:
                assert key_off == q_off and tk == pw
                modes.append("tri")
        return tuple(modes)

    def prepare(qi):
        qq_scr, m_scr, acc_scr = state[qi % 2]
        dim = lax.broadcasted_iota(jnp.int32, (V_DIM, 1), 0)
        for c in range(tq // LANES):
            rows = slice(qi * tq + c * LANES, qi * tq + (c + 1) * LANES)
            qt = q_ref[0, rows, :].astype(F32).T
            cols = slice(c * LANES, (c + 1) * LANES)
            qq_scr[:, cols] = jnp.where(dim < HEAD_DIM, qt, 0.0).astype(BF16)
            cols = slice(tq + c * LANES, tq + (c + 1) * LANES)
            qq_scr[:, cols] = jnp.where(dim < HEAD_DIM, 0.0, qt).astype(BF16)
        m_scr[...] = jnp.full(m_scr.shape, -jnp.inf, F32)
        acc_scr[...] = jnp.zeros(acc_scr.shape, F32)

    def scores(qi, t, modes=all_visible):
        qq_scr = state[qi % 2][0]
        dst = s_bufs[t % n_buf]
        k = k_ref[0, t * tk:(t + 1) * tk, :]
        for c in range(n_panels):
            if modes[c] == "skip":
                continue
            cols = slice(c * pw, (c + 1) * pw)
            dst[:, cols] = jnp.dot(k, qq_scr[:, cols], preferred_element_type=F32)

    def update(qi, t, modes=all_visible):
        _, m_scr, acc_scr = state[qi % 2]
        src = s_bufs[t % n_buf]
        vt = vt_scr[:, t * tk:(t + 1) * tk]
        for c in range(n_panels):
            if modes[c] == "skip":
                continue
            cols = slice(c * pw, (c + 1) * pw)
            s = src[:, cols]
            if modes[c] == "tri":
                s = s + bias_scr[...]
            m_prev = m_scr[:, cols]
            m_new = jnp.maximum(m_prev, jnp.max(s, axis=0, keepdims=True))
            alpha = jnp.exp2(m_prev - m_new)
            p = jnp.exp2(s - m_new).astype(BF16)
            m_scr[:, cols] = m_new
            acc_scr[:, cols] = alpha * acc_scr[:, cols] + jnp.dot(
                vt, p, preferred_element_type=F32)

    def finish(qi):
        acc = state[qi % 2][2][:, 0:2 * tq]
        o = acc[0:V_DIM, :] / acc[V_DIM:V_DIM + 1, :]
        a = (o[:, 0:tq] - lam * o[:, tq:]).T
        ms = jnp.mean(a * a, axis=-1, keepdims=True)
        o_ref[0, qi * tq:(qi + 1) * tq, :] = (
            ((a * lax.rsqrt(ms + SUBLN_EPS)) * sw_ref[...])
            * (1.0 - LAMBDA_INIT)).astype(o_ref.dtype)

    prepare(0)
    for u in range(ahead):
        scores(0, u, diag_modes(u * tk))
    for qi in range(n_q):
        n_full = n_diag * qi

        def tile_modes(t):
            return all_visible if t < n_full else diag_modes((t - n_full) * tk)

        for g in range(qi + 1):
            t0 = n_buf * g
            for u in range(ahead, n_buf):
                scores(qi, t0 + u, tile_modes(t0 + u))
            if g == qi and qi + 1 < n_q:
                prepare(qi + 1)
            for u in range(n_buf):
                update(qi, t0 + u, tile_modes(t0 + u))
                if u < ahead:
                    if g < qi:
                        scores(qi, t0 + n_buf + u, tile_modes(t0 + n_buf + u))
                    elif qi + 1 < n_q:
                        scores(qi + 1, u)
        finish(qi)


def _attention(qkv, lam_vecs, subln_w, tq, tk, pw, ahead):
    b, s, w3 = qkv.shape
    w = w3 // 3
    h = w // V_DIM
    kern = functools.partial(_attn_kernel, tq=tq, tk=tk, pw=pw, ahead=ahead)

    def head_block(first):
        return pl.BlockSpec((1, s, V_DIM), lambda bi, hi: (bi, 0, first + hi))

    per_q_tile = ([pltpu.VMEM((V_DIM, 2 * tq), BF16)] * 2
                  + [pltpu.VMEM((1, 2 * tq), F32)] * 2
                  + [pltpu.VMEM((V_DIM + BF16_SUBLANES, 2 * tq + LANES), F32)] * 2)
    return pl.pallas_call(
        kern,
        grid=(b, h),
        in_specs=[head_block(0), head_block(h), head_block(2 * h),
                  pl.BlockSpec((4, HEAD_DIM), lambda bi, hi: (0, 0)),
                  pl.BlockSpec((1, V_DIM), lambda bi, hi: (0, 0))],
        out_specs=head_block(0),
        out_shape=jax.ShapeDtypeStruct((b, s, w), BF16),
        scratch_shapes=[pltpu.VMEM((V_DIM + BF16_SUBLANES, s + LANES), BF16),
                        pltpu.VMEM((tk, pw), F32)]
        + per_q_tile
        + [pltpu.VMEM((tk, 2 * tq + LANES), F32)] * (tq // tk),
        compiler_params=_params(("parallel", "parallel")),
        name="diff_attention",
    )(qkv, qkv, qkv, lam_vecs, subln_w)


def _out_kernel(a_ref, c_ref, wa_ref, wc_ref, x_ref, fnw_ref, o_ref, fn_ref, h_scr,
                *, rb):
    j = pl.program_id(1)
    n_j, tm, tn = h_scr.shape
    h = x_ref[...] + _mm(a_ref[...], wa_ref[...]) + _mm(c_ref[...], wc_ref[...])
    o_ref[...] = h
    h_scr[j] = h

    @pl.when(j == n_j - 1)
    def _():
        for r in range(tm // rb):
            rows = slice(r * rb, (r + 1) * rb)
            ssq = sum(jnp.sum(h_scr[jj, rows, :] * h_scr[jj, rows, :], axis=-1, keepdims=True)
                      for jj in range(n_j))
            scale = lax.rsqrt(ssq * (1.0 / (n_j * tn)) + EPS)
            for jj in range(n_j):
                cols = slice(jj * tn, (jj + 1) * tn)
                fn_ref[rows, cols] = ((h_scr[jj, rows, :] * scale)
                                      * fnw_ref[:, cols]).astype(fn_ref.dtype)


def _out_proj(attn, conv, w_out, x, ffn_norm_w, tm, tn, rb):
    t, ka = attn.shape
    kc = conv.shape[1]
    n = w_out.shape[1]
    assert ka == kc
    return pl.pallas_call(
        functools.partial(_out_kernel, rb=rb),
        grid=(t // tm, n // tn),
        in_specs=[pl.BlockSpec((tm, ka), lambda i, j: (i, 0)),
                  pl.BlockSpec((tm, kc), lambda i, j: (i, 0)),
                  pl.BlockSpec((ka, tn), lambda i, j: (0, j)),
                  pl.BlockSpec((kc, tn), lambda i, j: (1, j)),
                  pl.BlockSpec((tm, tn), lambda i, j: (i, j)),
                  pl.BlockSpec((1, n), lambda i, j: (0, 0))],
        out_specs=[pl.BlockSpec((tm, tn), lambda i, j: (i, j)),
                   pl.BlockSpec((tm, n), lambda i, j: (i, 0))],
        out_shape=[jax.ShapeDtypeStruct((t, n), F32),
                   jax.ShapeDtypeStruct((t, n), BF16)],
        scratch_shapes=[pltpu.VMEM((n // tn, tm, tn), F32)],
        compiler_params=_params(("arbitrary", "arbitrary")),
        name="out_proj",
    )(attn, conv, w_out, w_out, x, ffn_norm_w.reshape(1, n))


def _gate_up_kernel(x_ref, wg_ref, wu_ref, o_ref):
    x = x_ref[...]
    g = _mm(x, wg_ref[...])
    u = _mm(x, wu_ref[...])
    o_ref[...] = ((g * jax.nn.sigmoid(g)) * u).astype(o_ref.dtype)


def _gate_up(fn, w_gate, w_up, tm, tn):
    t, d = fn.shape
    f = w_gate.shape[1]
    return pl.pallas_call(
        _gate_up_kernel,
        grid=(t // tm, f // tn),
        in_specs=[pl.BlockSpec((tm, d), lambda i, j: (i, 0)),
                  pl.BlockSpec((d, tn), lambda i, j: (0, j)),
                  pl.BlockSpec((d, tn), lambda i, j: (0, j))],
        out_specs=pl.BlockSpec((tm, tn), lambda i, j: (i, j)),
        out_shape=jax.ShapeDtypeStruct((t, f), BF16),
        compiler_params=_params(("parallel", "parallel")),
        name="gate_up",
    )(fn, w_gate, w_up)


def _down_kernel(h_ref, w_ref, r_ref, o_ref):
    o_ref[...] = r_ref[...] + _mm(h_ref[...], w_ref[...])


def _down(h, w_down, resid, tm, tn):
    t, f = h.shape
    n = w_down.shape[1]
    return pl.pallas_call(
        _down_kernel,
        grid=(t // tm, n // tn),
        in_specs=[pl.BlockSpec((tm, f), lambda i, j: (i, 0)),
                  pl.BlockSpec((f, tn), lambda i, j: (0, j)),
                  pl.BlockSpec((tm, tn), lambda i, j: (i, j))],
        out_specs=pl.BlockSpec((tm, tn), lambda i, j: (i, j)),
        out_shape=jax.ShapeDtypeStruct((t, n), F32),
        compiler_params=_params(("parallel", "parallel")),
        name="down_proj",
    )(h, w_down, resid)


TILES = dict(
    proj_tm=1024, proj_tn=512, proj_rb=256,
    conv_tm=1024, conv_tn=512, conv_rb=256,
    attn_tq=1024, attn_tk=256, attn_pw=256, attn_ahead=2,
    out_tm=1024, out_tn=512, out_rb=256,
    gu_tm=1024, gu_tn=512,
    down_tm=1024, down_tn=256,
)


def _rope_tables(seq):
    pos = jnp.arange(seq, dtype=F32)
    inv_freq = ROPE_THETA ** (-jnp.arange(0, HEAD_DIM, 2, dtype=F32) / HEAD_DIM)
    ang = pos[:, None] * inv_freq[None, :]
    cos, sin = jnp.cos(ang), jnp.sin(ang)
    reps = LANES // (HEAD_DIM // 2)
    cos_l = jnp.tile(cos, (1, reps))
    sin_l = jnp.tile(jnp.concatenate([-sin, sin], axis=1), (1, reps // 2))
    return cos_l, sin_l


def kernel(x, attn_norm_w, w_in, q_norm_w, k_norm_w, lambda_q1, lambda_k1,
           lambda_q2, lambda_k2, subln_w, conv_w, w_out, ffn_norm_w, w_gate,
           w_up, w_down):
    b, s, d = x.shape
    t = b * s
    attn_w = N_HEADS * V_DIM
    qk_cols = N_HEADS * 2 * HEAD_DIM
    conv_width = d - attn_w
    col_v = 2 * qk_cols
    col_b = col_v + attn_w
    col_c = col_b + conv_width
    col_h = col_c + conv_width
    tl = TILES
    assert w_in.shape[0] == 1, "single-layer block"

    x2 = x.reshape(t, d)
    cos_l, sin_l = _rope_tables(s)
    scale = math.log2(math.e) / math.sqrt(HEAD_DIM)
    nw = jnp.concatenate([jnp.tile(q_norm_w[0] * scale, qk_cols // HEAD_DIM),
                          jnp.tile(k_norm_w[0], qk_cols // HEAD_DIM)]).reshape(1, -1)
    lam_vecs = jnp.concatenate([lambda_q1, lambda_k1, lambda_q2, lambda_k2], axis=0)

    qkv, xn = _qkv_proj(x2, attn_norm_w[0], w_in[0], nw, cos_l, sin_l, s, col_b,
                        tl["proj_tm"], tl["proj_tn"], tl["proj_rb"])
    conv = _conv_proj(xn, w_in[0], conv_w[0], col_b, col_c, col_h, conv_width, s,
                      tl["conv_tm"], tl["conv_tn"], tl["conv_rb"])

    attn = _attention(qkv.reshape(b, s, col_b), lam_vecs, subln_w[0].reshape(1, V_DIM),
                      tl["attn_tq"], tl["attn_tk"], tl["attn_pw"], tl["attn_ahead"])

    h1, fn = _out_proj(attn.reshape(t, attn_w), conv, w_out[0], x2, ffn_norm_w[0],
                       tl["out_tm"], tl["out_tn"], tl["out_rb"])
    hid = _gate_up(fn, w_gate[0], w_up[0], tl["gu_tm"], tl["gu_tn"])
    out = _down(hid, w_down[0], h1, tl["down_tm"], tl["down_tn"])
    return out.reshape(b, s, d)
```

```python
import functools
import math

import jax
import jax.numpy as jnp
from jax import lax
from jax.experimental import pallas as pl
from jax.experimental.pallas import tpu as pltpu

F32 = jnp.float32
BF16 = jnp.bfloat16

CHUNK = 64
HEAD_DIM = 64
V_DIM = 2 * HEAD_DIM
N_HEADS = 8
CONV_K = 3
ROPE_THETA = 10000.0
EPS = 1e-6
SUBLN_EPS = 1e-5
LAMBDA_INIT = 0.8 - 0.6 * math.exp(-0.3 * 0)

LANES = 128
SUBLANES = 8
BF16_SUBLANES = 16
VMEM_LIMIT = 56 * 1024 * 1024


def _params(semantics, flags=None):
    return pltpu.CompilerParams(dimension_semantics=semantics,
                                vmem_limit_bytes=VMEM_LIMIT, flags=flags)


def _mm(a_bf16, w_f32):
    return jnp.dot(a_bf16, w_f32.astype(BF16), preferred_element_type=F32)


def _rms_scale(x, w, eps):
    ms = jnp.mean(x * x, axis=-1, keepdims=True)
    return (x * lax.rsqrt(ms + eps)) * w


def _row_block_pipeline(n_blocks, matmul, epilogue):
    acc = matmul(0)
    for r in range(n_blocks):
        nxt = matmul(r + 1) if r + 1 < n_blocks else None
        epilogue(r, acc)
        acc = nxt


def _qkv_kernel(x_ref, anw_ref, w_ref, nw_ref, cos_ref, sin_ref, o_ref, xn_ref,
                w_scr, *, rb, n_qk_tiles):
    j = pl.program_id(1)
    tm, tn = o_ref.shape

    @pl.when(j == 0)
    def _():
        for r in range(tm // rb):
            rows = slice(r * rb, (r + 1) * rb)
            xn_ref[rows, :] = _rms_scale(x_ref[rows, :], anw_ref[...], EPS).astype(BF16)

    w_scr[...] = w_ref[...].astype(BF16)
    lane = lax.broadcasted_iota(jnp.int32, (1, LANES), 1)
    lo_half = lane < HEAD_DIM
    first = (lane % HEAD_DIM) < (HEAD_DIM // 2)

    def matmul(r):
        return jnp.dot(xn_ref[r * rb:(r + 1) * rb, :], w_scr[...],
                       preferred_element_type=F32)

    def plain(r, acc):
        o_ref[r * rb:(r + 1) * rb, :] = acc.astype(o_ref.dtype)

    def norm_rope(r, acc):
        rows = slice(r * rb, (r + 1) * rb)
        cos = cos_ref[rows, :]
        sin = sin_ref[rows, :]
        for c in range(tn // LANES):
            cols = slice(c * LANES, (c + 1) * LANES)
            y = acc[:, cols]
            sq = y * y
            s_lo = jnp.sum(jnp.where(lo_half, sq, 0.0), axis=-1, keepdims=True)
            s_hi = jnp.sum(jnp.where(lo_half, 0.0, sq), axis=-1, keepdims=True)
            ms = jnp.where(lo_half, s_lo, s_hi) * (1.0 / HEAD_DIM)
            yn = (y * lax.rsqrt(ms + EPS)) * nw_ref[:, cols]
            rot = jnp.where(first, pltpu.roll(yn, LANES - HEAD_DIM // 2, 1),
                            pltpu.roll(yn, HEAD_DIM // 2, 1))
            o_ref[rows, cols] = (yn * cos + rot * sin).astype(o_ref.dtype)

    @pl.when(j < n_qk_tiles)
    def _():
        _row_block_pipeline(tm // rb, matmul, norm_rope)

    @pl.when(j >= n_qk_tiles)
    def _():
        _row_block_pipeline(tm // rb, matmul, plain)


def _qkv_proj(x, attn_norm_w, w_in, nw, cos, sin, seq, n_out, tm, tn, rb):
    t, d = x.shape
    n_qk_tiles = nw.shape[1] // tn
    tiles_per_seq = seq // tm
    kern = functools.partial(_qkv_kernel, rb=rb, n_qk_tiles=n_qk_tiles)
    return pl.pallas_call(
        kern,
        grid=(t // tm, n_out // tn),
        in_specs=[pl.BlockSpec((tm, d), lambda i, j: (i, 0)),
                  pl.BlockSpec((1, d), lambda i, j: (0, 0)),
                  pl.BlockSpec((d, tn), lambda i, j: (0, j)),
                  pl.BlockSpec((1, tn), lambda i, j: (0, jnp.minimum(j, n_qk_tiles - 1))),
                  pl.BlockSpec((tm, LANES), lambda i, j: (i % tiles_per_seq, 0)),
                  pl.BlockSpec((tm, LANES), lambda i, j: (i % tiles_per_seq, 0))],
        out_specs=[pl.BlockSpec((tm, tn), lambda i, j: (i, j)),
                   pl.BlockSpec((tm, d), lambda i, j: (i, 0))],
        out_shape=[jax.ShapeDtypeStruct((t, n_out), BF16),
                   jax.ShapeDtypeStruct((t, d), BF16)],
        scratch_shapes=[pltpu.VMEM((d, tn), BF16)],
        compiler_params=_params(("arbitrary", "arbitrary")),
        name="qkv_proj",
    )(x, attn_norm_w.reshape(1, d), w_in, nw, cos, sin)


def _conv_kernel(xn_ref, wb_ref, wc_ref, wh_ref, cw_ref, o_ref,
                 wb_scr, wc_scr, wh_scr, u_scr, carry_scr, *, tiles_per_seq, rb):
    i = pl.program_id(0)
    j = pl.program_id(1)
    tm = xn_ref.shape[0]
    wc_scr[...] = wc_ref[...].astype(BF16)
    wh_scr[...] = wh_ref[...].astype(BF16)
    wb_scr[...] = wb_ref[...].astype(BF16)

    @pl.when(i % tiles_per_seq == 0)
    def _():
        u_scr[0:SUBLANES, :] = jnp.zeros((SUBLANES, u_scr.shape[1]), F32)

    @pl.when(i % tiles_per_seq != 0)
    def _():
        u_scr[0:SUBLANES, :] = carry_scr[j]

    def xn_rows(r):
        return xn_ref[r * rb:(r + 1) * rb, :]

    def gate_matmuls(r):
        return (jnp.dot(xn_rows(r), wc_scr[...], preferred_element_type=F32),
                jnp.dot(xn_rows(r), wh_scr[...], preferred_element_type=F32))

    def store_u(r, acc):
        u_scr[SUBLANES + r * rb:SUBLANES + (r + 1) * rb, :] = acc[0] * acc[1]

    _row_block_pipeline(tm // rb, gate_matmuls, store_u)
    carry_scr[j] = u_scr[tm:tm + SUBLANES, :]
    cw = cw_ref[...]

    def b_matmul(r):
        return jnp.dot(xn_rows(r), wb_scr[...], preferred_element_type=F32)

    def conv_out(r, gate_b):
        def shifted(back):
            start = SUBLANES + r * rb - back
            return u_scr[start:start + rb, :]
        y = cw[2:3, :] * shifted(0) + cw[1:2, :] * shifted(1) + cw[0:1, :] * shifted(2)
        o_ref[r * rb:(r + 1) * rb, :] = (gate_b * y).astype(o_ref.dtype)

    _row_block_pipeline(tm // rb, b_matmul, conv_out)


def _conv_proj(xn, w_in, conv_w, col_b, col_c, col_h, n, seq, tm, tn, rb):
    t, d = xn.shape
    jb, jc, jh = col_b // tn, col_c // tn, col_h // tn
    nj = n // tn
    kern = functools.partial(_conv_kernel, tiles_per_seq=seq // tm, rb=rb)
    w_scratch = [pltpu.VMEM((d, tn), BF16)] * 3
    return pl.pallas_call(
        kern,
        grid=(t // tm, nj),
        in_specs=[pl.BlockSpec((tm, d), lambda i, j: (i, 0)),
                  pl.BlockSpec((d, tn), lambda i, j: (0, jb + j)),
                  pl.BlockSpec((d, tn), lambda i, j: (0, jc + j)),
                  pl.BlockSpec((d, tn), lambda i, j: (0, jh + j)),
                  pl.BlockSpec((CONV_K, tn), lambda i, j: (0, j))],
        out_specs=pl.BlockSpec((tm, tn), lambda i, j: (i, j)),
        out_shape=jax.ShapeDtypeStruct((t, n), BF16),
        scratch_shapes=w_scratch + [pltpu.VMEM((tm + SUBLANES, tn), F32),
                                    pltpu.VMEM((nj, SUBLANES, tn), F32)],
        compiler_params=_params(("arbitrary", "arbitrary")),
        name="conv_proj",
    )(xn, w_in, w_in, w_in, conv_w)


def _attn_kernel(q_ref, k_ref, v_ref, lam_ref, sw_ref, o_ref,
                 vt_scr, bias_scr, qq_a, qq_b, m_a, m_b, acc_a, acc_b, *s_bufs,
                 tq, tk, pw, ahead):
    seq = k_ref.shape[1]
    n_q = seq // tq
    n_diag = tq // tk
    n_buf = len(s_bufs)
    n_panels = 2 * tq // pw
    assert n_buf == n_diag and ahead < n_buf
    all_visible = ("full",) * n_panels
    state = ((qq_a, m_a, acc_a), (qq_b, m_b, acc_b))

    for c in range(seq // LANES):
        blk = v_ref[0, c * LANES:(c + 1) * LANES, :].astype(F32)
        vt_scr[0:V_DIM, c * LANES:(c + 1) * LANES] = blk.T.astype(BF16)
    vt_scr[V_DIM:, 0:seq] = jnp.ones((vt_scr.shape[0] - V_DIM, seq), BF16)
    key = lax.broadcasted_iota(jnp.int32, bias_scr.shape, 0)
    qry = lax.broadcasted_iota(jnp.int32, bias_scr.shape, 1)
    bias_scr[...] = jnp.where(key // CHUNK <= qry // CHUNK, 0.0, -1e30)

    lq1, lk1, lq2, lk2 = (lam_ref[r:r + 1, :] for r in range(4))
    lam = (jnp.exp(jnp.sum(lq1 * lk1, axis=-1, keepdims=True))
           - jnp.exp(jnp.sum(lq2 * lk2, axis=-1, keepdims=True)) + LAMBDA_INIT)

    def diag_modes(key_off):
        modes = []
        for c in range(n_panels):
            q_off = (c * pw) % tq
            if key_off + tk <= q_off:
                modes.append("full")
            elif key_off >= q_off + pw:
                modes.append("skip")
            else:
                assert key_off == q_off and tk == pw
                modes.append("tri")
        return tuple(modes)

    def prepare(qi):
        qq_scr, m_scr, acc_scr = state[qi % 2]
        dim = lax.broadcasted_iota(jnp.int32, (V_DIM, 1), 0)
        for c in range(tq // LANES):
            rows = slice(qi * tq + c * LANES, qi * tq + (c + 1) * LANES)
            qt = q_ref[0, rows, :].astype(F32).T
            cols = slice(c * LANES, (c + 1) * LANES)
            qq_scr[:, cols] = jnp.where(dim < HEAD_DIM, qt, 0.0).astype(BF16)
            cols = slice(tq + c * LANES, tq + (c + 1) * LANES)
            qq_scr[:, cols] = jnp.where(dim < HEAD_DIM, 0.0, qt).astype(BF16)
        m_scr[...] = jnp.full(m_scr.shape, -jnp.inf, F32)
        acc_scr[...] = jnp.zeros(acc_scr.shape, F32)

    def scores(qi, t, modes=all_visible):
        qq_scr = state[qi % 2][0]
        dst = s_bufs[t % n_buf]
        k = k_ref[0, t * tk:(t + 1) * tk, :]
        for c in range(n_panels):
            if modes[c] == "skip":
                continue
            cols = slice(c * pw, (c + 1) * pw)
            dst[:, cols] = jnp.dot(k, qq_scr[:, cols], preferred_element_type=F32)

    def update(qi, t, modes=all_visible):
        _, m_scr, acc_scr = state[qi % 2]
        src = s_bufs[t % n_buf]
        vt = vt_scr[:, t * tk:(t + 1) * tk]
        for c in range(n_panels):
            if modes[c] == "skip":
                continue
            cols = slice(c * pw, (c + 1) * pw)
            s = src[:, cols]
            if modes[c] == "tri":
                s = s + bias_scr[...]
            m_prev = m_scr[:, cols]
            m_new = jnp.maximum(m_prev, jnp.max(s, axis=0, keepdims=True))
            alpha = jnp.exp2(m_prev - m_new)
            p = jnp.exp2(s - m_new).astype(BF16)
            m_scr[:, cols] = m_new
            acc_scr[:, cols] = alpha * acc_scr[:, cols] + jnp.dot(
                vt, p, preferred_element_type=F32)

    def finish(qi):
        acc = state[qi % 2][2][:, 0:2 * tq]
        o = acc[0:V_DIM, :] / acc[V_DIM:V_DIM + 1, :]
        a = (o[:, 0:tq] - lam * o[:, tq:]).T
        ms = jnp.mean(a * a, axis=-1, keepdims=True)
        o_ref[0, qi * tq:(qi + 1) * tq, :] = (
            ((a * lax.rsqrt(ms + SUBLN_EPS)) * sw_ref[...])
            * (1.0 - LAMBDA_INIT)).astype(o_ref.dtype)

    prepare(0)
    for u in range(ahead):
        scores(0, u, diag_modes(u * tk))
    for qi in range(n_q):
        n_full = n_diag * qi

        def tile_modes(t):
            return all_visible if t < n_full else diag_modes((t - n_full) * tk)

        for g in range(qi + 1):
            t0 = n_buf * g
            for u in range(ahead, n_buf):
                scores(qi, t0 + u, tile_modes(t0 + u))
            if g == qi and qi + 1 < n_q:
                prepare(qi + 1)
            for u in range(n_buf):
                update(qi, t0 + u, tile_modes(t0 + u))
                if u < ahead:
                    if g < qi:
                        scores(qi, t0 + n_buf + u, tile_modes(t0 + n_buf + u))
                    elif qi + 1 < n_q:
                        scores(qi + 1, u)
        finish(qi)


def _attention(qkv, lam_vecs, subln_w, tq, tk, pw, ahead):
    b, s, w3 = qkv.shape
    w = w3 // 3
    h = w // V_DIM
    kern = functools.partial(_attn_kernel, tq=tq, tk=tk, pw=pw, ahead=ahead)

    def head_block(first):
        return pl.BlockSpec((1, s, V_DIM), lambda bi, hi: (bi, 0, first + hi))

    per_q_tile = ([pltpu.VMEM((V_DIM, 2 * tq), BF16)] * 2
                  + [pltpu.VMEM((1, 2 * tq), F32)] * 2
                  + [pltpu.VMEM((V_DIM + BF16_SUBLANES, 2 * tq + LANES), F32)] * 2)
    return pl.pallas_call(
        kern,
        grid=(b, h),
        in_specs=[head_block(0), head_block(h), head_block(2 * h),
                  pl.BlockSpec((4, HEAD_DIM), lambda bi, hi: (0, 0)),
                  pl.BlockSpec((1, V_DIM), lambda bi, hi: (0, 0))],
        out_specs=head_block(0),
        out_shape=jax.ShapeDtypeStruct((b, s, w), BF16),
        scratch_shapes=[pltpu.VMEM((V_DIM + BF16_SUBLANES, s + LANES), BF16),
                        pltpu.VMEM((tk, pw), F32)]
        + per_q_tile
        + [pltpu.VMEM((tk, 2 * tq + LANES), F32)] * (tq // tk),
        compiler_params=_params(("parallel", "parallel")),
        name="diff_attention",
    )(qkv, qkv, qkv, lam_vecs, subln_w)


def _out_kernel(a_ref, c_ref, wa_ref, wc_ref, x_ref, fnw_ref, o_ref, fn_ref, h_scr,
                *, rb):
    j = pl.program_id(1)
    n_j, tm, tn = h_scr.shape
    h = x_ref[...] + _mm(a_ref[...], wa_ref[...]) + _mm(c_ref[...], wc_ref[...])
    o_ref[...] = h
    h_scr[j] = h

    @pl.when(j == n_j - 1)
    def _():
        for r in range(tm // rb):
            rows = slice(r * rb, (r + 1) * rb)
            ssq = sum(jnp.sum(h_scr[jj, rows, :] * h_scr[jj, rows, :], axis=-1, keepdims=True)
                      for jj in range(n_j))
            scale = lax.rsqrt(ssq * (1.0 / (n_j * tn)) + EPS)
            for jj in range(n_j):
                cols = slice(jj * tn, (jj + 1) * tn)
                fn_ref[rows, cols] = ((h_scr[jj, rows, :] * scale)
                                      * fnw_ref[:, cols]).astype(fn_ref.dtype)


def _out_proj(attn, conv, w_out, x, ffn_norm_w, tm, tn, rb):
    t, ka = attn.shape
    kc = conv.shape[1]
    n = w_out.shape[1]
    assert ka == kc
    return pl.pallas_call(
        functools.partial(_out_kernel, rb=rb),
        grid=(t // tm, n // tn),
        in_specs=[pl.BlockSpec((tm, ka), lambda i, j: (i, 0)),
                  pl.BlockSpec((tm, kc), lambda i, j: (i, 0)),
                  pl.BlockSpec((ka, tn), lambda i, j: (0, j)),
                  pl.BlockSpec((kc, tn), lambda i, j: (1, j)),
                  pl.BlockSpec((tm, tn), lambda i, j: (i, j)),
                  pl.BlockSpec((1, n), lambda i, j: (0, 0))],
        out_specs=[pl.BlockSpec((tm, tn), lambda i, j: (i, j)),
                   pl.BlockSpec((tm, n), lambda i, j: (i, 0))],
        out_shape=[jax.ShapeDtypeStruct((t, n), F32),
                   jax.ShapeDtypeStruct((t, n), BF16)],
        scratch_shapes=[pltpu.VMEM((n // tn, tm, tn), F32)],
        compiler_params=_params(("arbitrary", "arbitrary")),
        name="out_proj",
    )(attn, conv, w_out, w_out, x, ffn_norm_w.reshape(1, n))


def _gate_up_kernel(x_ref, wg_ref, wu_ref, o_ref):
    x = x_ref[...]
    g = _mm(x, wg_ref[...])
    u = _mm(x, wu_ref[...])
    o_ref[...] = ((g * jax.nn.sigmoid(g)) * u).astype(o_ref.dtype)


def _gate_up(fn, w_gate, w_up, tm, tn):
    t, d = fn.shape
    f = w_gate.shape[1]
    return pl.pallas_call(
        _gate_up_kernel,
        grid=(t // tm, f // tn),
        in_specs=[pl.BlockSpec((tm, d), lambda i, j: (i, 0)),
                  pl.BlockSpec((d, tn), lambda i, j: (0, j)),
                  pl.BlockSpec((d, tn), lambda i, j: (0, j))],
        out_specs=pl.BlockSpec((tm, tn), lambda i, j: (i, j)),
        out_shape=jax.ShapeDtypeStruct((t, f), BF16),
        compiler_params=_params(("parallel", "parallel")),
        name="gate_up",
    )(fn, w_gate, w_up)


def _down_kernel(h_ref, w_ref, r_ref, o_ref):
    o_ref[...] = r_ref[...] + _mm(h_ref[...], w_ref[...])


def _down(h, w_down, resid, tm, tn):
    t, f = h.shape
    n = w_down.shape[1]
    return pl.pallas_call(
        _down_kernel,
        grid=(t // tm, n // tn),
        in_specs=[pl.BlockSpec((tm, f), lambda i, j: (i, 0)),
                  pl.BlockSpec((f, tn), lambda i, j: (0, j)),
                  pl.BlockSpec((tm, tn), lambda i, j: (i, j))],
        out_specs=pl.BlockSpec((tm, tn), lambda i, j: (i, j)),
        out_shape=jax.ShapeDtypeStruct((t, n), F32),
        compiler_params=_params(("parallel", "parallel")),
        name="down_proj",
    )(h, w_down, resid)


TILES = dict(
    proj_tm=1024, proj_tn=1024, proj_rb=256,
    conv_tm=1024, conv_tn=512, conv_rb=256,
    attn_tq=1024, attn_tk=256, attn_pw=256, attn_ahead=2,
    out_tm=1024, out_tn=512, out_rb=256,
    gu_tm=1024, gu_tn=512,
    down_tm=1024, down_tn=256,
)


def _rope_tables(seq):
    pos = jnp.arange(seq, dtype=F32)
    inv_freq = ROPE_THETA ** (-jnp.arange(0, HEAD_DIM, 2, dtype=F32) / HEAD_DIM)
    ang = pos[:, None] * inv_freq[None, :]
    cos, sin = jnp.cos(ang), jnp.sin(ang)
    reps = LANES // (HEAD_DIM // 2)
    cos_l = jnp.tile(cos, (1, reps))
    sin_l = jnp.tile(jnp.concatenate([-sin, sin], axis=1), (1, reps // 2))
    return cos_l, sin_l


def kernel(x, attn_norm_w, w_in, q_norm_w, k_norm_w, lambda_q1, lambda_k1,
           lambda_q2, lambda_k2, subln_w, conv_w, w_out, ffn_norm_w, w_gate,
           w_up, w_down):
    b, s, d = x.shape
    t = b * s
    attn_w = N_HEADS * V_DIM
    qk_cols = N_HEADS * 2 * HEAD_DIM
    conv_width = d - attn_w
    col_v = 2 * qk_cols
    col_b = col_v + attn_w
    col_c = col_b + conv_width
    col_h = col_c + conv_width
    tl = TILES
    assert w_in.shape[0] == 1, "single-layer block"

    x2 = x.reshape(t, d)
    cos_l, sin_l = _rope_tables(s)
    scale = math.log2(math.e) / math.sqrt(HEAD_DIM)
    nw = jnp.concatenate([jnp.tile(q_norm_w[0] * scale, qk_cols // HEAD_DIM),
                          jnp.tile(k_norm_w[0], qk_cols // HEAD_DIM)]).reshape(1, -1)
    lam_vecs = jnp.concatenate([lambda_q1, lambda_k1, lambda_q2, lambda_k2], axis=0)

    qkv, xn = _qkv_proj(x2, attn_norm_w[0], w_in[0], nw, cos_l, sin_l, s, col_b,
                        tl["proj_tm"], tl["proj_tn"], tl["proj_rb"])
    conv = _conv_proj(xn, w_in[0], conv_w[0], col_b, col_c, col_h, conv_width, s,
                      tl["conv_tm"], tl["conv_tn"], tl["conv_rb"])

    attn = _attention(qkv.reshape(b, s, col_b), lam_vecs, subln_w[0].reshape(1, V_DIM),
                      tl["attn_tq"], tl["attn_tk"], tl["attn_pw"], tl["attn_ahead"])

    h1, fn = _out_proj(attn.reshape(t, attn_w), conv, w_out[0], x2, ffn_norm_w[0],
                       tl["out_tm"], tl["out_tn"], tl["out_rb"])
    hid = _gate_up(fn, w_gate[0], w_up[0], tl["gu_tm"], tl["gu_tn"])
    out = _down(hid, w_down[0], h1, tl["down_tm"], tl["down_tn"])
    return out.reshape(b, s, d)
```

```python
import functools
import math

import jax
import jax.numpy as jnp
from jax import lax
from jax.experimental import pallas as pl
from jax.experimental.pallas import tpu as pltpu

F32 = jnp.float32
BF16 = jnp.bfloat16

CHUNK = 64
HEAD_DIM = 64
V_DIM = 2 * HEAD_DIM
N_HEADS = 8
CONV_K = 3
ROPE_THETA = 10000.0
EPS = 1e-6
SUBLN_EPS = 1e-5
LAMBDA_INIT = 0.8 - 0.6 * math.exp(-0.3 * 0)

LANES = 128
SUBLANES = 8
BF16_SUBLANES = 16
VMEM_LIMIT = 56 * 1024 * 1024


def _params(semantics, flags=None):
    return pltpu.CompilerParams(dimension_semantics=semantics,
                                vmem_limit_bytes=VMEM_LIMIT, flags=flags)


def _mm(a_bf16, w_f32):
    return jnp.dot(a_bf16, w_f32.astype(BF16), preferred_element_type=F32)


def _rms_scale(x, w, eps):
    ms = jnp.mean(x * x, axis=-1, keepdims=True)
    return (x * lax.rsqrt(ms + eps)) * w


def _row_block_pipeline(n_blocks, matmul, epilogue):
    acc = matmul(0)
    for r in range(n_blocks):
        nxt = matmul(r + 1) if r + 1 < n_blocks else None
        epilogue(r, acc)
        acc = nxt


def _qkv_kernel(x_ref, anw_ref, w_ref, nw_ref, cos_ref, sin_ref, o_ref, xn_ref,
                w_scr, *, rb, n_qk_tiles):
    j = pl.program_id(1)
    tm, tn = o_ref.shape

    @pl.when(j == 0)
    def _():
        for r in range(tm // rb):
            rows = slice(r * rb, (r + 1) * rb)
            xn_ref[rows, :] = _rms_scale(x_ref[rows, :], anw_ref[...], EPS).astype(BF16)

    w_scr[...] = w_ref[...].astype(BF16)
    lane = lax.broadcasted_iota(jnp.int32, (1, LANES), 1)
    lo_half = lane < HEAD_DIM
    first = (lane % HEAD_DIM) < (HEAD_DIM // 2)

    def matmul(r):
        return jnp.dot(xn_ref[r * rb:(r + 1) * rb, :], w_scr[...],
                       preferred_element_type=F32)

    def plain(r, acc):
        o_ref[r * rb:(r + 1) * rb, :] = acc.astype(o_ref.dtype)

    def norm_rope(r, acc):
        rows = slice(r * rb, (r + 1) * rb)
        cos = cos_ref[rows, :]
        sin = sin_ref[rows, :]
        for c in range(tn // LANES):
            cols = slice(c * LANES, (c + 1) * LANES)
            y = acc[:, cols]
            sq = y * y
            s_lo = jnp.sum(jnp.where(lo_half, sq, 0.0), axis=-1, keepdims=True)
            s_hi = jnp.sum(jnp.where(lo_half, 0.0, sq), axis=-1, keepdims=True)
            ms = jnp.where(lo_half, s_lo, s_hi) * (1.0 / HEAD_DIM)
            yn = (y * lax.rsqrt(ms + EPS)) * nw_ref[:, cols]
            rot = jnp.where(first, pltpu.roll(yn, LANES - HEAD_DIM // 2, 1),
                            pltpu.roll(yn, HEAD_DIM // 2, 1))
            o_ref[rows, cols] = (yn * cos + rot * sin).astype(o_ref.dtype)

    @pl.when(j < n_qk_tiles)
    def _():
        _row_block_pipeline(tm // rb, matmul, norm_rope)

    @pl.when(j >= n_qk_tiles)
    def _():
        _row_block_pipeline(tm // rb, matmul, plain)


def _qkv_proj(x, attn_norm_w, w_in, nw, cos, sin, seq, n_out, tm, tn, rb):
    t, d = x.shape
    n_qk_tiles = nw.shape[1] // tn
    tiles_per_seq = seq // tm
    kern = functools.partial(_qkv_kernel, rb=rb, n_qk_tiles=n_qk_tiles)
    return pl.pallas_call(
        kern,
        grid=(t // tm, n_out // tn),
        in_specs=[pl.BlockSpec((tm, d), lambda i, j: (i, 0)),
                  pl.BlockSpec((1, d), lambda i, j: (0, 0)),
                  pl.BlockSpec((d, tn), lambda i, j: (0, j)),
                  pl.BlockSpec((1, tn), lambda i, j: (0, jnp.minimum(j, n_qk_tiles - 1))),
                  pl.BlockSpec((tm, LANES), lambda i, j: (i % tiles_per_seq, 0)),
                  pl.BlockSpec((tm, LANES), lambda i, j: (i % tiles_per_seq, 0))],
        out_specs=[pl.BlockSpec((tm, tn), lambda i, j: (i, j)),
                   pl.BlockSpec((tm, d), lambda i, j: (i, 0))],
        out_shape=[jax.ShapeDtypeStruct((t, n_out), BF16),
                   jax.ShapeDtypeStruct((t, d), BF16)],
        scratch_shapes=[pltpu.VMEM((d, tn), BF16)],
        compiler_params=_params(("arbitrary", "arbitrary")),
        name="qkv_proj",
    )(x, attn_norm_w.reshape(1, d), w_in, nw, cos, sin)


def _conv_kernel(xn_ref, wb_ref, wc_ref, wh_ref, cw_ref, o_ref,
                 wb_scr, wc_scr, wh_scr, u_scr, carry_scr, *, tiles_per_seq, rb):
    i = pl.program_id(0)
    j = pl.program_id(1)
    tm = xn_ref.shape[0]
    wc_scr[...] = wc_ref[...].astype(BF16)
    wh_scr[...] = wh_ref[...].astype(BF16)
    wb_scr[...] = wb_ref[...].astype(BF16)

    @pl.when(i % tiles_per_seq == 0)
    def _():
        u_scr[0:SUBLANES, :] = jnp.zeros((SUBLANES, u_scr.shape[1]), F32)

    @pl.when(i % tiles_per_seq != 0)
    def _():
        u_scr[0:SUBLANES, :] = carry_scr[j]

    def xn_rows(r):
        return xn_ref[r * rb:(r + 1) * rb, :]

    def gate_matmuls(r):
        return (jnp.dot(xn_rows(r), wc_scr[...], preferred_element_type=F32),
                jnp.dot(xn_rows(r), wh_scr[...], preferred_element_type=F32))

    def store_u(r, acc):
        u_scr[SUBLANES + r * rb:SUBLANES + (r + 1) * rb, :] = acc[0] * acc[1]

    _row_block_pipeline(tm // rb, gate_matmuls, store_u)
    carry_scr[j] = u_scr[tm:tm + SUBLANES, :]
    cw = cw_ref[...]

    def b_matmul(r):
        return jnp.dot(xn_rows(r), wb_scr[...], preferred_element_type=F32)

    def conv_out(r, gate_b):
        def shifted(back):
            start = SUBLANES + r * rb - back
            return u_scr[start:start + rb, :]
        y = cw[2:3, :] * shifted(0) + cw[1:2, :] * shifted(1) + cw[0:1, :] * shifted(2)
        o_ref[r * rb:(r + 1) * rb, :] = (gate_b * y).astype(o_ref.dtype)

    _row_block_pipeline(tm // rb, b_matmul, conv_out)


def _conv_proj(xn, w_in, conv_w, col_b, col_c, col_h, n, seq, tm, tn, rb):
    t, d = xn.shape
    jb, jc, jh = col_b // tn, col_c // tn, col_h // tn
    nj = n // tn
    kern = functools.partial(_conv_kernel, tiles_per_seq=seq // tm, rb=rb)
    w_scratch = [pltpu.VMEM((d, tn), BF16)] * 3
    return pl.pallas_call(
        kern,
        grid=(t // tm, nj),
        in_specs=[pl.BlockSpec((tm, d), lambda i, j: (i, 0)),
                  pl.BlockSpec((d, tn), lambda i, j: (0, jb + j)),
                  pl.BlockSpec((d, tn), lambda i, j: (0, jc + j)),
                  pl.BlockSpec((d, tn), lambda i, j: (0, jh + j)),
                  pl.BlockSpec((CONV_K, tn), lambda i, j: (0, j))],
        out_specs=pl.BlockSpec((tm, tn), lambda i, j: (i, j)),
        out_shape=jax.ShapeDtypeStruct((t, n), BF16),
        scratch_shapes=w_scratch + [pltpu.VMEM((tm + SUBLANES, tn), F32),
                                    pltpu.VMEM((nj, SUBLANES, tn), F32)],
        compiler_params=_params(("arbitrary", "arbitrary")),
        name="conv_proj",
    )(xn, w_in, w_in, w_in, conv_w)


def _attn_kernel(q_ref, k_ref, v_ref, lam_ref, sw_ref, o_ref,
                 vt_scr, bias_scr, qq_a, qq_b, m_a, m_b, acc_a, acc_b, *s_bufs,
                 tq, tk, pw, ahead):
    seq = k_ref.shape[1]
    n_q = seq // tq
    n_diag = tq // tk
    n_buf = len(s_bufs)
    n_panels = 2 * tq // pw
    assert n_buf == n_diag and ahead < n_buf
    all_visible = ("full",) * n_panels
    state = ((qq_a, m_a, acc_a), (qq_b, m_b, acc_b))

    for c in range(seq // LANES):
        blk = v_ref[0, c * LANES:(c + 1) * LANES, :].astype(F32)
        vt_scr[0:V_DIM, c * LANES:(c + 1) * LANES] = blk.T.astype(BF16)
    vt_scr[V_DIM:, 0:seq] = jnp.ones((vt_scr.shape[0] - V_DIM, seq), BF16)
    key = lax.broadcasted_iota(jnp.int32, bias_scr.shape, 0)
    qry = lax.broadcasted_iota(jnp.int32, bias_scr.shape, 1)
    bias_scr[...] = jnp.where(key // CHUNK <= qry // CHUNK, 0.0, -1e30)

    lq1, lk1, lq2, lk2 = (lam_ref[r:r + 1, :] for r in range(4))
    lam = (jnp.exp(jnp.sum(lq1 * lk1, axis=-1, keepdims=True))
           - jnp.exp(jnp.sum(lq2 * lk2, axis=-1, keepdims=True)) + LAMBDA_INIT)

    def diag_modes(key_off):
        modes = []
        for c in range(n_panels):
            q_off = (c * pw) % tq
            if key_off + tk <= q_off:
                modes.append("full")
            elif key_off >= q_off + pw:
                modes.append("skip")
            else:
                assert key_off == q_off and tk == pw
                modes.append("tri")
        return tuple(modes)

    def prepare(qi):
        qq_scr, m_scr, acc_scr = state[qi % 2]
        dim = lax.broadcasted_iota(jnp.int32, (V_DIM, 1), 0)
        for c in range(tq // LANES):
            rows = slice(qi * tq + c * LANES, qi * tq + (c + 1) * LANES)
            qt = q_ref[0, rows, :].astype(F32).T
            cols = slice(c * LANES, (c + 1) * LANES)
            qq_scr[:, cols] = jnp.where(dim < HEAD_DIM, qt, 0.0).astype(BF16)
            cols = slice(tq + c * LANES, tq + (c + 1) * LANES)
            qq_scr[:, cols] = jnp.where(dim < HEAD_DIM, 0.0, qt).astype(BF16)
        m_scr[...] = jnp.full(m_scr.shape, -jnp.inf, F32)
        acc_scr[...] = jnp.zeros(acc_scr.shape, F32)

    def scores(qi, t, modes=all_visible):
        qq_scr = state[qi % 2][0]
        dst = s_bufs[t % n_buf]
        k = k_ref[0, t * tk:(t + 1) * tk, :]
        for c in range(n_panels):
            if modes[c] == "skip":
                continue
            cols = slice(c * pw, (c + 1) * pw)
            dst[:, cols] = jnp.dot(k, qq_scr[:, cols], preferred_element_type=F32)

    def update(qi, t, modes=all_visible):
        _, m_scr, acc_scr = state[qi % 2]
        src = s_bufs[t % n_buf]
        vt = vt_scr[:, t * tk:(t + 1) * tk]
        for c in range(n_panels):
            if modes[c] == "skip":
                continue
            cols = slice(c * pw, (c + 1) * pw)
            s = src[:, cols]
            if modes[c] == "tri":
                s = s + bias_scr[...]
            m_prev = m_scr[:, cols]
            m_new = jnp.maximum(m_prev, jnp.max(s, axis=0, keepdims=True))
            alpha = jnp.exp2(m_prev - m_new)
            p = jnp.exp2(s - m_new).astype(BF16)
            m_scr[:, cols] = m_new
            acc_scr[:, cols] = alpha * acc_scr[:, cols] + jnp.dot(
                vt, p, preferred_element_type=F32)

    def finish(qi):
        acc = state[qi % 2][2][:, 0:2 * tq]
        o = acc[0:V_DIM, :] / acc[V_DIM:V_DIM + 1, :]
        a = (o[:, 0:tq] - lam * o[:, tq:]).T
        ms = jnp.mean(a * a, axis=-1, keepdims=True)
        o_ref[0, qi * tq:(qi + 1) * tq, :] = (
            ((a * lax.rsqrt(ms + SUBLN_EPS)) * sw_ref[...])
            * (1.0 - LAMBDA_INIT)).astype(o_ref.dtype)

    prepare(0)
    for u in range(ahead):
        scores(0, u, diag_modes(u * tk))
    for qi in range(n_q):
        n_full = n_diag * qi

        def tile_modes(t):
            return all_visible if t < n_full else diag_modes((t - n_full) * tk)

        for g in range(qi + 1):
            t0 = n_buf * g
            for u in range(ahead, n_buf):
                scores(qi, t0 + u, tile_modes(t0 + u))
            if g == qi and qi + 1 < n_q:
                prepare(qi + 1)
            for u in range(n_buf):
                update(qi, t0 + u, tile_modes(t0 + u))
                if u < ahead:
                    if g < qi:
                        scores(qi, t0 + n_buf + u, tile_modes(t0 + n_buf + u))
                    elif qi + 1 < n_q:
                        scores(qi + 1, u)
        finish(qi)


def _attention(qkv, lam_vecs, subln_w, tq, tk, pw, ahead):
    b, s, w3 = qkv.shape
    w = w3 // 3
    h = w // V_DIM
    kern = functools.partial(_attn_kernel, tq=tq, tk=tk, pw=pw, ahead=ahead)

    def head_block(first):
        return pl.BlockSpec((1, s, V_DIM), lambda bi, hi: (bi, 0, first + hi))

    per_q_tile = ([pltpu.VMEM((V_DIM, 2 * tq), BF16)] * 2
                  + [pltpu.VMEM((1, 2 * tq), F32)] * 2
                  + [pltpu.VMEM((V_DIM + BF16_SUBLANES, 2 * tq + LANES), F32)] * 2)
    return pl.pallas_call(
        kern,
        grid=(b, h),
        in_specs=[head_block(0), head_block(h), head_block(2 * h),
                  pl.BlockSpec((4, HEAD_DIM), lambda bi, hi: (0, 0)),
                  pl.BlockSpec((1, V_DIM), lambda bi, hi: (0, 0))],
        out_specs=head_block(0),
        out_shape=jax.ShapeDtypeStruct((b, s, w), BF16),
        scratch_shapes=[pltpu.VMEM((V_DIM + BF16_SUBLANES, s + LANES), BF16),
                        pltpu.VMEM((tk, pw), F32)]
        + per_q_tile
        + [pltpu.VMEM((tk, 2 * tq + LANES), F32)] * (tq // tk),
        compiler_params=_params(("parallel", "parallel")),
        name="diff_attention",
    )(qkv, qkv, qkv, lam_vecs, subln_w)


def _out_kernel(a_ref, c_ref, wa_ref, wc_ref, x_ref, fnw_ref, o_ref, fn_ref, h_scr,
                *, rb):
    j = pl.program_id(1)
    n_j, tm, tn = h_scr.shape
    h = x_ref[...] + _mm(a_ref[...], wa_ref[...]) + _mm(c_ref[...], wc_ref[...])
    o_ref[...] = h
    h_scr[j] = h

    @pl.when(j == n_j - 1)
    def _():
        for r in range(tm // rb):
            rows = slice(r * rb, (r + 1) * rb)
            ssq = sum(jnp.sum(h_scr[jj, rows, :] * h_scr[jj, rows, :], axis=-1, keepdims=True)
                      for jj in range(n_j))
            scale = lax.rsqrt(ssq * (1.0 / (n_j * tn)) + EPS)
            for jj in range(n_j):
                cols = slice(jj * tn, (jj + 1) * tn)
                fn_ref[rows, cols] = ((h_scr[jj, rows, :] * scale)
                                      * fnw_ref[:, cols]).astype(fn_ref.dtype)


def _out_proj(attn, conv, w_out, x, ffn_norm_w, tm, tn, rb):
    t, ka = attn.shape
    kc = conv.shape[1]
    n = w_out.shape[1]
    assert ka == kc
    return pl.pallas_call(
        functools.partial(_out_kernel, rb=rb),
        grid=(t // tm, n // tn),
        in_specs=[pl.BlockSpec((tm, ka), lambda i, j: (i, 0)),
                  pl.BlockSpec((tm, kc), lambda i, j: (i, 0)),
                  pl.BlockSpec((ka, tn), lambda i, j: (0, j)),
                  pl.BlockSpec((kc, tn), lambda i, j: (1, j)),
                  pl.BlockSpec((tm, tn), lambda i, j: (i, j)),
                  pl.BlockSpec((1, n), lambda i, j: (0, 0))],
        out_specs=[pl.BlockSpec((tm, tn), lambda i, j: (i, j)),
                   pl.BlockSpec((tm, n), lambda i, j: (i, 0))],
        out_shape=[jax.ShapeDtypeStruct((t, n), F32),
                   jax.ShapeDtypeStruct((t, n), BF16)],
        scratch_shapes=[pltpu.VMEM((n // tn, tm, tn), F32)],
        compiler_params=_params(("arbitrary", "arbitrary")),
        name="out_proj",
    )(attn, conv, w_out, w_out, x, ffn_norm_w.reshape(1, n))


def _gate_up_kernel(x_ref, wg_ref, wu_ref, o_ref):
    x = x_ref[...]
    g = _mm(x, wg_ref[...])
    u = _mm(x, wu_ref[...])
    o_ref[...] = ((g * jax.nn.sigmoid(g)) * u).astype(o_ref.dtype)


def _gate_up(fn, w_gate, w_up, tm, tn):
    t, d = fn.shape
    f = w_gate.shape[1]
    return pl.pallas_call(
        _gate_up_kernel,
        grid=(t // tm, f // tn),
        in_specs=[pl.BlockSpec((tm, d), lambda i, j: (i, 0)),
                  pl.BlockSpec((d, tn), lambda i, j: (0, j)),
                  pl.BlockSpec((d, tn), lambda i, j: (0, j))],
        out_specs=pl.BlockSpec((tm, tn), lambda i, j: (i, j)),
        out_shape=jax.ShapeDtypeStruct((t, f), BF16),
        compiler_params=_params(("parallel", "parallel")),
        name="gate_up",
    )(fn, w_gate, w_up)


def _down_kernel(h_ref, w_ref, r_ref, o_ref):
    o_ref[...] = r_ref[...] + _mm(h_ref[...], w_ref[...])


def _down(h, w_down, resid, tm, tn):
    t, f = h.shape
    n = w_down.shape[1]
    return pl.pallas_call(
        _down_kernel,
        grid=(t // tm, n // tn),
        in_specs=[pl.BlockSpec((tm, f), lambda i, j: (i, 0)),
                  pl.BlockSpec((f, tn), lambda i, j: (0, j)),
                  pl.BlockSpec((tm, tn), lambda i, j: (i, j))],
        out_specs=pl.BlockSpec((tm, tn), lambda i, j: (i, j)),
        out_shape=jax.ShapeDtypeStruct((t, n), F32),
        compiler_params=_params(("parallel", "parallel")),
        name="down_proj",
    )(h, w_down, resid)


TILES = dict(
    proj_tm=1024, proj_tn=1024, proj_rb=256,
    conv_tm=1024, conv_tn=512, conv_rb=256,
    attn_tq=1024, attn_tk=256, attn_pw=256, attn_ahead=2,
    out_tm=1024, out_tn=512, out_rb=256,
    gu_tm=2048, gu_tn=512,
    down_tm=1024, down_tn=256,
)


def _rope_tables(seq):
    pos = jnp.arange(seq, dtype=F32)
    inv_freq = ROPE_THETA ** (-jnp.arange(0, HEAD_DIM, 2, dtype=F32) / HEAD_DIM)
    ang = pos[:, None] * inv_freq[None, :]
    cos, sin = jnp.cos(ang), jnp.sin(ang)
    reps = LANES // (HEAD_DIM // 2)
    cos_l = jnp.tile(cos, (1, reps))
    sin_l = jnp.tile(jnp.concatenate([-sin, sin], axis=1), (1, reps // 2))
    return cos_l, sin_l


def kernel(x, attn_norm_w, w_in, q_norm_w, k_norm_w, lambda_q1, lambda_k1,
           lambda_q2, lambda_k2, subln_w, conv_w, w_out, ffn_norm_w, w_gate,
           w_up, w_down):
    b, s, d = x.shape
    t = b * s
    attn_w = N_HEADS * V_DIM
    qk_cols = N_HEADS * 2 * HEAD_DIM
    conv_width = d - attn_w
    col_v = 2 * qk_cols
    col_b = col_v + attn_w
    col_c = col_b + conv_width
    col_h = col_c + conv_width
    tl = TILES
    assert w_in.shape[0] == 1, "single-layer block"

    x2 = x.reshape(t, d)
    cos_l, sin_l = _rope_tables(s)
    scale = math.log2(math.e) / math.sqrt(HEAD_DIM)
    nw = jnp.concatenate([jnp.tile(q_norm_w[0] * scale, qk_cols // HEAD_DIM),
                          jnp.tile(k_norm_w[0], qk_cols // HEAD_DIM)]).reshape(1, -1)
    lam_vecs = jnp.concatenate([lambda_q1, lambda_k1, lambda_q2, lambda_k2], axis=0)

    qkv, xn = _qkv_proj(x2, attn_norm_w[0], w_in[0], nw, cos_l, sin_l, s, col_b,
                        tl["proj_tm"], tl["proj_tn"], tl["proj_rb"])
    conv = _conv_proj(xn, w_in[0], conv_w[0], col_b, col_c, col_h, conv_width, s,
                      tl["conv_tm"], tl["conv_tn"], tl["conv_rb"])

    attn = _attention(qkv.reshape(b, s, col_b), lam_vecs, subln_w[0].reshape(1, V_DIM),
                      tl["attn_tq"], tl["attn_tk"], tl["attn_pw"], tl["attn_ahead"])

    h1, fn = _out_proj(attn.reshape(t, attn_w), conv, w_out[0], x2, ffn_norm_w[0],
                       tl["out_tm"], tl["out_tn"], tl["out_rb"])
    hid = _gate_up(fn, w_gate[0], w_up[0], tl["gu_tm"], tl["gu_tn"])
    out = _down(hid, w_down[0], h1, tl["down_tm"], tl["down_tn"])
    return out.reshape(b, s, d)
```

```python
import functools
import math

import jax
import jax.numpy as jnp
from jax import lax
from jax.experimental import pallas as pl
from jax.experimental.pallas import tpu as pltpu

F32 = jnp.float32
BF16 = jnp.bfloat16

CHUNK = 64
HEAD_DIM = 64
V_DIM = 2 * HEAD_DIM
N_HEADS = 8
CONV_K = 3
ROPE_THETA = 10000.0
EPS = 1e-6
SUBLN_EPS = 1e-5
LAMBDA_INIT = 0.8 - 0.6 * math.exp(-0.3 * 0)

LANES = 128
SUBLANES = 8
BF16_SUBLANES = 16
VMEM_LIMIT = 56 * 1024 * 1024


def _params(semantics, flags=None):
    return pltpu.CompilerParams(dimension_semantics=semantics,
                                vmem_limit_bytes=VMEM_LIMIT, flags=flags)


def _mm(a_bf16, w_f32):
    return jnp.dot(a_bf16, w_f32.astype(BF16), preferred_element_type=F32)


def _rms_scale(x, w, eps):
    ms = jnp.mean(x * x, axis=-1, keepdims=True)
    return (x * lax.rsqrt(ms + eps)) * w


def _row_block_pipeline(n_blocks, matmul, epilogue):
    acc = matmul(0)
    for r in range(n_blocks):
        nxt = matmul(r + 1) if r + 1 < n_blocks else None
        epilogue(r, acc)
        acc = nxt


def _qkv_kernel(x_ref, anw_ref, w_ref, nw_ref, cos_ref, sin_ref, o_ref, xn_ref,
                w_scr, *, rb, n_qk_tiles):
    j = pl.program_id(1)
    tm, tn = o_ref.shape
    w_scr[...] = w_ref[...].astype(BF16)
    lane = lax.broadcasted_iota(jnp.int32, (1, LANES), 1)
    lo_half = lane < HEAD_DIM
    first = (lane % HEAD_DIM) < (HEAD_DIM // 2)

    def matmul(r):
        return jnp.dot(xn_ref[r * rb:(r + 1) * rb, :], w_scr[...],
                       preferred_element_type=F32)

    def norm_then_matmul(r):
        rows = slice(r * rb, (r + 1) * rb)
        xn_ref[rows, :] = _rms_scale(x_ref[rows, :], anw_ref[...], EPS).astype(BF16)
        return matmul(r)

    def plain(r, acc):
        o_ref[r * rb:(r + 1) * rb, :] = acc.astype(o_ref.dtype)

    def norm_rope(r, acc):
        rows = slice(r * rb, (r + 1) * rb)
        cos = cos_ref[rows, :]
        sin = sin_ref[rows, :]
        for c in range(tn // LANES):
            cols = slice(c * LANES, (c + 1) * LANES)
            y = acc[:, cols]
            sq = y * y
            s_lo = jnp.sum(jnp.where(lo_half, sq, 0.0), axis=-1, keepdims=True)
            s_hi = jnp.sum(jnp.where(lo_half, 0.0, sq), axis=-1, keepdims=True)
            ms = jnp.where(lo_half, s_lo, s_hi) * (1.0 / HEAD_DIM)
            yn = (y * lax.rsqrt(ms + EPS)) * nw_ref[:, cols]
            rot = jnp.where(first, pltpu.roll(yn, LANES - HEAD_DIM // 2, 1),
                            pltpu.roll(yn, HEAD_DIM // 2, 1))
            o_ref[rows, cols] = (yn * cos + rot * sin).astype(o_ref.dtype)

    @pl.when(j == 0)
    def _():
        _row_block_pipeline(tm // rb, norm_then_matmul, norm_rope)

    @pl.when((j > 0) & (j < n_qk_tiles))
    def _():
        _row_block_pipeline(tm // rb, matmul, norm_rope)

    @pl.when(j >= n_qk_tiles)
    def _():
        _row_block_pipeline(tm // rb, matmul, plain)


def _qkv_proj(x, attn_norm_w, w_in, nw, cos, sin, seq, n_out, tm, tn, rb):
    t, d = x.shape
    n_qk_tiles = nw.shape[1] // tn
    tiles_per_seq = seq // tm
    kern = functools.partial(_qkv_kernel, rb=rb, n_qk_tiles=n_qk_tiles)
    return pl.pallas_call(
        kern,
        grid=(t // tm, n_out // tn),
        in_specs=[pl.BlockSpec((tm, d), lambda i, j: (i, 0)),
                  pl.BlockSpec((1, d), lambda i, j: (0, 0)),
                  pl.BlockSpec((d, tn), lambda i, j: (0, j)),
                  pl.BlockSpec((1, tn), lambda i, j: (0, jnp.minimum(j, n_qk_tiles - 1))),
                  pl.BlockSpec((tm, LANES), lambda i, j: (i % tiles_per_seq, 0)),
                  pl.BlockSpec((tm, LANES), lambda i, j: (i % tiles_per_seq, 0))],
        out_specs=[pl.BlockSpec((tm, tn), lambda i, j: (i, j)),
                   pl.BlockSpec((tm, d), lambda i, j: (i, 0))],
        out_shape=[jax.ShapeDtypeStruct((t, n_out), BF16),
                   jax.ShapeDtypeStruct((t, d), BF16)],
        scratch_shapes=[pltpu.VMEM((d, tn), BF16)],
        compiler_params=_params(("arbitrary", "arbitrary")),
        name="qkv_proj",
    )(x, attn_norm_w.reshape(1, d), w_in, nw, cos, sin)


def _conv_kernel(xn_ref, wb_ref, wc_ref, wh_ref, cw_ref, o_ref,
                 wb_scr, wc_scr, wh_scr, u_scr, carry_scr, *, tiles_per_seq, rb):
    i = pl.program_id(0)
    j = pl.program_id(1)
    tm = xn_ref.shape[0]
    wc_scr[...] = wc_ref[...].astype(BF16)
    wh_scr[...] = wh_ref[...].astype(BF16)
    wb_scr[...] = wb_ref[...].astype(BF16)

    @pl.when(i % tiles_per_seq == 0)
    def _():
        u_scr[0:SUBLANES, :] = jnp.zeros((SUBLANES, u_scr.shape[1]), F32)

    @pl.when(i % tiles_per_seq != 0)
    def _():
        u_scr[0:SUBLANES, :] = carry_scr[j]

    def xn_rows(r):
        return xn_ref[r * rb:(r + 1) * rb, :]

    def gate_matmuls(r):
        return (jnp.dot(xn_rows(r), wc_scr[...], preferred_element_type=F32),
                jnp.dot(xn_rows(r), wh_scr[...], preferred_element_type=F32))

    def store_u(r, acc):
        u_scr[SUBLANES + r * rb:SUBLANES + (r + 1) * rb, :] = acc[0] * acc[1]

    _row_block_pipeline(tm // rb, gate_matmuls, store_u)
    carry_scr[j] = u_scr[tm:tm + SUBLANES, :]
    cw = cw_ref[...]

    def b_matmul(r):
        return jnp.dot(xn_rows(r), wb_scr[...], preferred_element_type=F32)

    def conv_out(r, gate_b):
        def shifted(back):
            start = SUBLANES + r * rb - back
            return u_scr[start:start + rb, :]
        y = cw[2:3, :] * shifted(0) + cw[1:2, :] * shifted(1) + cw[0:1, :] * shifted(2)
        o_ref[r * rb:(r + 1) * rb, :] = (gate_b * y).astype(o_ref.dtype)

    _row_block_pipeline(tm // rb, b_matmul, conv_out)


def _conv_proj(xn, w_in, conv_w, col_b, col_c, col_h, n, seq, tm, tn, rb):
    t, d = xn.shape
    jb, jc, jh = col_b // tn, col_c // tn, col_h // tn
    nj = n // tn
    kern = functools.partial(_conv_kernel, tiles_per_seq=seq // tm, rb=rb)
    w_scratch = [pltpu.VMEM((d, tn), BF16)] * 3
    return pl.pallas_call(
        kern,
        grid=(t // tm, nj),
        in_specs=[pl.BlockSpec((tm, d), lambda i, j: (i, 0)),
                  pl.BlockSpec((d, tn), lambda i, j: (0, jb + j)),
                  pl.BlockSpec((d, tn), lambda i, j: (0, jc + j)),
                  pl.BlockSpec((d, tn), lambda i, j: (0, jh + j)),
                  pl.BlockSpec((CONV_K, tn), lambda i, j: (0, j))],
        out_specs=pl.BlockSpec((tm, tn), lambda i, j: (i, j)),
        out_shape=jax.ShapeDtypeStruct((t, n), BF16),
        scratch_shapes=w_scratch + [pltpu.VMEM((tm + SUBLANES, tn), F32),
                                    pltpu.VMEM((nj, SUBLANES, tn), F32)],
        compiler_params=_params(("arbitrary", "arbitrary")),
        name="conv_proj",
    )(xn, w_in, w_in, w_in, conv_w)


def _attn_kernel(q_ref, k_ref, v_ref, lam_ref, sw_ref, o_ref,
                 vt_scr, bias_scr, qq_a, qq_b, m_a, m_b, acc_a, acc_b, *s_bufs,
                 tq, tk, pw, ahead):
    seq = k_ref.shape[1]
    n_q = seq // tq
    n_diag = tq // tk
    n_buf = len(s_bufs)
    n_panels = 2 * tq // pw
    assert n_buf == n_diag and ahead < n_buf
    all_visible = ("full",) * n_panels
    state = ((qq_a, m_a, acc_a), (qq_b, m_b, acc_b))

    for c in range(seq // LANES):
        blk = v_ref[0, c * LANES:(c + 1) * LANES, :].astype(F32)
        vt_scr[0:V_DIM, c * LANES:(c + 1) * LANES] = blk.T.astype(BF16)
    vt_scr[V_DIM:, 0:seq] = jnp.ones((vt_scr.shape[0] - V_DIM, seq), BF16)
    key = lax.broadcasted_iota(jnp.int32, bias_scr.shape, 0)
    qry = lax.broadcasted_iota(jnp.int32, bias_scr.shape, 1)
    bias_scr[...] = jnp.where(key // CHUNK <= qry // CHUNK, 0.0, -1e30)

    lq1, lk1, lq2, lk2 = (lam_ref[r:r + 1, :] for r in range(4))
    lam = (jnp.exp(jnp.sum(lq1 * lk1, axis=-1, keepdims=True))
           - jnp.exp(jnp.sum(lq2 * lk2, axis=-1, keepdims=True)) + LAMBDA_INIT)

    def diag_modes(key_off):
        modes = []
        for c in range(n_panels):
            q_off = (c * pw) % tq
            if key_off + tk <= q_off:
                modes.append("full")
            elif key_off >= q_off + pw:
                modes.append("skip")
            else:
                assert key_off == q_off and tk == pw
                modes.append("tri")
        return tuple(modes)

    def prepare(qi):
        qq_scr, m_scr, acc_scr = state[qi % 2]
        dim = lax.broadcasted_iota(jnp.int32, (V_DIM, 1), 0)
        for c in range(tq // LANES):
            rows = slice(qi * tq + c * LANES, qi * tq + (c + 1) * LANES)
            qt = q_ref[0, rows, :].astype(F32).T
            cols = slice(c * LANES, (c + 1) * LANES)
            qq_scr[:, cols] = jnp.where(dim < HEAD_DIM, qt, 0.0).astype(BF16)
            cols = slice(tq + c * LANES, tq + (c + 1) * LANES)
            qq_scr[:, cols] = jnp.where(dim < HEAD_DIM, 0.0, qt).astype(BF16)
        m_scr[...] = jnp.full(m_scr.shape, -jnp.inf, F32)
        acc_scr[...] = jnp.zeros(acc_scr.shape, F32)

    def scores(qi, t, modes=all_visible):
        qq_scr = state[qi % 2][0]
        dst = s_bufs[t % n_buf]
        k = k_ref[0, t * tk:(t + 1) * tk, :]
        for c in range(n_panels):
            if modes[c] == "skip":
                continue
            cols = slice(c * pw, (c + 1) * pw)
            dst[:, cols] = jnp.dot(k, qq_scr[:, cols], preferred_element_type=F32)

    def update(qi, t, modes=all_visible):
        _, m_scr, acc_scr = state[qi % 2]
        src = s_bufs[t % n_buf]
        vt = vt_scr[:, t * tk:(t + 1) * tk]
        for c in range(n_panels):
            if modes[c] == "skip":
                continue
            cols = slice(c * pw, (c + 1) * pw)
            s = src[:, cols]
            if modes[c] == "tri":
                s = s + bias_scr[...]
            m_prev = m_scr[:, cols]
            m_new = jnp.maximum(m_prev, jnp.max(s, axis=0, keepdims=True))
            alpha = jnp.exp2(m_prev - m_new)
            p = jnp.exp2(s - m_new).astype(BF16)
            m_scr[:, cols] = m_new
            acc_scr[:, cols] = alpha * acc_scr[:, cols] + jnp.dot(
                vt, p, preferred_element_type=F32)

    def finish(qi):
        acc = state[qi % 2][2][:, 0:2 * tq]
        o = acc[0:V_DIM, :] / acc[V_DIM:V_DIM + 1, :]
        a = (o[:, 0:tq] - lam * o[:, tq:]).T
        ms = jnp.mean(a * a, axis=-1, keepdims=True)
        o_ref[0, qi * tq:(qi + 1) * tq, :] = (
            ((a * lax.rsqrt(ms + SUBLN_EPS)) * sw_ref[...])
            * (1.0 - LAMBDA_INIT)).astype(o_ref.dtype)

    prepare(0)
    for u in range(ahead):
        scores(0, u, diag_modes(u * tk))
    for qi in range(n_q):
        n_full = n_diag * qi

        def tile_modes(t):
            return all_visible if t < n_full else diag_modes((t - n_full) * tk)

        for g in range(qi + 1):
            t0 = n_buf * g
            for u in range(ahead, n_buf):
                scores(qi, t0 + u, tile_modes(t0 + u))
            if g == qi and qi + 1 < n_q:
                prepare(qi + 1)
            for u in range(n_buf):
                update(qi, t0 + u, tile_modes(t0 + u))
                if u < ahead:
                    if g < qi:
                        scores(qi, t0 + n_buf + u, tile_modes(t0 + n_buf + u))
                    elif qi + 1 < n_q:
                        scores(qi + 1, u)
        finish(qi)


def _attention(qkv, lam_vecs, subln_w, tq, tk, pw, ahead):
    b, s, w3 = qkv.shape
    w = w3 // 3
    h = w // V_DIM
    kern = functools.partial(_attn_kernel, tq=tq, tk=tk, pw=pw, ahead=ahead)

    def head_block(first):
        return pl.BlockSpec((1, s, V_DIM), lambda bi, hi: (bi, 0, first + hi))

    per_q_tile = ([pltpu.VMEM((V_DIM, 2 * tq), BF16)] * 2
                  + [pltpu.VMEM((1, 2 * tq), F32)] * 2
                  + [pltpu.VMEM((V_DIM + BF16_SUBLANES, 2 * tq + LANES), F32)] * 2)
    return pl.pallas_call(
        kern,
        grid=(b, h),
        in_specs=[head_block(0), head_block(h), head_block(2 * h),
                  pl.BlockSpec((4, HEAD_DIM), lambda bi, hi: (0, 0)),
                  pl.BlockSpec((1, V_DIM), lambda bi, hi: (0, 0))],
        out_specs=head_block(0),
        out_shape=jax.ShapeDtypeStruct((b, s, w), BF16),
        scratch_shapes=[pltpu.VMEM((V_DIM + BF16_SUBLANES, s + LANES), BF16),
                        pltpu.VMEM((tk, pw), F32)]
        + per_q_tile
        + [pltpu.VMEM((tk, 2 * tq + LANES), F32)] * (tq // tk),
        compiler_params=_params(("parallel", "parallel")),
        name="diff_attention",
    )(qkv, qkv, qkv, lam_vecs, subln_w)


def _out_kernel(a_ref, c_ref, wa_ref, wc_ref, x_ref, fnw_ref, o_ref, fn_ref,
                wa_scr, wc_scr, h_scr, *, rb):
    j = pl.program_id(1)
    n_j, tm, tn = h_scr.shape
    last = n_j - 1
    wa_scr[...] = wa_ref[...].astype(BF16)
    wc_scr[...] = wc_ref[...].astype(BF16)

    def matmul(r):
        rows = slice(r * rb, (r + 1) * rb)
        return (x_ref[rows, :]
                + jnp.dot(a_ref[rows, :], wa_scr[...], preferred_element_type=F32)
                + jnp.dot(c_ref[rows, :], wc_scr[...], preferred_element_type=F32))

    def store(r, h):
        rows = slice(r * rb, (r + 1) * rb)
        o_ref[rows, :] = h
        h_scr[j, rows, :] = h

    def store_and_norm(r, h):
        rows = slice(r * rb, (r + 1) * rb)
        o_ref[rows, :] = h
        pieces = [h_scr[jj, rows, :] for jj in range(last)] + [h]
        ssq = sum(jnp.sum(piece * piece, axis=-1, keepdims=True) for piece in pieces)
        scale = lax.rsqrt(ssq * (1.0 / (n_j * tn)) + EPS)
        for jj, piece in enumerate(pieces):
            cols = slice(jj * tn, (jj + 1) * tn)
            fn_ref[rows, cols] = ((piece * scale) * fnw_ref[:, cols]).astype(fn_ref.dtype)

    @pl.when(j < last)
    def _():
        _row_block_pipeline(tm // rb, matmul, store)

    @pl.when(j == last)
    def _():
        _row_block_pipeline(tm // rb, matmul, store_and_norm)


def _out_proj(attn, conv, w_out, x, ffn_norm_w, tm, tn, rb):
    t, ka = attn.shape
    kc = conv.shape[1]
    n = w_out.shape[1]
    assert ka == kc
    return pl.pallas_call(
        functools.partial(_out_kernel, rb=rb),
        grid=(t // tm, n // tn),
        in_specs=[pl.BlockSpec((tm, ka), lambda i, j: (i, 0)),
                  pl.BlockSpec((tm, kc), lambda i, j: (i, 0)),
                  pl.BlockSpec((ka, tn), lambda i, j: (0, j)),
                  pl.BlockSpec((kc, tn), lambda i, j: (1, j)),
                  pl.BlockSpec((tm, tn), lambda i, j: (i, j)),
                  pl.BlockSpec((1, n), lambda i, j: (0, 0))],
        out_specs=[pl.BlockSpec((tm, tn), lambda i, j: (i, j)),
                   pl.BlockSpec((tm, n), lambda i, j: (i, 0))],
        out_shape=[jax.ShapeDtypeStruct((t, n), F32),
                   jax.ShapeDtypeStruct((t, n), BF16)],
        scratch_shapes=[pltpu.VMEM((ka, tn), BF16), pltpu.VMEM((kc, tn), BF16),
                        pltpu.VMEM((n // tn, tm, tn), F32)],
        compiler_params=_params(("arbitrary", "arbitrary")),
        name="out_proj",
    )(attn, conv, w_out, w_out, x, ffn_norm_w.reshape(1, n))


def _gate_up_kernel(x_ref, wg_ref, wu_ref, o_ref):
    x = x_ref[...]
    g = _mm(x, wg_ref[...])
    u = _mm(x, wu_ref[...])
    o_ref[...] = ((g * jax.nn.sigmoid(g)) * u).astype(o_ref.dtype)


def _gate_up(fn, w_gate, w_up, tm, tn):
    t, d = fn.shape
    f = w_gate.shape[1]
    return pl.pallas_call(
        _gate_up_kernel,
        grid=(t // tm, f // tn),
        in_specs=[pl.BlockSpec((tm, d), lambda i, j: (i, 0)),
                  pl.BlockSpec((d, tn), lambda i, j: (0, j)),
                  pl.BlockSpec((d, tn), lambda i, j: (0, j))],
        out_specs=pl.BlockSpec((tm, tn), lambda i, j: (i, j)),
        out_shape=jax.ShapeDtypeStruct((t, f), BF16),
        compiler_params=_params(("parallel", "parallel")),
        name="gate_up",
    )(fn, w_gate, w_up)


def _down_kernel(h_ref, w_ref, r_ref, o_ref):
    o_ref[...] = r_ref[...] + _mm(h_ref[...], w_ref[...])


def _down(h, w_down, resid, tm, tn):
    t, f = h.shape
    n = w_down.shape[1]
    return pl.pallas_call(
        _down_kernel,
        grid=(t // tm, n // tn),
        in_specs=[pl.BlockSpec((tm, f), lambda i, j: (i, 0)),
                  pl.BlockSpec((f, tn), lambda i, j: (0, j)),
                  pl.BlockSpec((tm, tn), lambda i, j: (i, j))],
        out_specs=pl.BlockSpec((tm, tn), lambda i, j: (i, j)),
        out_shape=jax.ShapeDtypeStruct((t, n), F32),
        compiler_params=_params(("parallel", "parallel")),
        name="down_proj",
    )(h, w_down, resid)


TILES = dict(
    proj_tm=1024, proj_tn=1024, proj_rb=256,
    conv_tm=1024, conv_tn=512, conv_rb=256,
    attn_tq=1024, attn_tk=256, attn_pw=256, attn_ahead=2,
    out_tm=1024, out_tn=512, out_rb=256,
    gu_tm=1024, gu_tn=512,
    down_tm=1024, down_tn=256,
)


def _rope_tables(seq):
    pos = jnp.arange(seq, dtype=F32)
    inv_freq = ROPE_THETA ** (-jnp.arange(0, HEAD_DIM, 2, dtype=F32) / HEAD_DIM)
    ang = pos[:, None] * inv_freq[None, :]
    cos, sin = jnp.cos(ang), jnp.sin(ang)
    reps = LANES // (HEAD_DIM // 2)
    cos_l = jnp.tile(cos, (1, reps))
    sin_l = jnp.tile(jnp.concatenate([-sin, sin], axis=1), (1, reps // 2))
    return cos_l, sin_l


def kernel(x, attn_norm_w, w_in, q_norm_w, k_norm_w, lambda_q1, lambda_k1,
           lambda_q2, lambda_k2, subln_w, conv_w, w_out, ffn_norm_w, w_gate,
           w_up, w_down):
    b, s, d = x.shape
    t = b * s
    attn_w = N_HEADS * V_DIM
    qk_cols = N_HEADS * 2 * HEAD_DIM
    conv_width = d - attn_w
    col_v = 2 * qk_cols
    col_b = col_v + attn_w
    col_c = col_b + conv_width
    col_h = col_c + conv_width
    tl = TILES
    assert w_in.shape[0] == 1, "single-layer block"

    x2 = x.reshape(t, d)
    cos_l, sin_l = _rope_tables(s)
    scale = math.log2(math.e) / math.sqrt(HEAD_DIM)
    nw = jnp.concatenate([jnp.tile(q_norm_w[0] * scale, qk_cols // HEAD_DIM),
                          jnp.tile(k_norm_w[0], qk_cols // HEAD_DIM)]).reshape(1, -1)
    lam_vecs = jnp.concatenate([lambda_q1, lambda_k1, lambda_q2, lambda_k2], axis=0)

    qkv, xn = _qkv_proj(x2, attn_norm_w[0], w_in[0], nw, cos_l, sin_l, s, col_b,
                        tl["proj_tm"], tl["proj_tn"], tl["proj_rb"])
    conv = _conv_proj(xn, w_in[0], conv_w[0], col_b, col_c, col_h, conv_width, s,
                      tl["conv_tm"], tl["conv_tn"], tl["conv_rb"])

    attn = _attention(qkv.reshape(b, s, col_b), lam_vecs, subln_w[0].reshape(1, V_DIM),
                      tl["attn_tq"], tl["attn_tk"], tl["attn_pw"], tl["attn_ahead"])

    h1, fn = _out_proj(attn.reshape(t, attn_w), conv, w_out[0], x2, ffn_norm_w[0],
                       tl["out_tm"], tl["out_tn"], tl["out_rb"])
    hid = _gate_up(fn, w_gate[0], w_up[0], tl["gu_tm"], tl["gu_tn"])
    out = _down(hid, w_down[0], h1, tl["down_tm"], tl["down_tn"])
    return out.reshape(b, s, d)
```

```python
import functools
import math

import jax
import jax.numpy as jnp
from jax import lax
from jax.experimental import pallas as pl
from jax.experimental.pallas import tpu as pltpu

F32 = jnp.float32
BF16 = jnp.bfloat16

CHUNK = 64
HEAD_DIM = 64
V_DIM = 2 * HEAD_DIM
N_HEADS = 8
CONV_K = 3
ROPE_THETA = 10000.0
EPS = 1e-6
SUBLN_EPS = 1e-5
LAMBDA_INIT = 0.8 - 0.6 * math.exp(-0.3 * 0)

LANES = 128
SUBLANES = 8
BF16_SUBLANES = 16
VMEM_LIMIT = 56 * 1024 * 1024


def _params(semantics):
    return pltpu.CompilerParams(dimension_semantics=semantics,
                                vmem_limit_bytes=VMEM_LIMIT)


def _mm(a_bf16, w_f32):
    return jnp.dot(a_bf16, w_f32.astype(BF16), preferred_element_type=F32)


def _rms_scale(x, w, eps):
    ms = jnp.mean(x * x, axis=-1, keepdims=True)
    return (x * lax.rsqrt(ms + eps)) * w


def _row_block_pipeline(n_blocks, matmul, epilogue):
    acc = matmul(0)
    for r in range(n_blocks):
        nxt = matmul(r + 1) if r + 1 < n_blocks else None
        epilogue(r, acc)
        acc = nxt


def _qkv_kernel(x_ref, anw_ref, w_ref, nw_ref, cos_ref, sin_ref, o_ref, xn_ref,
                w_scr, *, rb, n_qk_tiles):
    j = pl.program_id(1)
    tm, tn = o_ref.shape

    @pl.when(j == 0)
    def _():
        for r in range(tm // rb):
            rows = slice(r * rb, (r + 1) * rb)
            xn_ref[rows, :] = _rms_scale(x_ref[rows, :], anw_ref[...], EPS).astype(BF16)

    w_scr[...] = w_ref[...].astype(BF16)
    lane = lax.broadcasted_iota(jnp.int32, (1, LANES), 1)
    lo_half = lane < HEAD_DIM
    first = (lane % HEAD_DIM) < (HEAD_DIM // 2)

    def matmul(r):
        return jnp.dot(xn_ref[r * rb:(r + 1) * rb, :], w_scr[...],
                       preferred_element_type=F32)

    def plain(r, acc):
        o_ref[r * rb:(r + 1) * rb, :] = acc.astype(o_ref.dtype)

    def norm_rope(r, acc):
        rows = slice(r * rb, (r + 1) * rb)
        cos = cos_ref[rows, :]
        sin = sin_ref[rows, :]
        for c in range(tn // LANES):
            cols = slice(c * LANES, (c + 1) * LANES)
            y = acc[:, cols]
            sq = y * y
            s_lo = jnp.sum(jnp.where(lo_half, sq, 0.0), axis=-1, keepdims=True)
            s_hi = jnp.sum(jnp.where(lo_half, 0.0, sq), axis=-1, keepdims=True)
            ms = jnp.where(lo_half, s_lo, s_hi) * (1.0 / HEAD_DIM)
            yn = (y * lax.rsqrt(ms + EPS)) * nw_ref[:, cols]
            rot = jnp.where(first, pltpu.roll(yn, LANES - HEAD_DIM // 2, 1),
                            pltpu.roll(yn, HEAD_DIM // 2, 1))
            o_ref[rows, cols] = (yn * cos + rot * sin).astype(o_ref.dtype)

    @pl.when(j < n_qk_tiles)
    def _():
        _row_block_pipeline(tm // rb, matmul, norm_rope)

    @pl.when(j >= n_qk_tiles)
    def _():
        _row_block_pipeline(tm // rb, matmul, plain)


def _qkv_proj(x, attn_norm_w, w_in, nw, cos, sin, seq, n_out, tm, tn, rb):
    t, d = x.shape
    n_qk_tiles = nw.shape[1] // tn
    tiles_per_seq = seq // tm
    kern = functools.partial(_qkv_kernel, rb=rb, n_qk_tiles=n_qk_tiles)
    return pl.pallas_call(
        kern,
        grid=(t // tm, n_out // tn),
        in_specs=[pl.BlockSpec((tm, d), lambda i, j: (i, 0)),
                  pl.BlockSpec((1, d), lambda i, j: (0, 0)),
                  pl.BlockSpec((d, tn), lambda i, j: (0, j)),
                  pl.BlockSpec((1, tn), lambda i, j: (0, jnp.minimum(j, n_qk_tiles - 1))),
                  pl.BlockSpec((tm, LANES), lambda i, j: (i % tiles_per_seq, 0)),
                  pl.BlockSpec((tm, LANES), lambda i, j: (i % tiles_per_seq, 0))],
        out_specs=[pl.BlockSpec((tm, tn), lambda i, j: (i, j)),
                   pl.BlockSpec((tm, d), lambda i, j: (i, 0))],
        out_shape=[jax.ShapeDtypeStruct((t, n_out), BF16),
                   jax.ShapeDtypeStruct((t, d), BF16)],
        scratch_shapes=[pltpu.VMEM((d, tn), BF16)],
        compiler_params=_params(("arbitrary", "arbitrary")),
        name="qkv_proj",
    )(x, attn_norm_w.reshape(1, d), w_in, nw, cos, sin)


def _conv_kernel(xn_ref, wb_ref, wc_ref, wh_ref, cw_ref, o_ref,
                 wb_scr, wc_scr, wh_scr, u_scr, carry_scr, *, tiles_per_seq, rb):
    i = pl.program_id(0)
    j = pl.program_id(1)
    tm = xn_ref.shape[0]
    wc_scr[...] = wc_ref[...].astype(BF16)
    wh_scr[...] = wh_ref[...].astype(BF16)
    wb_scr[...] = wb_ref[...].astype(BF16)

    @pl.when(i % tiles_per_seq == 0)
    def _():
        u_scr[0:SUBLANES, :] = jnp.zeros((SUBLANES, u_scr.shape[1]), F32)

    @pl.when(i % tiles_per_seq != 0)
    def _():
        u_scr[0:SUBLANES, :] = carry_scr[j]

    def xn_rows(r):
        return xn_ref[r * rb:(r + 1) * rb, :]

    def gate_matmuls(r):
        return (jnp.dot(xn_rows(r), wc_scr[...], preferred_element_type=F32),
                jnp.dot(xn_rows(r), wh_scr[...], preferred_element_type=F32))

    def store_u(r, acc):
        u_scr[SUBLANES + r * rb:SUBLANES + (r + 1) * rb, :] = acc[0] * acc[1]

    _row_block_pipeline(tm // rb, gate_matmuls, store_u)
    carry_scr[j] = u_scr[tm:tm + SUBLANES, :]
    cw = cw_ref[...]

    def b_matmul(r):
        return jnp.dot(xn_rows(r), wb_scr[...], preferred_element_type=F32)

    def conv_out(r, gate_b):
        def shifted(back):
            start = SUBLANES + r * rb - back
            return u_scr[start:start + rb, :]
        y = cw[2:3, :] * shifted(0) + cw[1:2, :] * shifted(1) + cw[0:1, :] * shifted(2)
        o_ref[r * rb:(r + 1) * rb, :] = (gate_b * y).astype(o_ref.dtype)

    _row_block_pipeline(tm // rb, b_matmul, conv_out)


def _conv_proj(xn, w_in, conv_w, col_b, col_c, col_h, n, seq, tm, tn, rb):
    t, d = xn.shape
    jb, jc, jh = col_b // tn, col_c // tn, col_h // tn
    nj = n // tn
    kern = functools.partial(_conv_kernel, tiles_per_seq=seq // tm, rb=rb)
    w_scratch = [pltpu.VMEM((d, tn), BF16)] * 3
    return pl.pallas_call(
        kern,
        grid=(t // tm, nj),
        in_specs=[pl.BlockSpec((tm, d), lambda i, j: (i, 0)),
                  pl.BlockSpec((d, tn), lambda i, j: (0, jb + j)),
                  pl.BlockSpec((d, tn), lambda i, j: (0, jc + j)),
                  pl.BlockSpec((d, tn), lambda i, j: (0, jh + j)),
                  pl.BlockSpec((CONV_K, tn), lambda i, j: (0, j))],
        out_specs=pl.BlockSpec((tm, tn), lambda i, j: (i, j)),
        out_shape=jax.ShapeDtypeStruct((t, n), BF16),
        scratch_shapes=w_scratch + [pltpu.VMEM((tm + SUBLANES, tn), F32),
                                    pltpu.VMEM((nj, SUBLANES, tn), F32)],
        compiler_params=_params(("arbitrary", "arbitrary")),
        name="conv_proj",
    )(xn, w_in, w_in, w_in, conv_w)


def _attn_kernel(q_ref, k_ref, v_ref, lam_ref, sw_ref, o_ref,
                 vt_scr, bias_scr, qq_a, qq_b, m_a, m_b, acc_a, acc_b, *s_bufs,
                 tq, tk, pw, ahead):
    seq = k_ref.shape[1]
    n_q = seq // tq
    n_diag = tq // tk
    n_buf = len(s_bufs)
    n_panels = 2 * tq // pw
    assert n_buf == n_diag and ahead < n_buf
    all_visible = ("full",) * n_panels
    state = ((qq_a, m_a, acc_a), (qq_b, m_b, acc_b))

    for c in range(seq // LANES):
        blk = v_ref[0, c * LANES:(c + 1) * LANES, :].astype(F32)
        vt_scr[0:V_DIM, c * LANES:(c + 1) * LANES] = blk.T.astype(BF16)
    vt_scr[V_DIM:, 0:seq] = jnp.ones((vt_scr.shape[0] - V_DIM, seq), BF16)
    key = lax.broadcasted_iota(jnp.int32, bias_scr.shape, 0)
    qry = lax.broadcasted_iota(jnp.int32, bias_scr.shape, 1)
    bias_scr[...] = jnp.where(key // CHUNK <= qry // CHUNK, 0.0, -1e30)

    lq1, lk1, lq2, lk2 = (lam_ref[r:r + 1, :] for r in range(4))
    lam = (jnp.exp(jnp.sum(lq1 * lk1, axis=-1, keepdims=True))
           - jnp.exp(jnp.sum(lq2 * lk2, axis=-1, keepdims=True)) + LAMBDA_INIT)

    def diag_modes(key_off):
        modes = []
        for c in range(n_panels):
            q_off = (c * pw) % tq
            if key_off + tk <= q_off:
                modes.append("full")
            elif key_off >= q_off + pw:
                modes.append("skip")
            else:
                assert key_off == q_off and tk == pw
                modes.append("tri")
        return tuple(modes)

    def prepare(qi):
        qq_scr, m_scr, acc_scr = state[qi % 2]
        dim = lax.broadcasted_iota(jnp.int32, (V_DIM, 1), 0)
        for c in range(tq // LANES):
            rows = slice(qi * tq + c * LANES, qi * tq + (c + 1) * LANES)
            qt = q_ref[0, rows, :].astype(F32).T
            cols = slice(c * LANES, (c + 1) * LANES)
            qq_scr[:, cols] = jnp.where(dim < HEAD_DIM, qt, 0.0).astype(BF16)
            cols = slice(tq + c * LANES, tq + (c + 1) * LANES)
            qq_scr[:, cols] = jnp.where(dim < HEAD_DIM, 0.0, qt).astype(BF16)
        m_scr[...] = jnp.full(m_scr.shape, -jnp.inf, F32)
        acc_scr[...] = jnp.zeros(acc_scr.shape, F32)

    def scores(qi, t, modes=all_visible):
        qq_scr = state[qi % 2][0]
        dst = s_bufs[t % n_buf]
        k = k_ref[0, t * tk:(t + 1) * tk, :]
        for c in range(n_panels):
            if modes[c] == "skip":
                continue
            cols = slice(c * pw, (c + 1) * pw)
            dst[:, cols] = jnp.dot(k, qq_scr[:, cols], preferred_element_type=F32)

    def update(qi, t, modes=all_visible):
        _, m_scr, acc_scr = state[qi % 2]
        src = s_bufs[t % n_buf]
        vt = vt_scr[:, t * tk:(t + 1) * tk]
        for c in range(n_panels):
            if modes[c] == "skip":
                continue
            cols = slice(c * pw, (c + 1) * pw)
            s = src[:, cols]
            if modes[c] == "tri":
                s = s + bias_scr[...]
            m_prev = m_scr[:, cols]
            m_new = jnp.maximum(m_prev, jnp.max(s, axis=0, keepdims=True))
            alpha = jnp.exp2(m_prev - m_new)
            p = jnp.exp2(s - m_new).astype(BF16)
            m_scr[:, cols] = m_new
            acc_scr[:, cols] = alpha * acc_scr[:, cols] + jnp.dot(
                vt, p, preferred_element_type=F32)

    def finish(qi):
        acc = state[qi % 2][2][:, 0:2 * tq]
        o = acc[0:V_DIM, :] / acc[V_DIM:V_DIM + 1, :]
        a = (o[:, 0:tq] - lam * o[:, tq:]).T
        ms = jnp.mean(a * a, axis=-1, keepdims=True)
        o_ref[0, qi * tq:(qi + 1) * tq, :] = (
            ((a * lax.rsqrt(ms + SUBLN_EPS)) * sw_ref[...])
            * (1.0 - LAMBDA_INIT)).astype(o_ref.dtype)

    prepare(0)
    for u in range(ahead):
        scores(0, u, diag_modes(u * tk))
    for qi in range(n_q):
        n_full = n_diag * qi

        def tile_modes(t):
            return all_visible if t < n_full else diag_modes((t - n_full) * tk)

        for g in range(qi + 1):
            t0 = n_buf * g
            for u in range(ahead, n_buf):
                scores(qi, t0 + u, tile_modes(t0 + u))
            if g == qi and qi + 1 < n_q:
                prepare(qi + 1)
            for u in range(n_buf):
                update(qi, t0 + u, tile_modes(t0 + u))
                if u < ahead:
                    if g < qi:
                        scores(qi, t0 + n_buf + u, tile_modes(t0 + n_buf + u))
                    elif qi + 1 < n_q:
                        scores(qi + 1, u)
        finish(qi)


def _attention(qkv, lam_vecs, subln_w, tq, tk, pw, ahead):
    b, s, w3 = qkv.shape
    w = w3 // 3
    h = w // V_DIM
    kern = functools.partial(_attn_kernel, tq=tq, tk=tk, pw=pw, ahead=ahead)

    def head_block(first):
        return pl.BlockSpec((1, s, V_DIM), lambda bi, hi: (bi, 0, first + hi))

    per_q_tile = ([pltpu.VMEM((V_DIM, 2 * tq), BF16)] * 2
                  + [pltpu.VMEM((1, 2 * tq), F32)] * 2
                  + [pltpu.VMEM((V_DIM + BF16_SUBLANES, 2 * tq + LANES), F32)] * 2)
    return pl.pallas_call(
        kern,
        grid=(b, h),
        in_specs=[head_block(0), head_block(h), head_block(2 * h),
                  pl.BlockSpec((4, HEAD_DIM), lambda bi, hi: (0, 0)),
                  pl.BlockSpec((1, V_DIM), lambda bi, hi: (0, 0))],
        out_specs=head_block(0),
        out_shape=jax.ShapeDtypeStruct((b, s, w), BF16),
        scratch_shapes=[pltpu.VMEM((V_DIM + BF16_SUBLANES, s + LANES), BF16),
                        pltpu.VMEM((tk, pw), F32)]
        + per_q_tile
        + [pltpu.VMEM((tk, 2 * tq + LANES), F32)] * (tq // tk),
        compiler_params=_params(("parallel", "parallel")),
        name="diff_attention",
    )(qkv, qkv, qkv, lam_vecs, subln_w)


def _out_kernel(a_ref, c_ref, wa_ref, wc_ref, x_ref, fnw_ref, o_ref, fn_ref, h_scr,
                *, rb):
    j = pl.program_id(1)
    n_j, tm, tn = h_scr.shape
    h = x_ref[...] + _mm(a_ref[...], wa_ref[...]) + _mm(c_ref[...], wc_ref[...])
    o_ref[...] = h
    h_scr[j] = h

    @pl.when(j == n_j - 1)
    def _():
        for r in range(tm // rb):
            rows = slice(r * rb, (r + 1) * rb)
            ssq = sum(jnp.sum(h_scr[jj, rows, :] * h_scr[jj, rows, :], axis=-1, keepdims=True)
                      for jj in range(n_j))
            scale = lax.rsqrt(ssq * (1.0 / (n_j * tn)) + EPS)
            for jj in range(n_j):
                cols = slice(jj * tn, (jj + 1) * tn)
                fn_ref[rows, cols] = ((h_scr[jj, rows, :] * scale)
                                      * fnw_ref[:, cols]).astype(fn_ref.dtype)


def _out_proj(attn, conv, w_out, x, ffn_norm_w, tm, tn, rb):
    t, ka = attn.shape
    kc = conv.shape[1]
    n = w_out.shape[1]
    assert ka == kc
    return pl.pallas_call(
        functools.partial(_out_kernel, rb=rb),
        grid=(t // tm, n // tn),
        in_specs=[pl.BlockSpec((tm, ka), lambda i, j: (i, 0)),
                  pl.BlockSpec((tm, kc), lambda i, j: (i, 0)),
                  pl.BlockSpec((ka, tn), lambda i, j: (0, j)),
                  pl.BlockSpec((kc, tn), lambda i, j: (1, j)),
                  pl.BlockSpec((tm, tn), lambda i, j: (i, j)),
                  pl.BlockSpec((1, n), lambda i, j: (0, 0))],
        out_specs=[pl.BlockSpec((tm, tn), lambda i, j: (i, j)),
                   pl.BlockSpec((tm, n), lambda i, j: (i, 0))],
        out_shape=[jax.ShapeDtypeStruct((t, n), F32),
                   jax.ShapeDtypeStruct((t, n), BF16)],
        scratch_shapes=[pltpu.VMEM((n // tn, tm, tn), F32)],
        compiler_params=_params(("arbitrary", "arbitrary")),
        name="out_proj",
    )(attn, conv, w_out, w_out, x, ffn_norm_w.reshape(1, n))


def _gate_up_kernel(x_ref, wg_ref, wu_ref, o_ref):
    x = x_ref[...]
    g = _mm(x, wg_ref[...])
    u = _mm(x, wu_ref[...])
    o_ref[...] = ((g * jax.nn.sigmoid(g)) * u).astype(o_ref.dtype)


def _gate_up(fn, w_gate, w_up, tm, tn):
    t, d = fn.shape
    f = w_gate.shape[1]
    return pl.pallas_call(
        _gate_up_kernel,
        grid=(t // tm, f // tn),
        in_specs=[pl.BlockSpec((tm, d), lambda i, j: (i, 0)),
                  pl.BlockSpec((d, tn), lambda i, j: (0, j)),
                  pl.BlockSpec((d, tn), lambda i, j: (0, j))],
        out_specs=pl.BlockSpec((tm, tn), lambda i, j: (i, j)),
        out_shape=jax.ShapeDtypeStruct((t, f), BF16),
        compiler_params=_params(("parallel", "parallel")),
        name="gate_up",
    )(fn, w_gate, w_up)


def _down_kernel(h_ref, w_ref, r_ref, o_ref):
    o_ref[...] = r_ref[...] + _mm(h_ref[...], w_ref[...])


def _down(h, w_down, resid, tm, tn):
    t, f = h.shape
    n = w_down.shape[1]
    return pl.pallas_call(
        _down_kernel,
        grid=(t // tm, n // tn),
        in_specs=[pl.BlockSpec((tm, f), lambda i, j: (i, 0)),
                  pl.BlockSpec((f, tn), lambda i, j: (0, j)),
                  pl.BlockSpec((tm, tn), lambda i, j: (i, j))],
        out_specs=pl.BlockSpec((tm, tn), lambda i, j: (i, j)),
        out_shape=jax.ShapeDtypeStruct((t, n), F32),
        compiler_params=_params(("parallel", "parallel")),
        name="down_proj",
    )(h, w_down, resid)


TILES = dict(
    proj_tm=1024, proj_tn=1024, proj_rb=256,
    conv_tm=1024, conv_tn=512, conv_rb=256,
    attn_tq=1024, attn_tk=256, attn_pw=256, attn_ahead=3,
    out_tm=1024, out_tn=512, out_rb=256,
    gu_tm=1024, gu_tn=512,
    down_tm=1024, down_tn=256,
)


def _rope_tables(seq):
    pos = jnp.arange(seq, dtype=F32)
    inv_freq = ROPE_THETA ** (-jnp.arange(0, HEAD_DIM, 2, dtype=F32) / HEAD_DIM)
    ang = pos[:, None] * inv_freq[None, :]
    cos, sin = jnp.cos(ang), jnp.sin(ang)
    reps = LANES // (HEAD_DIM // 2)
    cos_l = jnp.tile(cos, (1, reps))
    sin_l = jnp.tile(jnp.concatenate([-sin, sin], axis=1), (1, reps // 2))
    return cos_l, sin_l


def kernel(x, attn_norm_w, w_in, q_norm_w, k_norm_w, lambda_q1, lambda_k1,
           lambda_q2, lambda_k2, subln_w, conv_w, w_out, ffn_norm_w, w_gate,
           w_up, w_down):
    b, s, d = x.shape
    t = b * s
    attn_w = N_HEADS * V_DIM
    qk_cols = N_HEADS * 2 * HEAD_DIM
    conv_width = d - attn_w
    col_v = 2 * qk_cols
    col_b = col_v + attn_w
    col_c = col_b + conv_width
    col_h = col_c + conv_width
    tl = TILES
    assert w_in.shape[0] == 1, "single-layer block"

    x2 = x.reshape(t, d)
    cos_l, sin_l = _rope_tables(s)
    scale = math.log2(math.e) / math.sqrt(HEAD_DIM)
    nw = jnp.concatenate([jnp.tile(q_norm_w[0] * scale, qk_cols // HEAD_DIM),
                          jnp.tile(k_norm_w[0], qk_cols // HEAD_DIM)]).reshape(1, -1)
    lam_vecs = jnp.concatenate([lambda_q1, lambda_k1, lambda_q2, lambda_k2], axis=0)

    qkv, xn = _qkv_proj(x2, attn_norm_w[0], w_in[0], nw, cos_l, sin_l, s, col_b,
                        tl["proj_tm"], tl["proj_tn"], tl["proj_rb"])
    conv = _conv_proj(xn, w_in[0], conv_w[0], col_b, col_c, col_h, conv_width, s,
                      tl["conv_tm"], tl["conv_tn"], tl["conv_rb"])

    attn = _attention(qkv.reshape(b, s, col_b), lam_vecs, subln_w[0].reshape(1, V_DIM),
                      tl["attn_tq"], tl["attn_tk"], tl["attn_pw"], tl["attn_ahead"])

    h1, fn = _out_proj(attn.reshape(t, attn_w), conv, w_out[0], x2, ffn_norm_w[0],
                       tl["out_tm"], tl["out_tn"], tl["out_rb"])
    hid = _gate_up(fn, w_gate[0], w_up[0], tl["gu_tm"], tl["gu_tn"])
    out = _down(hid, w_down[0], h1, tl["down_tm"], tl["down_tn"])
    return out.reshape(b, s, d)
```

```python
import functools
import math

import jax
import jax.numpy as jnp
from jax import lax
from jax.experimental import pallas as pl
from jax.experimental.pallas import tpu as pltpu

F32 = jnp.float32
BF16 = jnp.bfloat16

CHUNK = 64
HEAD_DIM = 64
V_DIM = 2 * HEAD_DIM
N_HEADS = 8
CONV_K = 3
ROPE_THETA = 10000.0
EPS = 1e-6
SUBLN_EPS = 1e-5
LAMBDA_INIT = 0.8 - 0.6 * math.exp(-0.3 * 0)

LANES = 128
SUBLANES = 8
BF16_SUBLANES = 16
VMEM_LIMIT = 56 * 1024 * 1024


def _params(semantics):
    return pltpu.CompilerParams(dimension_semantics=semantics,
                                vmem_limit_bytes=VMEM_LIMIT)


def _mm(a_bf16, w_f32):
    return jnp.dot(a_bf16, w_f32.astype(BF16), preferred_element_type=F32)


def _rms_scale(x, w, eps):
    ms = jnp.mean(x * x, axis=-1, keepdims=True)
    return (x * lax.rsqrt(ms + eps)) * w


def _row_block_pipeline(n_blocks, matmul, epilogue):
    acc = matmul(0)
    for r in range(n_blocks):
        nxt = matmul(r + 1) if r + 1 < n_blocks else None
        epilogue(r, acc)
        acc = nxt


def _qkv_kernel(x_ref, anw_ref, w_ref, nw_ref, cos_ref, sin_ref, o_ref, xn_ref,
                w_scr, *, rb, n_qk_tiles):
    j = pl.program_id(1)
    tm, tn = o_ref.shape

    @pl.when(j == 0)
    def _():
        for r in range(tm // rb):
            rows = slice(r * rb, (r + 1) * rb)
            xn_ref[rows, :] = _rms_scale(x_ref[rows, :], anw_ref[...], EPS).astype(BF16)

    w_scr[...] = w_ref[...].astype(BF16)
    lane = lax.broadcasted_iota(jnp.int32, (1, LANES), 1)
    lo_half = lane < HEAD_DIM
    first = (lane % HEAD_DIM) < (HEAD_DIM // 2)

    def matmul(r):
        return jnp.dot(xn_ref[r * rb:(r + 1) * rb, :], w_scr[...],
                       preferred_element_type=F32)

    def plain(r, acc):
        o_ref[r * rb:(r + 1) * rb, :] = acc.astype(o_ref.dtype)

    def norm_rope(r, acc):
        rows = slice(r * rb, (r + 1) * rb)
        cos = cos_ref[rows, :]
        sin = sin_ref[rows, :]
        for c in range(tn // LANES):
            cols = slice(c * LANES, (c + 1) * LANES)
            y = acc[:, cols]
            sq = y * y
            s_lo = jnp.sum(jnp.where(lo_half, sq, 0.0), axis=-1, keepdims=True)
            s_hi = jnp.sum(jnp.where(lo_half, 0.0, sq), axis=-1, keepdims=True)
            ms = jnp.where(lo_half, s_lo, s_hi) * (1.0 / HEAD_DIM)
            yn = (y * lax.rsqrt(ms + EPS)) * nw_ref[:, cols]
            rot = jnp.where(first, pltpu.roll(yn, LANES - HEAD_DIM // 2, 1),
                            pltpu.roll(yn, HEAD_DIM // 2, 1))
            o_ref[rows, cols] = (yn * cos + rot * sin).astype(o_ref.dtype)

    @pl.when(j < n_qk_tiles)
    def _():
        _row_block_pipeline(tm // rb, matmul, norm_rope)

    @pl.when(j >= n_qk_tiles)
    def _():
        _row_block_pipeline(tm // rb, matmul, plain)


def _qkv_proj(x, attn_norm_w, w_in, nw, cos, sin, seq, n_out, tm, tn, rb):
    t, d = x.shape
    n_qk_tiles = nw.shape[1] // tn
    tiles_per_seq = seq // tm
    kern = functools.partial(_qkv_kernel, rb=rb, n_qk_tiles=n_qk_tiles)
    return pl.pallas_call(
        kern,
        grid=(t // tm, n_out // tn),
        in_specs=[pl.BlockSpec((tm, d), lambda i, j: (i, 0)),
                  pl.BlockSpec((1, d), lambda i, j: (0, 0)),
                  pl.BlockSpec((d, tn), lambda i, j: (0, j)),
                  pl.BlockSpec((1, tn), lambda i, j: (0, jnp.minimum(j, n_qk_tiles - 1))),
                  pl.BlockSpec((tm, LANES), lambda i, j: (i % tiles_per_seq, 0)),
                  pl.BlockSpec((tm, LANES), lambda i, j: (i % tiles_per_seq, 0))],
        out_specs=[pl.BlockSpec((tm, tn), lambda i, j: (i, j)),
                   pl.BlockSpec((tm, d), lambda i, j: (i, 0))],
        out_shape=[jax.ShapeDtypeStruct((t, n_out), BF16),
                   jax.ShapeDtypeStruct((t, d), BF16)],
        scratch_shapes=[pltpu.VMEM((d, tn), BF16)],
        compiler_params=_params(("arbitrary", "arbitrary")),
        name="qkv_proj",
    )(x, attn_norm_w.reshape(1, d), w_in, nw, cos, sin)


def _conv_kernel(xn_ref, wb_ref, wc_ref, wh_ref, cw_ref, o_ref,
                 wb_scr, wc_scr, wh_scr, u_scr, carry_scr, *, tiles_per_seq, rb):
    i = pl.program_id(0)
    j = pl.program_id(1)
    tm = xn_ref.shape[0]
    wc_scr[...] = wc_ref[...].astype(BF16)
    wh_scr[...] = wh_ref[...].astype(BF16)
    wb_scr[...] = wb_ref[...].astype(BF16)

    @pl.when(i % tiles_per_seq == 0)
    def _():
        u_scr[0:SUBLANES, :] = jnp.zeros((SUBLANES, u_scr.shape[1]), F32)

    @pl.when(i % tiles_per_seq != 0)
    def _():
        u_scr[0:SUBLANES, :] = carry_scr[j]

    def xn_rows(r):
        return xn_ref[r * rb:(r + 1) * rb, :]

    def gate_matmuls(r):
        return (jnp.dot(xn_rows(r), wc_scr[...], preferred_element_type=F32),
                jnp.dot(xn_rows(r), wh_scr[...], preferred_element_type=F32))

    def store_u(r, acc):
        u_scr[SUBLANES + r * rb:SUBLANES + (r + 1) * rb, :] = acc[0] * acc[1]

    _row_block_pipeline(tm // rb, gate_matmuls, store_u)
    carry_scr[j] = u_scr[tm:tm + SUBLANES, :]
    cw = cw_ref[...]

    def b_matmul(r):
        return jnp.dot(xn_rows(r), wb_scr[...], preferred_element_type=F32)

    def conv_out(r, gate_b):
        def shifted(back):
            start = SUBLANES + r * rb - back
            return u_scr[start:start + rb, :]
        y = cw[2:3, :] * shifted(0) + cw[1:2, :] * shifted(1) + cw[0:1, :] * shifted(2)
        o_ref[r * rb:(r + 1) * rb, :] = (gate_b * y).astype(o_ref.dtype)

    _row_block_pipeline(tm // rb, b_matmul, conv_out)


def _conv_proj(xn, w_in, conv_w, col_b, col_c, col_h, n, seq, tm, tn, rb):
    t, d = xn.shape
    jb, jc, jh = col_b // tn, col_c // tn, col_h // tn
    nj = n // tn
    kern = functools.partial(_conv_kernel, tiles_per_seq=seq // tm, rb=rb)
    w_scratch = [pltpu.VMEM((d, tn), BF16)] * 3
    return pl.pallas_call(
        kern,
        grid=(t // tm, nj),
        in_specs=[pl.BlockSpec((tm, d), lambda i, j: (i, 0)),
                  pl.BlockSpec((d, tn), lambda i, j: (0, jb + j)),
                  pl.BlockSpec((d, tn), lambda i, j: (0, jc + j)),
                  pl.BlockSpec((d, tn), lambda i, j: (0, jh + j)),
                  pl.BlockSpec((CONV_K, tn), lambda i, j: (0, j))],
        out_specs=pl.BlockSpec((tm, tn), lambda i, j: (i, j)),
        out_shape=jax.ShapeDtypeStruct((t, n), BF16),
        scratch_shapes=w_scratch + [pltpu.VMEM((tm + SUBLANES, tn), F32),
                                    pltpu.VMEM((nj, SUBLANES, tn), F32)],
        compiler_params=_params(("arbitrary", "arbitrary")),
        name="conv_proj",
    )(xn, w_in, w_in, w_in, conv_w)


def _attn_kernel(q_ref, k_ref, v_ref, lam_ref, sw_ref, o_ref,
                 vt_scr, bias_scr, qq_a, qq_b, m_a, m_b, acc_a, acc_b, *s_bufs,
                 tq, tk, pw, ahead):
    seq = k_ref.shape[1]
    n_q = seq // tq
    n_diag = tq // tk
    n_buf = len(s_bufs)
    n_panels = 2 * tq // pw
    assert n_buf == n_diag and ahead < n_buf
    all_visible = ("full",) * n_panels
    state = ((qq_a, m_a, acc_a), (qq_b, m_b, acc_b))

    for c in range(seq // LANES):
        blk = v_ref[0, c * LANES:(c + 1) * LANES, :].astype(F32)
        vt_scr[0:V_DIM, c * LANES:(c + 1) * LANES] = blk.T.astype(BF16)
    vt_scr[V_DIM:, 0:seq] = jnp.ones((vt_scr.shape[0] - V_DIM, seq), BF16)
    key = lax.broadcasted_iota(jnp.int32, bias_scr.shape, 0)
    qry = lax.broadcasted_iota(jnp.int32, bias_scr.shape, 1)
    bias_scr[...] = jnp.where(key // CHUNK <= qry // CHUNK, 0.0, -1e30)

    lq1, lk1, lq2, lk2 = (lam_ref[r:r + 1, :] for r in range(4))
    lam = (jnp.exp(jnp.sum(lq1 * lk1, axis=-1, keepdims=True))
           - jnp.exp(jnp.sum(lq2 * lk2, axis=-1, keepdims=True)) + LAMBDA_INIT)

    def diag_modes(key_off):
        modes = []
        for c in range(n_panels):
            q_off = (c * pw) % tq
            if key_off + tk <= q_off:
                modes.append("full")
            elif key_off >= q_off + pw:
                modes.append("skip")
            else:
                assert key_off == q_off and tk == pw
                modes.append("tri")
        return tuple(modes)

    def prepare(qi):
        qq_scr, m_scr, acc_scr = state[qi % 2]
        dim = lax.broadcasted_iota(jnp.int32, (V_DIM, 1), 0)
        for c in range(tq // LANES):
            rows = slice(qi * tq + c * LANES, qi * tq + (c + 1) * LANES)
            qt = q_ref[0, rows, :].astype(F32).T
            cols = slice(c * LANES, (c + 1) * LANES)
            qq_scr[:, cols] = jnp.where(dim < HEAD_DIM, qt, 0.0).astype(BF16)
            cols = slice(tq + c * LANES, tq + (c + 1) * LANES)
            qq_scr[:, cols] = jnp.where(dim < HEAD_DIM, 0.0, qt).astype(BF16)
        m_scr[...] = jnp.full(m_scr.shape, -jnp.inf, F32)
        acc_scr[...] = jnp.zeros(acc_scr.shape, F32)

    stages = []
    for qi in range(n_q):
        n_full = n_diag * qi
        for t in range(n_full + n_diag):
            modes = all_visible if t < n_full else diag_modes((t - n_full) * tk)
            stages.append((qi, t, modes))

    def scores(g, c):
        qi, t, modes = stages[g]
        if modes[c] == "skip":
            return
        cols = slice(c * pw, (c + 1) * pw)
        s_bufs[g % n_buf][:, cols] = jnp.dot(
            k_ref[0, t * tk:(t + 1) * tk, :], state[qi % 2][0][:, cols],
            preferred_element_type=F32)

    def update(g, c):
        qi, t, modes = stages[g]
        if modes[c] == "skip":
            return
        _, m_scr, acc_scr = state[qi % 2]
        cols = slice(c * pw, (c + 1) * pw)
        s = s_bufs[g % n_buf][:, cols]
        if modes[c] == "tri":
            s = s + bias_scr[...]
        m_prev = m_scr[:, cols]
        m_new = jnp.maximum(m_prev, jnp.max(s, axis=0, keepdims=True))
        alpha = jnp.exp2(m_prev - m_new)
        p = jnp.exp2(s - m_new).astype(BF16)
        m_scr[:, cols] = m_new
        acc_scr[:, cols] = alpha * acc_scr[:, cols] + jnp.dot(
            vt_scr[:, t * tk:(t + 1) * tk], p, preferred_element_type=F32)

    def finish(qi):
        acc = state[qi % 2][2][:, 0:2 * tq]
        o = acc[0:V_DIM, :] / acc[V_DIM:V_DIM + 1, :]
        a = (o[:, 0:tq] - lam * o[:, tq:]).T
        ms = jnp.mean(a * a, axis=-1, keepdims=True)
        o_ref[0, qi * tq:(qi + 1) * tq, :] = (
            ((a * lax.rsqrt(ms + SUBLN_EPS)) * sw_ref[...])
            * (1.0 - LAMBDA_INIT)).astype(o_ref.dtype)

    def start(g):
        qi, t, _ = stages[g]
        if t == 0:
            prepare(qi)

    for g in range(ahead):
        start(g)
        for c in range(n_panels):
            scores(g, c)
    for g, (qi, t, _) in enumerate(stages):
        if g + ahead < len(stages):
            start(g + ahead)
        for c in range(n_panels):
            update(g, c)
            if g + ahead < len(stages):
                scores(g + ahead, c)
        if t == n_diag * (qi + 1) - 1:
            finish(qi)


def _attention(qkv, lam_vecs, subln_w, tq, tk, pw, ahead):
    b, s, w3 = qkv.shape
    w = w3 // 3
    h = w // V_DIM
    kern = functools.partial(_attn_kernel, tq=tq, tk=tk, pw=pw, ahead=ahead)

    def head_block(first):
        return pl.BlockSpec((1, s, V_DIM), lambda bi, hi: (bi, 0, first + hi))

    per_q_tile = ([pltpu.VMEM((V_DIM, 2 * tq), BF16)] * 2
                  + [pltpu.VMEM((1, 2 * tq), F32)] * 2
                  + [pltpu.VMEM((V_DIM + BF16_SUBLANES, 2 * tq + LANES), F32)] * 2)
    return pl.pallas_call(
        kern,
        grid=(b, h),
        in_specs=[head_block(0), head_block(h), head_block(2 * h),
                  pl.BlockSpec((4, HEAD_DIM), lambda bi, hi: (0, 0)),
                  pl.BlockSpec((1, V_DIM), lambda bi, hi: (0, 0))],
        out_specs=head_block(0),
        out_shape=jax.ShapeDtypeStruct((b, s, w), BF16),
        scratch_shapes=[pltpu.VMEM((V_DIM + BF16_SUBLANES, s + LANES), BF16),
                        pltpu.VMEM((tk, pw), F32)]
        + per_q_tile
        + [pltpu.VMEM((tk, 2 * tq + LANES), F32)] * (tq // tk),
        compiler_params=_params(("parallel", "parallel")),
        name="diff_attention",
    )(qkv, qkv, qkv, lam_vecs, subln_w)


def _out_kernel(a_ref, c_ref, wa_ref, wc_ref, x_ref, fnw_ref, o_ref, fn_ref, h_scr,
                *, rb):
    j = pl.program_id(1)
    n_j, tm, tn = h_scr.shape
    h = x_ref[...] + _mm(a_ref[...], wa_ref[...]) + _mm(c_ref[...], wc_ref[...])
    o_ref[...] = h
    h_scr[j] = h

    @pl.when(j == n_j - 1)
    def _():
        for r in range(tm // rb):
            rows = slice(r * rb, (r + 1) * rb)
            ssq = sum(jnp.sum(h_scr[jj, rows, :] * h_scr[jj, rows, :], axis=-1, keepdims=True)
                      for jj in range(n_j))
            scale = lax.rsqrt(ssq * (1.0 / (n_j * tn)) + EPS)
            for jj in range(n_j):
                cols = slice(jj * tn, (jj + 1) * tn)
                fn_ref[rows, cols] = ((h_scr[jj, rows, :] * scale)
                                      * fnw_ref[:, cols]).astype(fn_ref.dtype)


def _out_proj(attn, conv, w_out, x, ffn_norm_w, tm, tn, rb):
    t, ka = attn.shape
    kc = conv.shape[1]
    n = w_out.shape[1]
    assert ka == kc
    return pl.pallas_call(
        functools.partial(_out_kernel, rb=rb),
        grid=(t // tm, n // tn),
        in_specs=[pl.BlockSpec((tm, ka), lambda i, j: (i, 0)),
                  pl.BlockSpec((tm, kc), lambda i, j: (i, 0)),
                  pl.BlockSpec((ka, tn), lambda i, j: (0, j)),
                  pl.BlockSpec((kc, tn), lambda i, j: (1, j)),
                  pl.BlockSpec((tm, tn), lambda i, j: (i, j)),
                  pl.BlockSpec((1, n), lambda i, j: (0, 0))],
        out_specs=[pl.BlockSpec((tm, tn), lambda i, j: (i, j)),
                   pl.BlockSpec((tm, n), lambda i, j: (i, 0))],
        out_shape=[jax.ShapeDtypeStruct((t, n), F32),
                   jax.ShapeDtypeStruct((t, n), BF16)],
        scratch_shapes=[pltpu.VMEM((n // tn, tm, tn), F32)],
        compiler_params=_params(("arbitrary", "arbitrary")),
        name="out_proj",
    )(attn, conv, w_out, w_out, x, ffn_norm_w.reshape(1, n))


def _gate_up_kernel(x_ref, wg_ref, wu_ref, o_ref):
    x = x_ref[...]
    g = _mm(x, wg_ref[...])
    u = _mm(x, wu_ref[...])
    o_ref[...] = ((g * jax.nn.sigmoid(g)) * u).astype(o_ref.dtype)


def _gate_up(fn, w_gate, w_up, tm, tn):
    t, d = fn.shape
    f = w_gate.shape[1]
    return pl.pallas_call(
        _gate_up_kernel,
        grid=(t // tm, f // tn),
        in_specs=[pl.BlockSpec((tm, d), lambda i, j: (i, 0)),
                  pl.BlockSpec((d, tn), lambda i, j: (0, j)),
                  pl.BlockSpec((d, tn), lambda i, j: (0, j))],
        out_specs=pl.BlockSpec((tm, tn), lambda i, j: (i, j)),
        out_shape=jax.ShapeDtypeStruct((t, f), BF16),
        compiler_params=_params(("parallel", "parallel")),
        name="gate_up",
    )(fn, w_gate, w_up)


def _down_kernel(h_ref, w_ref, r_ref, o_ref):
    o_ref[...] = r_ref[...] + _mm(h_ref[...], w_ref[...])


def _down(h, w_down, resid, tm, tn):
    t, f = h.shape
    n = w_down.shape[1]
    return pl.pallas_call(
        _down_kernel,
        grid=(t // tm, n // tn),
        in_specs=[pl.BlockSpec((tm, f), lambda i, j: (i, 0)),
                  pl.BlockSpec((f, tn), lambda i, j: (0, j)),
                  pl.BlockSpec((tm, tn), lambda i, j: (i, j))],
        out_specs=pl.BlockSpec((tm, tn), lambda i, j: (i, j)),
        out_shape=jax.ShapeDtypeStruct((t, n), F32),
        compiler_params=_params(("parallel", "parallel")),
        name="down_proj",
    )(h, w_down, resid)


TILES = dict(
    proj_tm=1024, proj_tn=1024, proj_rb=256,
    conv_tm=1024, conv_tn=512, conv_rb=256,
    attn_tq=1024, attn_tk=256, attn_pw=256, attn_ahead=3,
    out_tm=1024, out_tn=512, out_rb=256,
    gu_tm=1024, gu_tn=512,
    down_tm=1024, down_tn=256,
)


def _rope_tables(seq):
    pos = jnp.arange(seq, dtype=F32)
    inv_freq = ROPE_THETA ** (-jnp.arange(0, HEAD_DIM, 2, dtype=F32) / HEAD_DIM)
    ang = pos[:, None] * inv_freq[None, :]
    cos, sin = jnp.cos(ang), jnp.sin(ang)
    reps = LANES // (HEAD_DIM // 2)
    cos_l = jnp.tile(cos, (1, reps))
    sin_l = jnp.tile(jnp.concatenate([-sin, sin], axis=1), (1, reps // 2))
    return cos_l, sin_l


def kernel(x, attn_norm_w, w_in, q_norm_w, k_norm_w, lambda_q1, lambda_k1,
           lambda_q2, lambda_k2, subln_w, conv_w, w_out, ffn_norm_w, w_gate,
           w_up, w_down):
    b, s, d = x.shape
    t = b * s
    attn_w = N_HEADS * V_DIM
    qk_cols = N_HEADS * 2 * HEAD_DIM
    conv_width = d - attn_w
    col_v = 2 * qk_cols
    col_b = col_v + attn_w
    col_c = col_b + conv_width
    col_h = col_c + conv_width
    tl = TILES
    assert w_in.shape[0] == 1, "single-layer block"

    x2 = x.reshape(t, d)
    cos_l, sin_l = _rope_tables(s)
    scale = math.log2(math.e) / math.sqrt(HEAD_DIM)
    nw = jnp.concatenate([jnp.tile(q_norm_w[0] * scale, qk_cols // HEAD_DIM),
                          jnp.tile(k_norm_w[0], qk_cols // HEAD_DIM)]).reshape(1, -1)
    lam_vecs = jnp.concatenate([lambda_q1, lambda_k1, lambda_q2, lambda_k2], axis=0)

    qkv, xn = _qkv_proj(x2, attn_norm_w[0], w_in[0], nw, cos_l, sin_l, s, col_b,
                        tl["proj_tm"], tl["proj_tn"], tl["proj_rb"])
    conv = _conv_proj(xn, w_in[0], conv_w[0], col_b, col_c, col_h, conv_width, s,
                      tl["conv_tm"], tl["conv_tn"], tl["conv_rb"])

    attn = _attention(qkv.reshape(b, s, col_b), lam_vecs, subln_w[0].reshape(1, V_DIM),
                      tl["attn_tq"], tl["attn_tk"], tl["attn_pw"], tl["attn_ahead"])

    h1, fn = _out_proj(attn.reshape(t, attn_w), conv, w_out[0], x2, ffn_norm_w[0],
                       tl["out_tm"], tl["out_tn"], tl["out_rb"])
    hid = _gate_up(fn, w_gate[0], w_up[0], tl["gu_tm"], tl["gu_tn"])
    out = _down(hid, w_down[0], h1, tl["down_tm"], tl["down_tn"])
    return out.reshape(b, s, d)
```

```python
import functools
import math

import jax
import jax.numpy as jnp
from jax import lax
from jax.experimental import pallas as pl
from jax.experimental.pallas import tpu as pltpu

F32 = jnp.float32
BF16 = jnp.bfloat16

CHUNK = 64
HEAD_DIM = 64
V_DIM = 2 * HEAD_DIM
N_HEADS = 8
CONV_K = 3
ROPE_THETA = 10000.0
EPS = 1e-6
SUBLN_EPS = 1e-5
LAMBDA_INIT = 0.8 - 0.6 * math.exp(-0.3 * 0)

LANES = 128
SUBLANES = 8
BF16_SUBLANES = 16
VMEM_LIMIT = 56 * 1024 * 1024


def _params(semantics):
    return pltpu.CompilerParams(dimension_semantics=semantics,
                                vmem_limit_bytes=VMEM_LIMIT)


def _mm(a_bf16, w_f32):
    return jnp.dot(a_bf16, w_f32.astype(BF16), preferred_element_type=F32)


def _rms_scale(x, w, eps):
    ms = jnp.mean(x * x, axis=-1, keepdims=True)
    return (x * lax.rsqrt(ms + eps)) * w


def _row_block_pipeline(n_blocks, matmul, epilogue):
    acc = matmul(0)
    for r in range(n_blocks):
        nxt = matmul(r + 1) if r + 1 < n_blocks else None
        epilogue(r, acc)
        acc = nxt


def _qkv_kernel(x_ref, anw_ref, w_ref, nw_ref, cos_ref, sin_ref, o_ref, xn_ref,
                w_scr, *, rb, n_qk_tiles):
    j = pl.program_id(1)
    tm, tn = o_ref.shape

    @pl.when(j == 0)
    def _():
        for r in range(tm // rb):
            rows = slice(r * rb, (r + 1) * rb)
            xn_ref[rows, :] = _rms_scale(x_ref[rows, :], anw_ref[...], EPS).astype(BF16)

    w_scr[...] = w_ref[...].astype(BF16)
    lane = lax.broadcasted_iota(jnp.int32, (1, LANES), 1)
    lo_half = lane < HEAD_DIM
    first = (lane % HEAD_DIM) < (HEAD_DIM // 2)

    def matmul(r):
        return jnp.dot(xn_ref[r * rb:(r + 1) * rb, :], w_scr[...],
                       preferred_element_type=F32)

    def plain(r, acc):
        o_ref[r * rb:(r + 1) * rb, :] = acc.astype(o_ref.dtype)

    def norm_rope(r, acc):
        rows = slice(r * rb, (r + 1) * rb)
        cos = cos_ref[rows, :]
        sin = sin_ref[rows, :]
        for c in range(tn // LANES):
            cols = slice(c * LANES, (c + 1) * LANES)
            y = acc[:, cols]
            sq = y * y
            s_lo = jnp.sum(jnp.where(lo_half, sq, 0.0), axis=-1, keepdims=True)
            s_hi = jnp.sum(jnp.where(lo_half, 0.0, sq), axis=-1, keepdims=True)
            ms = jnp.where(lo_half, s_lo, s_hi) * (1.0 / HEAD_DIM)
            yn = (y * lax.rsqrt(ms + EPS)) * nw_ref[:, cols]
            rot = jnp.where(first, pltpu.roll(yn, LANES - HEAD_DIM // 2, 1),
                            pltpu.roll(yn, HEAD_DIM // 2, 1))
            o_ref[rows, cols] = (yn * cos + rot * sin).astype(o_ref.dtype)

    @pl.when(j < n_qk_tiles)
    def _():
        _row_block_pipeline(tm // rb, matmul, norm_rope)

    @pl.when(j >= n_qk_tiles)
    def _():
        _row_block_pipeline(tm // rb, matmul, plain)


def _qkv_proj(x, attn_norm_w, w_in, nw, cos, sin, seq, n_out, tm, tn, rb):
    t, d = x.shape
    n_qk_tiles = nw.shape[1] // tn
    tiles_per_seq = seq // tm
    kern = functools.partial(_qkv_kernel, rb=rb, n_qk_tiles=n_qk_tiles)
    return pl.pallas_call(
        kern,
        grid=(t // tm, n_out // tn),
        in_specs=[pl.BlockSpec((tm, d), lambda i, j: (i, 0)),
                  pl.BlockSpec((1, d), lambda i, j: (0, 0)),
                  pl.BlockSpec((d, tn), lambda i, j: (0, j)),
                  pl.BlockSpec((1, tn), lambda i, j: (0, jnp.minimum(j, n_qk_tiles - 1))),
                  pl.BlockSpec((tm, LANES), lambda i, j: (i % tiles_per_seq, 0)),
                  pl.BlockSpec((tm, LANES), lambda i, j: (i % tiles_per_seq, 0))],
        out_specs=[pl.BlockSpec((tm, tn), lambda i, j: (i, j)),
                   pl.BlockSpec((tm, d), lambda i, j: (i, 0))],
        out_shape=[jax.ShapeDtypeStruct((t, n_out), BF16),
                   jax.ShapeDtypeStruct((t, d), BF16)],
        scratch_shapes=[pltpu.VMEM((d, tn), BF16)],
        compiler_params=_params(("arbitrary", "arbitrary")),
        name="qkv_proj",
    )(x, attn_norm_w.reshape(1, d), w_in, nw, cos, sin)


def _conv_kernel(xn_ref, wb_ref, wc_ref, wh_ref, cw_ref, o_ref,
                 wb_scr, wc_scr, wh_scr, u_scr, carry_scr, *, tiles_per_seq, rb):
    i = pl.program_id(0)
    j = pl.program_id(1)
    tm = xn_ref.shape[0]
    wc_scr[...] = wc_ref[...].astype(BF16)
    wh_scr[...] = wh_ref[...].astype(BF16)
    wb_scr[...] = wb_ref[...].astype(BF16)

    @pl.when(i % tiles_per_seq == 0)
    def _():
        u_scr[0:SUBLANES, :] = jnp.zeros((SUBLANES, u_scr.shape[1]), F32)

    @pl.when(i % tiles_per_seq != 0)
    def _():
        u_scr[0:SUBLANES, :] = carry_scr[j]

    def xn_rows(r):
        return xn_ref[r * rb:(r + 1) * rb, :]

    def gate_matmuls(r):
        return (jnp.dot(xn_rows(r), wc_scr[...], preferred_element_type=F32),
                jnp.dot(xn_rows(r), wh_scr[...], preferred_element_type=F32))

    def store_u(r, acc):
        u_scr[SUBLANES + r * rb:SUBLANES + (r + 1) * rb, :] = acc[0] * acc[1]

    _row_block_pipeline(tm // rb, gate_matmuls, store_u)
    carry_scr[j] = u_scr[tm:tm + SUBLANES, :]
    cw = cw_ref[...]

    def b_matmul(r):
        return jnp.dot(xn_rows(r), wb_scr[...], preferred_element_type=F32)

    def conv_out(r, gate_b):
        def shifted(back):
            start = SUBLANES + r * rb - back
            return u_scr[start:start + rb, :]
        y = cw[2:3, :] * shifted(0) + cw[1:2, :] * shifted(1) + cw[0:1, :] * shifted(2)
        o_ref[r * rb:(r + 1) * rb, :] = (gate_b * y).astype(o_ref.dtype)

    _row_block_pipeline(tm // rb, b_matmul, conv_out)


def _conv_proj(xn, w_in, conv_w, col_b, col_c, col_h, n, seq, tm, tn, rb):
    t, d = xn.shape
    jb, jc, jh = col_b // tn, col_c // tn, col_h // tn
    nj = n // tn
    kern = functools.partial(_conv_kernel, tiles_per_seq=seq // tm, rb=rb)
    w_scratch = [pltpu.VMEM((d, tn), BF16)] * 3
    return pl.pallas_call(
        kern,
        grid=(t // tm, nj),
        in_specs=[pl.BlockSpec((tm, d), lambda i, j: (i, 0)),
                  pl.BlockSpec((d, tn), lambda i, j: (0, jb + j)),
                  pl.BlockSpec((d, tn), lambda i, j: (0, jc + j)),
                  pl.BlockSpec((d, tn), lambda i, j: (0, jh + j)),
                  pl.BlockSpec((CONV_K, tn), lambda i, j: (0, j))],
        out_specs=pl.BlockSpec((tm, tn), lambda i, j: (i, j)),
        out_shape=jax.ShapeDtypeStruct((t, n), BF16),
        scratch_shapes=w_scratch + [pltpu.VMEM((tm + SUBLANES, tn), F32),
                                    pltpu.VMEM((nj, SUBLANES, tn), F32)],
        compiler_params=_params(("arbitrary", "arbitrary")),
        name="conv_proj",
    )(xn, w_in, w_in, w_in, conv_w)


def _attn_kernel(q_ref, k_ref, v_ref, lam_ref, sw_ref, o_ref,
                 vt_scr, bias_scr, qq_a, qq_b, m_a, m_b, acc_a, acc_b, *s_bufs,
                 tq, tk, pw, ahead):
    seq = k_ref.shape[1]
    n_q = seq // tq
    n_diag = tq // tk
    n_buf = len(s_bufs)
    n_panels = 2 * tq // pw
    assert n_buf == n_diag and ahead < n_buf
    all_visible = ("full",) * n_panels
    state = ((qq_a, m_a, acc_a), (qq_b, m_b, acc_b))

    for c in range(seq // LANES):
        blk = v_ref[0, c * LANES:(c + 1) * LANES, :].astype(F32)
        vt_scr[0:V_DIM, c * LANES:(c + 1) * LANES] = blk.T.astype(BF16)
    vt_scr[V_DIM:, 0:seq] = jnp.ones((vt_scr.shape[0] - V_DIM, seq), BF16)
    key = lax.broadcasted_iota(jnp.int32, bias_scr.shape, 0)
    qry = lax.broadcasted_iota(jnp.int32, bias_scr.shape, 1)
    bias_scr[...] = jnp.where(key // CHUNK <= qry // CHUNK, 0.0, -1e30)

    lq1, lk1, lq2, lk2 = (lam_ref[r:r + 1, :] for r in range(4))
    lam = (jnp.exp(jnp.sum(lq1 * lk1, axis=-1, keepdims=True))
           - jnp.exp(jnp.sum(lq2 * lk2, axis=-1, keepdims=True)) + LAMBDA_INIT)

    def diag_modes(key_off):
        modes = []
        for c in range(n_panels):
            q_off = (c * pw) % tq
            if key_off + tk <= q_off:
                modes.append("full")
            elif key_off >= q_off + pw:
                modes.append("skip")
            else:
                assert key_off == q_off and tk == pw
                modes.append("tri")
        return tuple(modes)

    def prepare(qi):
        qq_scr, m_scr, acc_scr = state[qi % 2]
        dim = lax.broadcasted_iota(jnp.int32, (V_DIM, 1), 0)
        for c in range(tq // LANES):
            rows = slice(qi * tq + c * LANES, qi * tq + (c + 1) * LANES)
            qt = q_ref[0, rows, :].astype(F32).T
            cols = slice(c * LANES, (c + 1) * LANES)
            qq_scr[:, cols] = jnp.where(dim < HEAD_DIM, qt, 0.0).astype(BF16)
            cols = slice(tq + c * LANES, tq + (c + 1) * LANES)
            qq_scr[:, cols] = jnp.where(dim < HEAD_DIM, 0.0, qt).astype(BF16)
        m_scr[...] = jnp.full(m_scr.shape, -jnp.inf, F32)
        acc_scr[...] = jnp.zeros(acc_scr.shape, F32)

    stages = []
    for qi in range(n_q):
        n_full = n_diag * qi
        for t in range(n_full + n_diag):
            modes = all_visible if t < n_full else diag_modes((t - n_full) * tk)
            stages.append((qi, t, modes))

    def scores(g, c):
        qi, t, modes = stages[g]
        if modes[c] == "skip":
            return
        cols = slice(c * pw, (c + 1) * pw)
        s_bufs[g % n_buf][:, cols] = jnp.dot(
            k_ref[0, t * tk:(t + 1) * tk, :], state[qi % 2][0][:, cols],
            preferred_element_type=F32)

    def update(g, c):
        qi, t, modes = stages[g]
        if modes[c] == "skip":
            return
        _, m_scr, acc_scr = state[qi % 2]
        cols = slice(c * pw, (c + 1) * pw)
        s = s_bufs[g % n_buf][:, cols]
        if modes[c] == "tri":
            s = s + bias_scr[...]
        m_prev = m_scr[:, cols]
        m_new = jnp.maximum(m_prev, jnp.max(s, axis=0, keepdims=True))
        alpha = jnp.exp2(m_prev - m_new)
        p = jnp.exp2(s - m_new).astype(BF16)
        m_scr[:, cols] = m_new
        acc_scr[:, cols] = alpha * acc_scr[:, cols] + jnp.dot(
            vt_scr[:, t * tk:(t + 1) * tk], p, preferred_element_type=F32)

    def finish(qi):
        acc = state[qi % 2][2][:, 0:2 * tq]
        o = acc[0:V_DIM, :] / acc[V_DIM:V_DIM + 1, :]
        a = (o[:, 0:tq] - lam * o[:, tq:]).T
        ms = jnp.mean(a * a, axis=-1, keepdims=True)
        o_ref[0, qi * tq:(qi + 1) * tq, :] = (
            ((a * lax.rsqrt(ms + SUBLN_EPS)) * sw_ref[...])
            * (1.0 - LAMBDA_INIT)).astype(o_ref.dtype)

    def start(g):
        qi, t, _ = stages[g]
        if t == 0:
            prepare(qi)

    for g in range(ahead):
        start(g)
        for c in range(n_panels):
            scores(g, c)
    for g, (qi, t, _) in enumerate(stages):
        if g + ahead < len(stages):
            start(g + ahead)
        for c in range(n_panels):
            update(g, c)
            if g + ahead < len(stages):
                scores(g + ahead, c)
        if t == n_diag * (qi + 1) - 1:
            finish(qi)


def _attention(qkv, lam_vecs, subln_w, tq, tk, pw, ahead):
    b, s, w3 = qkv.shape
    w = w3 // 3
    h = w // V_DIM
    kern = functools.partial(_attn_kernel, tq=tq, tk=tk, pw=pw, ahead=ahead)

    def head_block(first):
        return pl.BlockSpec((1, s, V_DIM), lambda bi, hi: (bi, 0, first + hi))

    per_q_tile = ([pltpu.VMEM((V_DIM, 2 * tq), BF16)] * 2
                  + [pltpu.VMEM((1, 2 * tq), F32)] * 2
                  + [pltpu.VMEM((V_DIM + BF16_SUBLANES, 2 * tq + LANES), F32)] * 2)
    return pl.pallas_call(
        kern,
        grid=(b, h),
        in_specs=[head_block(0), head_block(h), head_block(2 * h),
                  pl.BlockSpec((4, HEAD_DIM), lambda bi, hi: (0, 0)),
                  pl.BlockSpec((1, V_DIM), lambda bi, hi: (0, 0))],
        out_specs=head_block(0),
        out_shape=jax.ShapeDtypeStruct((b, s, w), BF16),
        scratch_shapes=[pltpu.VMEM((V_DIM + BF16_SUBLANES, s + LANES), BF16),
                        pltpu.VMEM((tk, pw), F32)]
        + per_q_tile
        + [pltpu.VMEM((tk, 2 * tq + LANES), F32)] * (tq // tk),
        compiler_params=_params(("parallel", "parallel")),
        name="diff_attention",
    )(qkv, qkv, qkv, lam_vecs, subln_w)


def _out_kernel(a_ref, c_ref, wa_ref, wc_ref, x_ref, fnw_ref, o_ref, fn_ref, h_scr,
                *, rb):
    j = pl.program_id(1)
    n_j, tm, tn = h_scr.shape
    h = x_ref[...] + _mm(a_ref[...], wa_ref[...]) + _mm(c_ref[...], wc_ref[...])
    o_ref[...] = h
    h_scr[j] = h

    @pl.when(j == n_j - 1)
    def _():
        for r in range(tm // rb):
            rows = slice(r * rb, (r + 1) * rb)
            ssq = sum(jnp.sum(h_scr[jj, rows, :] * h_scr[jj, rows, :], axis=-1, keepdims=True)
                      for jj in range(n_j))
            scale = lax.rsqrt(ssq * (1.0 / (n_j * tn)) + EPS)
            for jj in range(n_j):
                cols = slice(jj * tn, (jj + 1) * tn)
                fn_ref[rows, cols] = ((h_scr[jj, rows, :] * scale)
                                      * fnw_ref[:, cols]).astype(fn_ref.dtype)


def _out_proj(attn, conv, w_out, x, ffn_norm_w, tm, tn, rb):
    t, ka = attn.shape
    kc = conv.shape[1]
    n = w_out.shape[1]
    assert ka == kc
    return pl.pallas_call(
        functools.partial(_out_kernel, rb=rb),
        grid=(t // tm, n // tn),
        in_specs=[pl.BlockSpec((tm, ka), lambda i, j: (i, 0)),
                  pl.BlockSpec((tm, kc), lambda i, j: (i, 0)),
                  pl.BlockSpec((ka, tn), lambda i, j: (0, j)),
                  pl.BlockSpec((kc, tn), lambda i, j: (1, j)),
                  pl.BlockSpec((tm, tn), lambda i, j: (i, j)),
                  pl.BlockSpec((1, n), lambda i, j: (0, 0))],
        out_specs=[pl.BlockSpec((tm, tn), lambda i, j: (i, j)),
                   pl.BlockSpec((tm, n), lambda i, j: (i, 0))],
        out_shape=[jax.ShapeDtypeStruct((t, n), F32),
                   jax.ShapeDtypeStruct((t, n), BF16)],
        scratch_shapes=[pltpu.VMEM((n // tn, tm, tn), F32)],
        compiler_params=_params(("arbitrary", "arbitrary")),
        name="out_proj",
    )(attn, conv, w_out, w_out, x, ffn_norm_w.reshape(1, n))


def _gate_up_kernel(x_ref, wg_ref, wu_ref, o_ref):
    x = x_ref[...]
    g = _mm(x, wg_ref[...])
    u = _mm(x, wu_ref[...])
    o_ref[...] = ((g * jax.nn.sigmoid(g)) * u).astype(o_ref.dtype)


def _gate_up(fn, w_gate, w_up, tm, tn):
    t, d = fn.shape
    f = w_gate.shape[1]
    return pl.pallas_call(
        _gate_up_kernel,
        grid=(t // tm, f // tn),
        in_specs=[pl.BlockSpec((tm, d), lambda i, j: (i, 0)),
                  pl.BlockSpec((d, tn), lambda i, j: (0, j)),
                  pl.BlockSpec((d, tn), lambda i, j: (0, j))],
        out_specs=pl.BlockSpec((tm, tn), lambda i, j: (i, j)),
        out_shape=jax.ShapeDtypeStruct((t, f), BF16),
        compiler_params=_params(("parallel", "parallel")),
        name="gate_up",
    )(fn, w_gate, w_up)


def _down_kernel(h_ref, w_ref, r_ref, o_ref):
    o_ref[...] = r_ref[...] + _mm(h_ref[...], w_ref[...])


def _down(h, w_down, resid, tm, tn):
    t, f = h.shape
    n = w_down.shape[1]
    return pl.pallas_call(
        _down_kernel,
        grid=(t // tm, n // tn),
        in_specs=[pl.BlockSpec((tm, f), lambda i, j: (i, 0)),
                  pl.BlockSpec((f, tn), lambda i, j: (0, j)),
                  pl.BlockSpec((tm, tn), lambda i, j: (i, j))],
        out_specs=pl.BlockSpec((tm, tn), lambda i, j: (i, j)),
        out_shape=jax.ShapeDtypeStruct((t, n), F32),
        compiler_params=_params(("parallel", "parallel")),
        name="down_proj",
    )(h, w_down, resid)


TILES = dict(
    proj_tm=1024, proj_tn=1024, proj_rb=256,
    conv_tm=1024, conv_tn=512, conv_rb=256,
    attn_tq=1024, attn_tk=256, attn_pw=256, attn_ahead=2,
    out_tm=1024, out_tn=512, out_rb=256,
    gu_tm=1024, gu_tn=512,
    down_tm=1024, down_tn=256,
)


def _rope_tables(seq):
    pos = jnp.arange(seq, dtype=F32)
    inv_freq = ROPE_THETA ** (-jnp.arange(0, HEAD_DIM, 2, dtype=F32) / HEAD_DIM)
    ang = pos[:, None] * inv_freq[None, :]
    cos, sin = jnp.cos(ang), jnp.sin(ang)
    reps = LANES // (HEAD_DIM // 2)
    cos_l = jnp.tile(cos, (1, reps))
    sin_l = jnp.tile(jnp.concatenate([-sin, sin], axis=1), (1, reps // 2))
    return cos_l, sin_l


def kernel(x, attn_norm_w, w_in, q_norm_w, k_norm_w, lambda_q1, lambda_k1,
           lambda_q2, lambda_k2, subln_w, conv_w, w_out, ffn_norm_w, w_gate,
           w_up, w_down):
    b, s, d = x.shape
    t = b * s
    attn_w = N_HEADS * V_DIM
    qk_cols = N_HEADS * 2 * HEAD_DIM
    conv_width = d - attn_w
    col_v = 2 * qk_cols
    col_b = col_v + attn_w
    col_c = col_b + conv_width
    col_h = col_c + conv_width
    tl = TILES
    assert w_in.shape[0] == 1, "single-layer block"

    x2 = x.reshape(t, d)
    cos_l, sin_l = _rope_tables(s)
    scale = math.log2(math.e) / math.sqrt(HEAD_DIM)
    nw = jnp.concatenate([jnp.tile(q_norm_w[0] * scale, qk_cols // HEAD_DIM),
                          jnp.tile(k_norm_w[0], qk_cols // HEAD_DIM)]).reshape(1, -1)
    lam_vecs = jnp.concatenate([lambda_q1, lambda_k1, lambda_q2, lambda_k2], axis=0)

    qkv, xn = _qkv_proj(x2, attn_norm_w[0], w_in[0], nw, cos_l, sin_l, s, col_b,
                        tl["proj_tm"], tl["proj_tn"], tl["proj_rb"])
    conv = _conv_proj(xn, w_in[0], conv_w[0], col_b, col_c, col_h, conv_width, s,
                      tl["conv_tm"], tl["conv_tn"], tl["conv_rb"])

    attn = _attention(qkv.reshape(b, s, col_b), lam_vecs, subln_w[0].reshape(1, V_DIM),
                      tl["attn_tq"], tl["attn_tk"], tl["attn_pw"], tl["attn_ahead"])

    h1, fn = _out_proj(attn.reshape(t, attn_w), conv, w_out[0], x2, ffn_norm_w[0],
                       tl["out_tm"], tl["out_tn"], tl["out_rb"])
    hid = _gate_up(fn, w_gate[0], w_up[0], tl["gu_tm"], tl["gu_tn"])
    out = _down(hid, w_down[0], h1, tl["down_tm"], tl["down_tn"])
    return out.reshape(b, s, d)
```

```python
import functools
import math

import jax
import jax.numpy as jnp
from jax import lax
from jax.experimental import pallas as pl
from jax.experimental.pallas import tpu as pltpu

F32 = jnp.float32
BF16 = jnp.bfloat16

CHUNK = 64
HEAD_DIM = 64
V_DIM = 2 * HEAD_DIM
N_HEADS = 8
CONV_K = 3
ROPE_THETA = 10000.0
EPS = 1e-6
SUBLN_EPS = 1e-5
LAMBDA_INIT = 0.8 - 0.6 * math.exp(-0.3 * 0)
MASKED_SCORE = -1e30

LANES = 128
SUBLANES = 8
BF16_SUBLANES = 16
VMEM_LIMIT = 56 * 1024 * 1024


def _params(semantics):
    return pltpu.CompilerParams(dimension_semantics=semantics,
                                vmem_limit_bytes=VMEM_LIMIT)


def _mm(a_bf16, w_f32):
    return jnp.dot(a_bf16, w_f32.astype(BF16), preferred_element_type=F32)


def _rms_scale(x, w, eps):
    ms = jnp.mean(x * x, axis=-1, keepdims=True)
    return (x * lax.rsqrt(ms + eps)) * w


def _row_block_pipeline(n_blocks, matmul, epilogue):
    acc = matmul(0)
    for r in range(n_blocks):
        nxt = matmul(r + 1) if r + 1 < n_blocks else None
        epilogue(r, acc)
        acc = nxt


def _qkv_kernel(x_ref, anw_ref, w_ref, nw_ref, cos_ref, sin_ref, o_ref, xn_ref,
                w_scr, *, rb, n_qk_tiles):
    j = pl.program_id(1)
    tm, tn = o_ref.shape

    @pl.when(j == 0)
    def _():
        for r in range(tm // rb):
            rows = slice(r * rb, (r + 1) * rb)
            xn_ref[rows, :] = _rms_scale(x_ref[rows, :], anw_ref[...], EPS).astype(BF16)

    w_scr[...] = w_ref[...].astype(BF16)
    lane = lax.broadcasted_iota(jnp.int32, (1, LANES), 1)
    lo_half = lane < HEAD_DIM
    first = (lane % HEAD_DIM) < (HEAD_DIM // 2)

    def matmul(r):
        return jnp.dot(xn_ref[r * rb:(r + 1) * rb, :], w_scr[...],
                       preferred_element_type=F32)

    def plain(r, acc):
        o_ref[r * rb:(r + 1) * rb, :] = acc.astype(o_ref.dtype)

    def norm_rope(r, acc):
        rows = slice(r * rb, (r + 1) * rb)
        cos = cos_ref[rows, :]
        sin = sin_ref[rows, :]
        for c in range(tn // LANES):
            cols = slice(c * LANES, (c + 1) * LANES)
            y = acc[:, cols]
            sq = y * y
            s_lo = jnp.sum(jnp.where(lo_half, sq, 0.0), axis=-1, keepdims=True)
            s_hi = jnp.sum(jnp.where(lo_half, 0.0, sq), axis=-1, keepdims=True)
            ms = jnp.where(lo_half, s_lo, s_hi) * (1.0 / HEAD_DIM)
            yn = (y * lax.rsqrt(ms + EPS)) * nw_ref[:, cols]
            rot = jnp.where(first, pltpu.roll(yn, LANES - HEAD_DIM // 2, 1),
                            pltpu.roll(yn, HEAD_DIM // 2, 1))
            o_ref[rows, cols] = (yn * cos + rot * sin).astype(o_ref.dtype)

    @pl.when(j < n_qk_tiles)
    def _():
        _row_block_pipeline(tm // rb, matmul, norm_rope)

    @pl.when(j >= n_qk_tiles)
    def _():
        _row_block_pipeline(tm // rb, matmul, plain)


def _qkv_proj(x, attn_norm_w, w_in, nw, cos, sin, seq, n_out, tm, tn, rb):
    t, d = x.shape
    n_qk_tiles = nw.shape[1] // tn
    tiles_per_seq = seq // tm
    kern = functools.partial(_qkv_kernel, rb=rb, n_qk_tiles=n_qk_tiles)
    return pl.pallas_call(
        kern,
        grid=(t // tm, n_out // tn),
        in_specs=[pl.BlockSpec((tm, d), lambda i, j: (i, 0)),
                  pl.BlockSpec((1, d), lambda i, j: (0, 0)),
                  pl.BlockSpec((d, tn), lambda i, j: (0, j)),
                  pl.BlockSpec((1, tn), lambda i, j: (0, jnp.minimum(j, n_qk_tiles - 1))),
                  pl.BlockSpec((tm, LANES), lambda i, j: (i % tiles_per_seq, 0)),
                  pl.BlockSpec((tm, LANES), lambda i, j: (i % tiles_per_seq, 0))],
        out_specs=[pl.BlockSpec((tm, tn), lambda i, j: (i, j)),
                   pl.BlockSpec((tm, d), lambda i, j: (i, 0))],
        out_shape=[jax.ShapeDtypeStruct((t, n_out), BF16),
                   jax.ShapeDtypeStruct((t, d), BF16)],
        scratch_shapes=[pltpu.VMEM((d, tn), BF16)],
        compiler_params=_params(("arbitrary", "arbitrary")),
        name="qkv_proj",
    )(x, attn_norm_w.reshape(1, d), w_in, nw, cos, sin)


def _conv_kernel(xn_ref, wb_ref, wc_ref, wh_ref, cw_ref, o_ref,
                 wb_scr, wc_scr, wh_scr, u_scr, carry_scr, *, tiles_per_seq, rb):
    i = pl.program_id(0)
    j = pl.program_id(1)
    tm = xn_ref.shape[0]
    wc_scr[...] = wc_ref[...].astype(BF16)
    wh_scr[...] = wh_ref[...].astype(BF16)
    wb_scr[...] = wb_ref[...].astype(BF16)

    @pl.when(i % tiles_per_seq == 0)
    def _():
        u_scr[0:SUBLANES, :] = jnp.zeros((SUBLANES, u_scr.shape[1]), F32)

    @pl.when(i % tiles_per_seq != 0)
    def _():
        u_scr[0:SUBLANES, :] = carry_scr[j]

    def xn_rows(r):
        return xn_ref[r * rb:(r + 1) * rb, :]

    def gate_matmuls(r):
        return (jnp.dot(xn_rows(r), wc_scr[...], preferred_element_type=F32),
                jnp.dot(xn_rows(r), wh_scr[...], preferred_element_type=F32))

    def store_u(r, acc):
        u_scr[SUBLANES + r * rb:SUBLANES + (r + 1) * rb, :] = acc[0] * acc[1]

    _row_block_pipeline(tm // rb, gate_matmuls, store_u)
    carry_scr[j] = u_scr[tm:tm + SUBLANES, :]
    cw = cw_ref[...]

    def b_matmul(r):
        return jnp.dot(xn_rows(r), wb_scr[...], preferred_element_type=F32)

    def conv_out(r, gate_b):
        def shifted(back):
            start = SUBLANES + r * rb - back
            return u_scr[start:start + rb, :]
        y = cw[2:3, :] * shifted(0) + cw[1:2, :] * shifted(1) + cw[0:1, :] * shifted(2)
        o_ref[r * rb:(r + 1) * rb, :] = (gate_b * y).astype(o_ref.dtype)

    _row_block_pipeline(tm // rb, b_matmul, conv_out)


def _conv_proj(xn, w_in, conv_w, col_b, col_c, col_h, n, seq, tm, tn, rb):
    t, d = xn.shape
    jb, jc, jh = col_b // tn, col_c // tn, col_h // tn
    nj = n // tn
    kern = functools.partial(_conv_kernel, tiles_per_seq=seq // tm, rb=rb)
    w_scratch = [pltpu.VMEM((d, tn), BF16)] * 3
    return pl.pallas_call(
        kern,
        grid=(t // tm, nj),
        in_specs=[pl.BlockSpec((tm, d), lambda i, j: (i, 0)),
                  pl.BlockSpec((d, tn), lambda i, j: (0, jb + j)),
                  pl.BlockSpec((d, tn), lambda i, j: (0, jc + j)),
                  pl.BlockSpec((d, tn), lambda i, j: (0, jh + j)),
                  pl.BlockSpec((CONV_K, tn), lambda i, j: (0, j))],
        out_specs=pl.BlockSpec((tm, tn), lambda i, j: (i, j)),
        out_shape=jax.ShapeDtypeStruct((t, n), BF16),
        scratch_shapes=w_scratch + [pltpu.VMEM((tm + SUBLANES, tn), F32),
                                    pltpu.VMEM((nj, SUBLANES, tn), F32)],
        compiler_params=_params(("arbitrary", "arbitrary")),
        name="conv_proj",
    )(xn, w_in, w_in, w_in, conv_w)


def _attn_kernel(q_ref, k_ref, v_ref, lam_ref, sw_ref, o_ref,
                 vt_scr, bias_scr, qq_a, qq_b, m_a, m_b, acc_a, acc_b, *s_bufs,
                 tq, tk, pw, ahead):
    seq = k_ref.shape[1]
    n_q = seq // tq
    n_diag = tq // tk
    n_buf = len(s_bufs)
    n_panels = 2 * tq // pw
    assert ahead < n_buf
    all_visible = ("full",) * n_panels
    state = ((qq_a, m_a, acc_a), (qq_b, m_b, acc_b))

    for c in range(seq // LANES):
        blk = v_ref[0, c * LANES:(c + 1) * LANES, :].astype(F32)
        vt_scr[0:V_DIM, c * LANES:(c + 1) * LANES] = blk.T.astype(BF16)
    vt_scr[V_DIM:, :] = jnp.ones((vt_scr.shape[0] - V_DIM, seq), BF16)
    key = lax.broadcasted_iota(jnp.int32, bias_scr.shape, 0)
    qry = lax.broadcasted_iota(jnp.int32, bias_scr.shape, 1)
    bias_scr[...] = jnp.where(key // CHUNK <= qry // CHUNK, 0.0, MASKED_SCORE)

    lq1, lk1, lq2, lk2 = (lam_ref[r:r + 1, :] for r in range(4))
    lam = (jnp.exp(jnp.sum(lq1 * lk1, axis=-1, keepdims=True))
           - jnp.exp(jnp.sum(lq2 * lk2, axis=-1, keepdims=True)) + LAMBDA_INIT)

    def diag_modes(key_off):
        modes = []
        for c in range(n_panels):
            q_off = (c * pw) % tq
            if key_off + tk <= q_off:
                modes.append("full")
            elif key_off >= q_off + pw:
                modes.append("skip")
            else:
                assert key_off == q_off and tk == pw
                modes.append("tri")
        return tuple(modes)

    def prepare(qi):
        qq_scr, m_scr, acc_scr = state[qi % 2]
        dim = lax.broadcasted_iota(jnp.int32, (V_DIM, 1), 0)
        for c in range(tq // LANES):
            rows = slice(qi * tq + c * LANES, qi * tq + (c + 1) * LANES)
            qt = q_ref[0, rows, :].astype(F32).T
            cols = slice(c * LANES, (c + 1) * LANES)
            qq_scr[:, cols] = jnp.where(dim < HEAD_DIM, qt, 0.0).astype(BF16)
            cols = slice(tq + c * LANES, tq + (c + 1) * LANES)
            qq_scr[:, cols] = jnp.where(dim < HEAD_DIM, 0.0, qt).astype(BF16)
        m_scr[...] = jnp.full(m_scr.shape, -jnp.inf, F32)
        acc_scr[...] = jnp.zeros(acc_scr.shape, F32)

    stages = []
    for qi in range(n_q):
        n_full = n_diag * qi
        for t in range(n_full + n_diag):
            modes = all_visible if t < n_full else diag_modes((t - n_full) * tk)
            stages.append((qi, t, modes))

    def scores(g, c):
        qi, t, modes = stages[g]
        if modes[c] == "skip":
            return
        cols = slice(c * pw, (c + 1) * pw)
        s_bufs[g % n_buf][:, cols] = jnp.dot(
            k_ref[0, t * tk:(t + 1) * tk, :], state[qi % 2][0][:, cols],
            preferred_element_type=F32)

    def update(g, c):
        qi, t, modes = stages[g]
        if modes[c] == "skip":
            return
        _, m_scr, acc_scr = state[qi % 2]
        cols = slice(c * pw, (c + 1) * pw)
        s = s_bufs[g % n_buf][:, cols]
        if modes[c] == "tri":
            s = s + bias_scr[...]
        m_prev = m_scr[:, cols]
        m_new = jnp.maximum(m_prev, jnp.max(s, axis=0, keepdims=True))
        alpha = jnp.exp2(m_prev - m_new)
        p = jnp.exp2(s - m_new).astype(BF16)
        m_scr[:, cols] = m_new
        acc_scr[:, cols] = alpha * acc_scr[:, cols] + jnp.dot(
            vt_scr[:, t * tk:(t + 1) * tk], p, preferred_element_type=F32)

    def finish(qi):
        acc = state[qi % 2][2][...]
        o = acc[0:V_DIM, :] / acc[V_DIM:V_DIM + 1, :]
        a = (o[:, 0:tq] - lam * o[:, tq:]).T
        ms = jnp.mean(a * a, axis=-1, keepdims=True)
        o_ref[0, qi * tq:(qi + 1) * tq, :] = (
            ((a * lax.rsqrt(ms + SUBLN_EPS)) * sw_ref[...])
            * (1.0 - LAMBDA_INIT)).astype(o_ref.dtype)

    def start(g):
        qi, t, _ = stages[g]
        if t == 0:
            prepare(qi)

    for g in range(ahead):
        start(g)
        for c in range(n_panels):
            scores(g, c)
    for g, (qi, t, _) in enumerate(stages):
        if g + ahead < len(stages):
            start(g + ahead)
        for c in range(n_panels):
            update(g, c)
            if g + ahead < len(stages):
                scores(g + ahead, c)
        if t == n_diag * (qi + 1) - 1:
            finish(qi)


def _attention(qkv, lam_vecs, subln_w, tq, tk, pw, ahead):
    b, s, w3 = qkv.shape
    w = w3 // 3
    h = w // V_DIM
    kern = functools.partial(_attn_kernel, tq=tq, tk=tk, pw=pw, ahead=ahead)

    def head_block(first):
        return pl.BlockSpec((1, s, V_DIM), lambda bi, hi: (bi, 0, first + hi))

    per_q_tile = ([pltpu.VMEM((V_DIM, 2 * tq), BF16)] * 2
                  + [pltpu.VMEM((1, 2 * tq), F32)] * 2
                  + [pltpu.VMEM((V_DIM + BF16_SUBLANES, 2 * tq), F32)] * 2)
    return pl.pallas_call(
        kern,
        grid=(b, h),
        in_specs=[head_block(0), head_block(h), head_block(2 * h),
                  pl.BlockSpec((4, HEAD_DIM), lambda bi, hi: (0, 0)),
                  pl.BlockSpec((1, V_DIM), lambda bi, hi: (0, 0))],
        out_specs=head_block(0),
        out_shape=jax.ShapeDtypeStruct((b, s, w), BF16),
        scratch_shapes=[pltpu.VMEM((V_DIM + BF16_SUBLANES, s), BF16),
                        pltpu.VMEM((tk, pw), F32)]
        + per_q_tile
        + [pltpu.VMEM((tk, 2 * tq), F32)] * (ahead + 1),
        compiler_params=_params(("parallel", "parallel")),
        name="diff_attention",
    )(qkv, qkv, qkv, lam_vecs, subln_w)


def _out_kernel(a_ref, c_ref, wa_ref, wc_ref, x_ref, fnw_ref, o_ref, fn_ref, h_scr,
                *, rb):
    j = pl.program_id(1)
    n_j, tm, tn = h_scr.shape
    h = x_ref[...] + _mm(a_ref[...], wa_ref[...]) + _mm(c_ref[...], wc_ref[...])
    o_ref[...] = h
    h_scr[j] = h

    @pl.when(j == n_j - 1)
    def _():
        for r in range(tm // rb):
            rows = slice(r * rb, (r + 1) * rb)
            ssq = sum(jnp.sum(h_scr[jj, rows, :] * h_scr[jj, rows, :], axis=-1, keepdims=True)
                      for jj in range(n_j))
            scale = lax.rsqrt(ssq * (1.0 / (n_j * tn)) + EPS)
            for jj in range(n_j):
                cols = slice(jj * tn, (jj + 1) * tn)
                fn_ref[rows, cols] = ((h_scr[jj, rows, :] * scale)
                                      * fnw_ref[:, cols]).astype(fn_ref.dtype)


def _out_proj(attn, conv, w_out, x, ffn_norm_w, tm, tn, rb):
    t, ka = attn.shape
    kc = conv.shape[1]
    n = w_out.shape[1]
    assert ka == kc
    return pl.pallas_call(
        functools.partial(_out_kernel, rb=rb),
        grid=(t // tm, n // tn),
        in_specs=[pl.BlockSpec((tm, ka), lambda i, j: (i, 0)),
                  pl.BlockSpec((tm, kc), lambda i, j: (i, 0)),
                  pl.BlockSpec((ka, tn), lambda i, j: (0, j)),
                  pl.BlockSpec((kc, tn), lambda i, j: (1, j)),
                  pl.BlockSpec((tm, tn), lambda i, j: (i, j)),
                  pl.BlockSpec((1, n), lambda i, j: (0, 0))],
        out_specs=[pl.BlockSpec((tm, tn), lambda i, j: (i, j)),
                   pl.BlockSpec((tm, n), lambda i, j: (i, 0))],
        out_shape=[jax.ShapeDtypeStruct((t, n), F32),
                   jax.ShapeDtypeStruct((t, n), BF16)],
        scratch_shapes=[pltpu.VMEM((n // tn, tm, tn), F32)],
        compiler_params=_params(("arbitrary", "arbitrary")),
        name="out_proj",
    )(attn, conv, w_out, w_out, x, ffn_norm_w.reshape(1, n))


def _gate_up_kernel(x_ref, wg_ref, wu_ref, o_ref):
    x = x_ref[...]
    g = _mm(x, wg_ref[...])
    u = _mm(x, wu_ref[...])
    o_ref[...] = ((g * jax.nn.sigmoid(g)) * u).astype(o_ref.dtype)


def _gate_up(fn, w_gate, w_up, tm, tn):
    t, d = fn.shape
    f = w_gate.shape[1]
    return pl.pallas_call(
        _gate_up_kernel,
        grid=(t // tm, f // tn),
        in_specs=[pl.BlockSpec((tm, d), lambda i, j: (i, 0)),
                  pl.BlockSpec((d, tn), lambda i, j: (0, j)),
                  pl.BlockSpec((d, tn), lambda i, j: (0, j))],
        out_specs=pl.BlockSpec((tm, tn), lambda i, j: (i, j)),
        out_shape=jax.ShapeDtypeStruct((t, f), BF16),
        compiler_params=_params(("parallel", "parallel")),
        name="gate_up",
    )(fn, w_gate, w_up)


def _down_kernel(h_ref, w_ref, r_ref, o_ref):
    o_ref[...] = r_ref[...] + _mm(h_ref[...], w_ref[...])


def _down(h, w_down, resid, tm, tn):
    t, f = h.shape
    n = w_down.shape[1]
    return pl.pallas_call(
        _down_kernel,
        grid=(t // tm, n // tn),
        in_specs=[pl.BlockSpec((tm, f), lambda i, j: (i, 0)),
                  pl.BlockSpec((f, tn), lambda i, j: (0, j)),
                  pl.BlockSpec((tm, tn), lambda i, j: (i, j))],
        out_specs=pl.BlockSpec((tm, tn), lambda i, j: (i, j)),
        out_shape=jax.ShapeDtypeStruct((t, n), F32),
        compiler_params=_params(("parallel", "parallel")),
        name="down_proj",
    )(h, w_down, resid)


TILES = dict(
    proj_tm=1024, proj_tn=1024, proj_rb=256,
    conv_tm=1024, conv_tn=512, conv_rb=256,
    attn_tq=1024, attn_tk=256, attn_pw=256, attn_ahead=3,
    out_tm=1024, out_tn=512, out_rb=256,
    gu_tm=1024, gu_tn=512,
    down_tm=1024, down_tn=256,
)


def _rope_tables(seq):
    pos = jnp.arange(seq, dtype=F32)
    inv_freq = ROPE_THETA ** (-jnp.arange(0, HEAD_DIM, 2, dtype=F32) / HEAD_DIM)
    ang = pos[:, None] * inv_freq[None, :]
    cos, sin = jnp.cos(ang), jnp.sin(ang)
    reps = LANES // (HEAD_DIM // 2)
    cos_l = jnp.tile(cos, (1, reps))
    sin_l = jnp.tile(jnp.concatenate([-sin, sin], axis=1), (1, reps // 2))
    return cos_l, sin_l


def kernel(x, attn_norm_w, w_in, q_norm_w, k_norm_w, lambda_q1, lambda_k1,
           lambda_q2, lambda_k2, subln_w, conv_w, w_out, ffn_norm_w, w_gate,
           w_up, w_down):
    b, s, d = x.shape
    t = b * s
    attn_w = N_HEADS * V_DIM
    qk_cols = N_HEADS * 2 * HEAD_DIM
    conv_width = d - attn_w
    col_v = 2 * qk_cols
    col_b = col_v + attn_w
    col_c = col_b + conv_width
    col_h = col_c + conv_width
    tl = TILES
    assert w_in.shape[0] == 1, "single-layer block"

    x2 = x.reshape(t, d)
    cos_l, sin_l = _rope_tables(s)
    scale = math.log2(math.e) / math.sqrt(HEAD_DIM)
    nw = jnp.concatenate([jnp.tile(q_norm_w[0] * scale, qk_cols // HEAD_DIM),
                          jnp.tile(k_norm_w[0], qk_cols // HEAD_DIM)]).reshape(1, -1)
    lam_vecs = jnp.concatenate([lambda_q1, lambda_k1, lambda_q2, lambda_k2], axis=0)

    qkv, xn = _qkv_proj(x2, attn_norm_w[0], w_in[0], nw, cos_l, sin_l, s, col_b,
                        tl["proj_tm"], tl["proj_tn"], tl["proj_rb"])
    conv = _conv_proj(xn, w_in[0], conv_w[0], col_b, col_c, col_h, conv_width, s,
                      tl["conv_tm"], tl["conv_tn"], tl["conv_rb"])

    attn = _attention(qkv.reshape(b, s, col_b), lam_vecs, subln_w[0].reshape(1, V_DIM),
                      tl["attn_tq"], tl["attn_tk"], tl["attn_pw"], tl["attn_ahead"])

    h1, fn = _out_proj(attn.reshape(t, attn_w), conv, w_out[0], x2, ffn_norm_w[0],
                       tl["out_tm"], tl["out_tn"], tl["out_rb"])
    hid = _gate_up(fn, w_gate[0], w_up[0], tl["gu_tm"], tl["gu_tn"])
    out = _down(hid, w_down[0], h1, tl["down_tm"], tl["down_tn"])
    return out.reshape(b, s, d)
```

```python
import functools
import math

import jax
import jax.numpy as jnp
from jax import lax
from jax.experimental import pallas as pl
from jax.experimental.pallas import tpu as pltpu

F32 = jnp.float32
BF16 = jnp.bfloat16

CHUNK = 64
HEAD_DIM = 64
V_DIM = 2 * HEAD_DIM
N_HEADS = 8
CONV_K = 3
ROPE_THETA = 10000.0
EPS = 1e-6
SUBLN_EPS = 1e-5
LAMBDA_INIT = 0.8 - 0.6 * math.exp(-0.3 * 0)
MASKED_SCORE = -1e30

LANES = 128
SUBLANES = 8
BF16_SUBLANES = 16
VMEM_LIMIT = 56 * 1024 * 1024


def _params(semantics):
    return pltpu.CompilerParams(dimension_semantics=semantics,
                                vmem_limit_bytes=VMEM_LIMIT)


def _mm(a_bf16, w_f32):
    return jnp.dot(a_bf16, w_f32.astype(BF16), preferred_element_type=F32)


def _rms_scale(x, w, eps):
    ms = jnp.mean(x * x, axis=-1, keepdims=True)
    return (x * lax.rsqrt(ms + eps)) * w


def _row_block_pipeline(n_blocks, matmul, epilogue):
    acc = matmul(0)
    for r in range(n_blocks):
        nxt = matmul(r + 1) if r + 1 < n_blocks else None
        epilogue(r, acc)
        acc = nxt


def _qkv_kernel(x_ref, anw_ref, w_ref, nw_ref, cos_ref, sin_ref, o_ref, xn_ref,
                w_scr, *, rb, n_qk_tiles):
    j = pl.program_id(1)
    tm, tn = o_ref.shape

    @pl.when(j == 0)
    def _():
        for r in range(tm // rb):
            rows = slice(r * rb, (r + 1) * rb)
            xn_ref[rows, :] = _rms_scale(x_ref[rows, :], anw_ref[...], EPS).astype(BF16)

    w_scr[...] = w_ref[...].astype(BF16)
    lane = lax.broadcasted_iota(jnp.int32, (1, LANES), 1)
    lo_half = lane < HEAD_DIM
    first = (lane % HEAD_DIM) < (HEAD_DIM // 2)

    def matmul(r):
        return jnp.dot(xn_ref[r * rb:(r + 1) * rb, :], w_scr[...],
                       preferred_element_type=F32)

    def plain(r, acc):
        o_ref[r * rb:(r + 1) * rb, :] = acc.astype(o_ref.dtype)

    def norm_rope(r, acc):
        rows = slice(r * rb, (r + 1) * rb)
        cos = cos_ref[rows, :]
        sin = sin_ref[rows, :]
        for c in range(tn // LANES):
            cols = slice(c * LANES, (c + 1) * LANES)
            y = acc[:, cols]
            sq = y * y
            s_lo = jnp.sum(jnp.where(lo_half, sq, 0.0), axis=-1, keepdims=True)
            s_hi = jnp.sum(jnp.where(lo_half, 0.0, sq), axis=-1, keepdims=True)
            ms = jnp.where(lo_half, s_lo, s_hi) * (1.0 / HEAD_DIM)
            yn = (y * lax.rsqrt(ms + EPS)) * nw_ref[:, cols]
            rot = jnp.where(first, pltpu.roll(yn, LANES - HEAD_DIM // 2, 1),
                            pltpu.roll(yn, HEAD_DIM // 2, 1))
            o_ref[rows, cols] = (yn * cos + rot * sin).astype(o_ref.dtype)

    @pl.when(j < n_qk_tiles)
    def _():
        _row_block_pipeline(tm // rb, matmul, norm_rope)

    @pl.when(j >= n_qk_tiles)
    def _():
        _row_block_pipeline(tm // rb, matmul, plain)


def _qkv_proj(x, attn_norm_w, w_in, nw, cos, sin, seq, n_out, tm, tn, rb):
    t, d = x.shape
    n_qk_tiles = nw.shape[1] // tn
    tiles_per_seq = seq // tm
    kern = functools.partial(_qkv_kernel, rb=rb, n_qk_tiles=n_qk_tiles)
    return pl.pallas_call(
        kern,
        grid=(t // tm, n_out // tn),
        in_specs=[pl.BlockSpec((tm, d), lambda i, j: (i, 0)),
                  pl.BlockSpec((1, d), lambda i, j: (0, 0)),
                  pl.BlockSpec((d, tn), lambda i, j: (0, j)),
                  pl.BlockSpec((1, tn), lambda i, j: (0, jnp.minimum(j, n_qk_tiles - 1))),
                  pl.BlockSpec((tm, LANES), lambda i, j: (i % tiles_per_seq, 0)),
                  pl.BlockSpec((tm, LANES), lambda i, j: (i % tiles_per_seq, 0))],
        out_specs=[pl.BlockSpec((tm, tn), lambda i, j: (i, j)),
                   pl.BlockSpec((tm, d), lambda i, j: (i, 0))],
        out_shape=[jax.ShapeDtypeStruct((t, n_out), BF16),
                   jax.ShapeDtypeStruct((t, d), BF16)],
        scratch_shapes=[pltpu.VMEM((d, tn), BF16)],
        compiler_params=_params(("arbitrary", "arbitrary")),
        name="qkv_proj",
    )(x, attn_norm_w.reshape(1, d), w_in, nw, cos, sin)


def _conv_kernel(xn_ref, wb_ref, wc_ref, wh_ref, cw_ref, o_ref,
                 wb_scr, wc_scr, wh_scr, u_scr, carry_scr, *, tiles_per_seq, rb):
    i = pl.program_id(0)
    j = pl.program_id(1)
    tm = xn_ref.shape[0]
    wc_scr[...] = wc_ref[...].astype(BF16)
    wh_scr[...] = wh_ref[...].astype(BF16)
    wb_scr[...] = wb_ref[...].astype(BF16)

    @pl.when(i % tiles_per_seq == 0)
    def _():
        u_scr[0:SUBLANES, :] = jnp.zeros((SUBLANES, u_scr.shape[1]), F32)

    @pl.when(i % tiles_per_seq != 0)
    def _():
        u_scr[0:SUBLANES, :] = carry_scr[j]

    def xn_rows(r):
        return xn_ref[r * rb:(r + 1) * rb, :]

    def gate_matmuls(r):
        return (jnp.dot(xn_rows(r), wc_scr[...], preferred_element_type=F32),
                jnp.dot(xn_rows(r), wh_scr[...], preferred_element_type=F32))

    def store_u(r, acc):
        u_scr[SUBLANES + r * rb:SUBLANES + (r + 1) * rb, :] = acc[0] * acc[1]

    _row_block_pipeline(tm // rb, gate_matmuls, store_u)
    carry_scr[j] = u_scr[tm:tm + SUBLANES, :]
    cw = cw_ref[...]

    def b_matmul(r):
        return jnp.dot(xn_rows(r), wb_scr[...], preferred_element_type=F32)

    def conv_out(r, gate_b):
        def shifted(back):
            start = SUBLANES + r * rb - back
            return u_scr[start:start + rb, :]
        y = cw[2:3, :] * shifted(0) + cw[1:2, :] * shifted(1) + cw[0:1, :] * shifted(2)
        o_ref[r * rb:(r + 1) * rb, :] = (gate_b * y).astype(o_ref.dtype)

    _row_block_pipeline(tm // rb, b_matmul, conv_out)


def _conv_proj(xn, w_in, conv_w, col_b, col_c, col_h, n, seq, tm, tn, rb):
    t, d = xn.shape
    jb, jc, jh = col_b // tn, col_c // tn, col_h // tn
    nj = n // tn
    kern = functools.partial(_conv_kernel, tiles_per_seq=seq // tm, rb=rb)
    w_scratch = [pltpu.VMEM((d, tn), BF16)] * 3
    return pl.pallas_call(
        kern,
        grid=(t // tm, nj),
        in_specs=[pl.BlockSpec((tm, d), lambda i, j: (i, 0)),
                  pl.BlockSpec((d, tn), lambda i, j: (0, jb + j)),
                  pl.BlockSpec((d, tn), lambda i, j: (0, jc + j)),
                  pl.BlockSpec((d, tn), lambda i, j: (0, jh + j)),
                  pl.BlockSpec((CONV_K, tn), lambda i, j: (0, j))],
        out_specs=pl.BlockSpec((tm, tn), lambda i, j: (i, j)),
        out_shape=jax.ShapeDtypeStruct((t, n), BF16),
        scratch_shapes=w_scratch + [pltpu.VMEM((tm + SUBLANES, tn), F32),
                                    pltpu.VMEM((nj, SUBLANES, tn), F32)],
        compiler_params=_params(("arbitrary", "arbitrary")),
        name="conv_proj",
    )(xn, w_in, w_in, w_in, conv_w)


def _attn_kernel(q_ref, k_ref, v_ref, lam_ref, sw_ref, o_ref,
                 vt_scr, bias_scr, qq_a, qq_b, m_a, m_b, acc_a, acc_b, *s_bufs,
                 tq, tk, pw, group, ahead):
    seq = k_ref.shape[1]
    n_q = seq // tq
    n_diag = tq // tk
    n_buf = len(s_bufs)
    n_panels = 2 * tq // pw
    all_visible = ("full",) * n_panels
    state = ((qq_a, m_a, acc_a), (qq_b, m_b, acc_b))

    for c in range(seq // LANES):
        blk = v_ref[0, c * LANES:(c + 1) * LANES, :].astype(F32)
        vt_scr[0:V_DIM, c * LANES:(c + 1) * LANES] = blk.T.astype(BF16)
    vt_scr[V_DIM:, :] = jnp.ones((vt_scr.shape[0] - V_DIM, seq), BF16)
    key = lax.broadcasted_iota(jnp.int32, bias_scr.shape, 0)
    qry = lax.broadcasted_iota(jnp.int32, bias_scr.shape, 1)
    bias_scr[...] = jnp.where(key // CHUNK <= qry // CHUNK, 0.0, MASKED_SCORE)

    lq1, lk1, lq2, lk2 = (lam_ref[r:r + 1, :] for r in range(4))
    lam = (jnp.exp(jnp.sum(lq1 * lk1, axis=-1, keepdims=True))
           - jnp.exp(jnp.sum(lq2 * lk2, axis=-1, keepdims=True)) + LAMBDA_INIT)

    def diag_modes(key_off):
        modes = []
        for c in range(n_panels):
            q_off = (c * pw) % tq
            if key_off + tk <= q_off:
                modes.append("full")
            elif key_off >= q_off + pw:
                modes.append("skip")
            else:
                assert key_off == q_off and tk == pw
                modes.append("tri")
        return tuple(modes)

    def prepare(qi):
        qq_scr, m_scr, acc_scr = state[qi % 2]
        dim = lax.broadcasted_iota(jnp.int32, (V_DIM, 1), 0)
        for c in range(tq // LANES):
            rows = slice(qi * tq + c * LANES, qi * tq + (c + 1) * LANES)
            qt = q_ref[0, rows, :].astype(F32).T
            cols = slice(c * LANES, (c + 1) * LANES)
            qq_scr[:, cols] = jnp.where(dim < HEAD_DIM, qt, 0.0).astype(BF16)
            cols = slice(tq + c * LANES, tq + (c + 1) * LANES)
            qq_scr[:, cols] = jnp.where(dim < HEAD_DIM, 0.0, qt).astype(BF16)
        m_scr[...] = jnp.full(m_scr.shape, -jnp.inf, F32)
        acc_scr[...] = jnp.zeros(acc_scr.shape, F32)

    stages = []
    for qi in range(n_q):
        n_full = n_diag * qi
        for t in range(n_full + n_diag):
            modes = all_visible if t < n_full else diag_modes((t - n_full) * tk)
            stages.append((qi, t, modes))

    def scores(g, c):
        qi, t, modes = stages[g]
        if modes[c] == "skip":
            return
        cols = slice(c * pw, (c + 1) * pw)
        s_bufs[g % n_buf][:, cols] = jnp.dot(
            k_ref[0, t * tk:(t + 1) * tk, :], state[qi % 2][0][:, cols],
            preferred_element_type=F32)

    def update(gs, c):
        live = [g for g in gs if stages[g][2][c] != "skip"]
        if not live:
            return
        qi, t0, _ = stages[live[0]]
        _, m_scr, acc_scr = state[qi % 2]
        cols = slice(c * pw, (c + 1) * pw)
        ss = []
        for g in live:
            s = s_bufs[g % n_buf][:, cols]
            if stages[g][2][c] == "tri":
                s = s + bias_scr[...]
            ss.append(s)
        m_prev = m_scr[:, cols]
        m_new = m_prev
        for s in ss:
            m_new = jnp.maximum(m_new, jnp.max(s, axis=0, keepdims=True))
        alpha = jnp.exp2(m_prev - m_new)
        p = jnp.concatenate([jnp.exp2(s - m_new).astype(BF16) for s in ss], axis=0)
        m_scr[:, cols] = m_new
        acc_scr[:, cols] = alpha * acc_scr[:, cols] + jnp.dot(
            vt_scr[:, t0 * tk:(t0 + len(live)) * tk], p,
            preferred_element_type=F32)

    def finish(qi):
        acc = state[qi % 2][2][...]
        o = acc[0:V_DIM, :] / acc[V_DIM:V_DIM + 1, :]
        a = (o[:, 0:tq] - lam * o[:, tq:]).T
        ms = jnp.mean(a * a, axis=-1, keepdims=True)
        o_ref[0, qi * tq:(qi + 1) * tq, :] = (
            ((a * lax.rsqrt(ms + SUBLN_EPS)) * sw_ref[...])
            * (1.0 - LAMBDA_INIT)).astype(o_ref.dtype)

    assert len(stages) % group == 0 and n_diag % group == 0
    assert n_buf == (ahead + 1) * group

    def issue_scores(first, c=None):
        for g in range(first, min(first + group, len(stages))):
            qi, t, _ = stages[g]
            if c is None:
                if t == 0:
                    prepare(qi)
            else:
                scores(g, c)

    for first in range(0, ahead * group, group):
        issue_scores(first)
        for c in range(n_panels):
            issue_scores(first, c)
    for first in range(0, len(stages), group):
        gs = list(range(first, first + group))
        issue_scores(first + ahead * group)
        for c in range(n_panels):
            update(gs, c)
            issue_scores(first + ahead * group, c)
        qi, t, _ = stages[gs[-1]]
        if t == n_diag * (qi + 1) - 1:
            finish(qi)


def _attention(qkv, lam_vecs, subln_w, tq, tk, pw, group, ahead):
    b, s, w3 = qkv.shape
    w = w3 // 3
    h = w // V_DIM
    kern = functools.partial(_attn_kernel, tq=tq, tk=tk, pw=pw, group=group,
                             ahead=ahead)

    def head_block(first):
        return pl.BlockSpec((1, s, V_DIM), lambda bi, hi: (bi, 0, first + hi))

    per_q_tile = ([pltpu.VMEM((V_DIM, 2 * tq), BF16)] * 2
                  + [pltpu.VMEM((1, 2 * tq), F32)] * 2
                  + [pltpu.VMEM((V_DIM + BF16_SUBLANES, 2 * tq), F32)] * 2)
    return pl.pallas_call(
        kern,
        grid=(b, h),
        in_specs=[head_block(0), head_block(h), head_block(2 * h),
                  pl.BlockSpec((4, HEAD_DIM), lambda bi, hi: (0, 0)),
                  pl.BlockSpec((1, V_DIM), lambda bi, hi: (0, 0))],
        out_specs=head_block(0),
        out_shape=jax.ShapeDtypeStruct((b, s, w), BF16),
        scratch_shapes=[pltpu.VMEM((V_DIM + BF16_SUBLANES, s), BF16),
                        pltpu.VMEM((tk, pw), F32)]
        + per_q_tile
        + [pltpu.VMEM((tk, 2 * tq), F32)] * ((ahead + 1) * group),
        compiler_params=_params(("parallel", "parallel")),
        name="diff_attention",
    )(qkv, qkv, qkv, lam_vecs, subln_w)


def _out_kernel(a_ref, c_ref, wa_ref, wc_ref, x_ref, fnw_ref, o_ref, fn_ref, h_scr,
                *, rb):
    j = pl.program_id(1)
    n_j, tm, tn = h_scr.shape
    h = x_ref[...] + _mm(a_ref[...], wa_ref[...]) + _mm(c_ref[...], wc_ref[...])
    o_ref[...] = h
    h_scr[j] = h

    @pl.when(j == n_j - 1)
    def _():
        for r in range(tm // rb):
            rows = slice(r * rb, (r + 1) * rb)
            ssq = sum(jnp.sum(h_scr[jj, rows, :] * h_scr[jj, rows, :], axis=-1, keepdims=True)
                      for jj in range(n_j))
            scale = lax.rsqrt(ssq * (1.0 / (n_j * tn)) + EPS)
            for jj in range(n_j):
                cols = slice(jj * tn, (jj + 1) * tn)
                fn_ref[rows, cols] = ((h_scr[jj, rows, :] * scale)
                                      * fnw_ref[:, cols]).astype(fn_ref.dtype)


def _out_proj(attn, conv, w_out, x, ffn_norm_w, tm, tn, rb):
    t, ka = attn.shape
    kc = conv.shape[1]
    n = w_out.shape[1]
    assert ka == kc
    return pl.pallas_call(
        functools.partial(_out_kernel, rb=rb),
        grid=(t // tm, n // tn),
        in_specs=[pl.BlockSpec((tm, ka), lambda i, j: (i, 0)),
                  pl.BlockSpec((tm, kc), lambda i, j: (i, 0)),
                  pl.BlockSpec((ka, tn), lambda i, j: (0, j)),
                  pl.BlockSpec((kc, tn), lambda i, j: (1, j)),
                  pl.BlockSpec((tm, tn), lambda i, j: (i, j)),
                  pl.BlockSpec((1, n), lambda i, j: (0, 0))],
        out_specs=[pl.BlockSpec((tm, tn), lambda i, j: (i, j)),
                   pl.BlockSpec((tm, n), lambda i, j: (i, 0))],
        out_shape=[jax.ShapeDtypeStruct((t, n), F32),
                   jax.ShapeDtypeStruct((t, n), BF16)],
        scratch_shapes=[pltpu.VMEM((n // tn, tm, tn), F32)],
        compiler_params=_params(("arbitrary", "arbitrary")),
        name="out_proj",
    )(attn, conv, w_out, w_out, x, ffn_norm_w.reshape(1, n))


def _gate_up_kernel(x_ref, wg_ref, wu_ref, o_ref):
    x = x_ref[...]
    g = _mm(x, wg_ref[...])
    u = _mm(x, wu_ref[...])
    o_ref[...] = ((g * jax.nn.sigmoid(g)) * u).astype(o_ref.dtype)


def _gate_up(fn, w_gate, w_up, tm, tn):
    t, d = fn.shape
    f = w_gate.shape[1]
    return pl.pallas_call(
        _gate_up_kernel,
        grid=(t // tm, f // tn),
        in_specs=[pl.BlockSpec((tm, d), lambda i, j: (i, 0)),
                  pl.BlockSpec((d, tn), lambda i, j: (0, j)),
                  pl.BlockSpec((d, tn), lambda i, j: (0, j))],
        out_specs=pl.BlockSpec((tm, tn), lambda i, j: (i, j)),
        out_shape=jax.ShapeDtypeStruct((t, f), BF16),
        compiler_params=_params(("parallel", "parallel")),
        name="gate_up",
    )(fn, w_gate, w_up)


def _down_kernel(h_ref, w_ref, r_ref, o_ref):
    o_ref[...] = r_ref[...] + _mm(h_ref[...], w_ref[...])


def _down(h, w_down, resid, tm, tn):
    t, f = h.shape
    n = w_down.shape[1]
    return pl.pallas_call(
        _down_kernel,
        grid=(t // tm, n // tn),
        in_specs=[pl.BlockSpec((tm, f), lambda i, j: (i, 0)),
                  pl.BlockSpec((f, tn), lambda i, j: (0, j)),
                  pl.BlockSpec((tm, tn), lambda i, j: (i, j))],
        out_specs=pl.BlockSpec((tm, tn), lambda i, j: (i, j)),
        out_shape=jax.ShapeDtypeStruct((t, n), F32),
        compiler_params=_params(("parallel", "parallel")),
        name="down_proj",
    )(h, w_down, resid)


TILES = dict(
    proj_tm=1024, proj_tn=1024, proj_rb=256,
    conv_tm=1024, conv_tn=512, conv_rb=256,
    attn_tq=1024, attn_tk=256, attn_pw=256, attn_group=2, attn_ahead=1,
    out_tm=1024, out_tn=512, out_rb=256,
    gu_tm=1024, gu_tn=512,
    down_tm=1024, down_tn=256,
)


def _rope_tables(seq):
    pos = jnp.arange(seq, dtype=F32)
    inv_freq = ROPE_THETA ** (-jnp.arange(0, HEAD_DIM, 2, dtype=F32) / HEAD_DIM)
    ang = pos[:, None] * inv_freq[None, :]
    cos, sin = jnp.cos(ang), jnp.sin(ang)
    reps = LANES // (HEAD_DIM // 2)
    cos_l = jnp.tile(cos, (1, reps))
    sin_l = jnp.tile(jnp.concatenate([-sin, sin], axis=1), (1, reps // 2))
    return cos_l, sin_l


def kernel(x, attn_norm_w, w_in, q_norm_w, k_norm_w, lambda_q1, lambda_k1,
           lambda_q2, lambda_k2, subln_w, conv_w, w_out, ffn_norm_w, w_gate,
           w_up, w_down):
    b, s, d = x.shape
    t = b * s
    attn_w = N_HEADS * V_DIM
    qk_cols = N_HEADS * 2 * HEAD_DIM
    conv_width = d - attn_w
    col_v = 2 * qk_cols
    col_b = col_v + attn_w
    col_c = col_b + conv_width
    col_h = col_c + conv_width
    tl = TILES
    assert w_in.shape[0] == 1, "single-layer block"

    x2 = x.reshape(t, d)
    cos_l, sin_l = _rope_tables(s)
    scale = math.log2(math.e) / math.sqrt(HEAD_DIM)
    nw = jnp.concatenate([jnp.tile(q_norm_w[0] * scale, qk_cols // HEAD_DIM),
                          jnp.tile(k_norm_w[0], qk_cols // HEAD_DIM)]).reshape(1, -1)
    lam_vecs = jnp.concatenate([lambda_q1, lambda_k1, lambda_q2, lambda_k2], axis=0)

    qkv, xn = _qkv_proj(x2, attn_norm_w[0], w_in[0], nw, cos_l, sin_l, s, col_b,
                        tl["proj_tm"], tl["proj_tn"], tl["proj_rb"])
    conv = _conv_proj(xn, w_in[0], conv_w[0], col_b, col_c, col_h, conv_width, s,
                      tl["conv_tm"], tl["conv_tn"], tl["conv_rb"])

    attn = _attention(qkv.reshape(b, s, col_b), lam_vecs, subln_w[0].reshape(1, V_DIM),
                      tl["attn_tq"], tl["attn_tk"], tl["attn_pw"], tl["attn_group"],
                      tl["attn_ahead"])

    h1, fn = _out_proj(attn.reshape(t, attn_w), conv, w_out[0], x2, ffn_norm_w[0],
                       tl["out_tm"], tl["out_tn"], tl["out_rb"])
    hid = _gate_up(fn, w_gate[0], w_up[0], tl["gu_tm"], tl["gu_tn"])
    out = _down(hid, w_down[0], h1, tl["down_tm"], tl["down_tn"])
    return out.reshape(b, s, d)
```

```python
import functools
import math

import jax
import jax.numpy as jnp
from jax import lax
from jax.experimental import pallas as pl
from jax.experimental.pallas import tpu as pltpu

F32 = jnp.float32
BF16 = jnp.bfloat16

CHUNK = 64
HEAD_DIM = 64
V_DIM = 2 * HEAD_DIM
N_HEADS = 8
CONV_K = 3
ROPE_THETA = 10000.0
EPS = 1e-6
SUBLN_EPS = 1e-5
LAMBDA_INIT = 0.8 - 0.6 * math.exp(-0.3 * 0)
MASKED_SCORE = -1e30

LANES = 128
SUBLANES = 8
BF16_SUBLANES = 16
VMEM_LIMIT = 60 * 1024 * 1024


def _params(semantics):
    return pltpu.CompilerParams(dimension_semantics=semantics,
                                vmem_limit_bytes=VMEM_LIMIT)


def _mm(a_bf16, w_f32):
    return jnp.dot(a_bf16, w_f32.astype(BF16), preferred_element_type=F32)


def _rms_scale(x, w, eps):
    ms = jnp.mean(x * x, axis=-1, keepdims=True)
    return (x * lax.rsqrt(ms + eps)) * w


def _row_block_pipeline(n_blocks, matmul, epilogue):
    acc = matmul(0)
    for r in range(n_blocks):
        nxt = matmul(r + 1) if r + 1 < n_blocks else None
        epilogue(r, acc)
        acc = nxt


def _qkv_kernel(x_ref, anw_ref, w_ref, nw_ref, cos_ref, sin_ref, o_ref, xn_ref,
                w_scr, *, rb, n_qk_tiles):
    j = pl.program_id(1)
    tm, tn = o_ref.shape

    @pl.when(j == 0)
    def _():
        for r in range(tm // rb):
            rows = slice(r * rb, (r + 1) * rb)
            xn_ref[rows, :] = _rms_scale(x_ref[rows, :], anw_ref[...], EPS).astype(BF16)

    w_scr[...] = w_ref[...].astype(BF16)
    lane = lax.broadcasted_iota(jnp.int32, (1, LANES), 1)
    lo_half = lane < HEAD_DIM
    first = (lane % HEAD_DIM) < (HEAD_DIM // 2)

    def matmul(r):
        return jnp.dot(xn_ref[r * rb:(r + 1) * rb, :], w_scr[...],
                       preferred_element_type=F32)

    def plain(r, acc):
        o_ref[r * rb:(r + 1) * rb, :] = acc.astype(o_ref.dtype)

    def norm_rope(r, acc):
        rows = slice(r * rb, (r + 1) * rb)
        cos = cos_ref[rows, :]
        sin = sin_ref[rows, :]
        for c in range(tn // LANES):
            cols = slice(c * LANES, (c + 1) * LANES)
            y = acc[:, cols]
            sq = y * y
            s_lo = jnp.sum(jnp.where(lo_half, sq, 0.0), axis=-1, keepdims=True)
            s_hi = jnp.sum(jnp.where(lo_half, 0.0, sq), axis=-1, keepdims=True)
            ms = jnp.where(lo_half, s_lo, s_hi) * (1.0 / HEAD_DIM)
            yn = (y * lax.rsqrt(ms + EPS)) * nw_ref[:, cols]
            rot = jnp.where(first, pltpu.roll(yn, LANES - HEAD_DIM // 2, 1),
                            pltpu.roll(yn, HEAD_DIM // 2, 1))
            o_ref[rows, cols] = (yn * cos + rot * sin).astype(o_ref.dtype)

    @pl.when(j < n_qk_tiles)
    def _():
        _row_block_pipeline(tm // rb, matmul, norm_rope)

    @pl.when(j >= n_qk_tiles)
    def _():
        _row_block_pipeline(tm // rb, matmul, plain)


def _qkv_proj(x, attn_norm_w, w_in, nw, cos, sin, seq, n_out, tm, tn, rb):
    t, d = x.shape
    n_qk_tiles = nw.shape[1] // tn
    tiles_per_seq = seq // tm
    kern = functools.partial(_qkv_kernel, rb=rb, n_qk_tiles=n_qk_tiles)
    return pl.pallas_call(
        kern,
        grid=(t // tm, n_out // tn),
        in_specs=[pl.BlockSpec((tm, d), lambda i, j: (i, 0)),
                  pl.BlockSpec((1, d), lambda i, j: (0, 0)),
                  pl.BlockSpec((d, tn), lambda i, j: (0, j)),
                  pl.BlockSpec((1, tn), lambda i, j: (0, jnp.minimum(j, n_qk_tiles - 1))),
                  pl.BlockSpec((tm, LANES), lambda i, j: (i % tiles_per_seq, 0)),
                  pl.BlockSpec((tm, LANES), lambda i, j: (i % tiles_per_seq, 0))],
        out_specs=[pl.BlockSpec((tm, tn), lambda i, j: (i, j)),
                   pl.BlockSpec((tm, d), lambda i, j: (i, 0))],
        out_shape=[jax.ShapeDtypeStruct((t, n_out), BF16),
                   jax.ShapeDtypeStruct((t, d), BF16)],
        scratch_shapes=[pltpu.VMEM((d, tn), BF16)],
        compiler_params=_params(("arbitrary", "arbitrary")),
        name="qkv_proj",
    )(x, attn_norm_w.reshape(1, d), w_in, nw, cos, sin)


def _conv_kernel(xn_ref, wb_ref, wc_ref, wh_ref, cw_ref, o_ref,
                 wb_scr, wc_scr, wh_scr, u_scr, carry_scr, *, tiles_per_seq, rb):
    i = pl.program_id(1)
    tm = xn_ref.shape[0]

    @pl.when(i == 0)
    def _():
        wc_scr[...] = wc_ref[...].astype(BF16)
        wh_scr[...] = wh_ref[...].astype(BF16)
        wb_scr[...] = wb_ref[...].astype(BF16)

    @pl.when(i % tiles_per_seq == 0)
    def _():
        u_scr[0:SUBLANES, :] = jnp.zeros((SUBLANES, u_scr.shape[1]), F32)

    @pl.when(i % tiles_per_seq != 0)
    def _():
        u_scr[0:SUBLANES, :] = carry_scr[...]

    def xn_rows(r):
        return xn_ref[r * rb:(r + 1) * rb, :]

    def gate_matmuls(r):
        return (jnp.dot(xn_rows(r), wc_scr[...], preferred_element_type=F32),
                jnp.dot(xn_rows(r), wh_scr[...], preferred_element_type=F32))

    def store_u(r, acc):
        u_scr[SUBLANES + r * rb:SUBLANES + (r + 1) * rb, :] = acc[0] * acc[1]

    _row_block_pipeline(tm // rb, gate_matmuls, store_u)
    carry_scr[...] = u_scr[tm:tm + SUBLANES, :]
    cw = cw_ref[...]

    def b_matmul(r):
        return jnp.dot(xn_rows(r), wb_scr[...], preferred_element_type=F32)

    def conv_out(r, gate_b):
        def shifted(back):
            start = SUBLANES + r * rb - back
            return u_scr[start:start + rb, :]
        y = cw[2:3, :] * shifted(0) + cw[1:2, :] * shifted(1) + cw[0:1, :] * shifted(2)
        o_ref[r * rb:(r + 1) * rb, :] = (gate_b * y).astype(o_ref.dtype)

    _row_block_pipeline(tm // rb, b_matmul, conv_out)


def _conv_proj(xn, w_in, conv_w, col_b, col_c, col_h, n, seq, tm, tn, rb):
    t, d = xn.shape
    jb, jc, jh = col_b // tn, col_c // tn, col_h // tn
    nj = n // tn
    kern = functools.partial(_conv_kernel, tiles_per_seq=seq // tm, rb=rb)
    w_scratch = [pltpu.VMEM((d, tn), BF16)] * 3
    return pl.pallas_call(
        kern,
        grid=(nj, t // tm),
        in_specs=[pl.BlockSpec((tm, d), lambda j, i: (i, 0)),
                  pl.BlockSpec((d, tn), lambda j, i: (0, jb + j)),
                  pl.BlockSpec((d, tn), lambda j, i: (0, jc + j)),
                  pl.BlockSpec((d, tn), lambda j, i: (0, jh + j)),
                  pl.BlockSpec((CONV_K, tn), lambda j, i: (0, j))],
        out_specs=pl.BlockSpec((tm, tn), lambda j, i: (i, j)),
        out_shape=jax.ShapeDtypeStruct((t, n), BF16),
        scratch_shapes=w_scratch + [pltpu.VMEM((tm + SUBLANES, tn), F32),
                                    pltpu.VMEM((SUBLANES, tn), F32)],
        compiler_params=_params(("arbitrary", "arbitrary")),
        name="conv_proj",
    )(xn, w_in, w_in, w_in, conv_w)


def _attn_kernel(q_ref, k_ref, v_ref, lam_ref, sw_ref, o_ref,
                 vt_scr, bias_scr, qq_a, qq_b, m_a, m_b, acc_a, acc_b, *s_bufs,
                 tq, tk, pw, ahead):
    seq = k_ref.shape[1]
    n_q = seq // tq
    n_diag = tq // tk
    n_buf = len(s_bufs)
    n_panels = 2 * tq // pw
    assert ahead < n_buf
    all_visible = ("full",) * n_panels
    state = ((qq_a, m_a, acc_a), (qq_b, m_b, acc_b))

    for c in range(seq // LANES):
        blk = v_ref[0, c * LANES:(c + 1) * LANES, :].astype(F32)
        vt_scr[0:V_DIM, c * LANES:(c + 1) * LANES] = blk.T.astype(BF16)
    vt_scr[V_DIM:, :] = jnp.ones((vt_scr.shape[0] - V_DIM, seq), BF16)
    key = lax.broadcasted_iota(jnp.int32, bias_scr.shape, 0)
    qry = lax.broadcasted_iota(jnp.int32, bias_scr.shape, 1)
    bias_scr[...] = jnp.where(key // CHUNK <= qry // CHUNK, 0.0, MASKED_SCORE)

    lq1, lk1, lq2, lk2 = (lam_ref[r:r + 1, :] for r in range(4))
    lam = (jnp.exp(jnp.sum(lq1 * lk1, axis=-1, keepdims=True))
           - jnp.exp(jnp.sum(lq2 * lk2, axis=-1, keepdims=True)) + LAMBDA_INIT)

    def diag_modes(key_off):
        modes = []
        for c in range(n_panels):
            q_off = (c * pw) % tq
            if key_off + tk <= q_off:
                modes.append("full")
            elif key_off >= q_off + pw:
                modes.append("skip")
            else:
                assert key_off == q_off and tk == pw
                modes.append("tri")
        return tuple(modes)

    def prepare(qi):
        qq_scr, m_scr, acc_scr = state[qi % 2]
        dim = lax.broadcasted_iota(jnp.int32, (V_DIM, 1), 0)
        for c in range(tq // LANES):
            rows = slice(qi * tq + c * LANES, qi * tq + (c + 1) * LANES)
            qt = q_ref[0, rows, :].astype(F32).T
            cols = slice(c * LANES, (c + 1) * LANES)
            qq_scr[:, cols] = jnp.where(dim < HEAD_DIM, qt, 0.0).astype(BF16)
            cols = slice(tq + c * LANES, tq + (c + 1) * LANES)
            qq_scr[:, cols] = jnp.where(dim < HEAD_DIM, 0.0, qt).astype(BF16)
        m_scr[...] = jnp.full(m_scr.shape, -jnp.inf, F32)
        acc_scr[...] = jnp.zeros(acc_scr.shape, F32)

    stages = []
    for qi in range(n_q):
        n_full = n_diag * qi
        for t in range(n_full + n_diag):
            modes = all_visible if t < n_full else diag_modes((t - n_full) * tk)
            stages.append((qi, t, modes))

    def scores(g, c):
        qi, t, modes = stages[g]
        if modes[c] == "skip":
            return
        cols = slice(c * pw, (c + 1) * pw)
        s_bufs[g % n_buf][:, cols] = jnp.dot(
            k_ref[0, t * tk:(t + 1) * tk, :], state[qi % 2][0][:, cols],
            preferred_element_type=F32)

    def update(g, c):
        qi, t, modes = stages[g]
        if modes[c] == "skip":
            return
        _, m_scr, acc_scr = state[qi % 2]
        cols = slice(c * pw, (c + 1) * pw)
        s = s_bufs[g % n_buf][:, cols]
        if modes[c] == "tri":
            s = s + bias_scr[...]
        m_prev = m_scr[:, cols]
        m_new = jnp.maximum(m_prev, jnp.max(s, axis=0, keepdims=True))
        alpha = jnp.exp2(m_prev - m_new)
        p = jnp.exp2(s - m_new).astype(BF16)
        m_scr[:, cols] = m_new
        acc_scr[:, cols] = alpha * acc_scr[:, cols] + jnp.dot(
            vt_scr[:, t * tk:(t + 1) * tk], p, preferred_element_type=F32)

    def finish(qi):
        acc = state[qi % 2][2][...]
        o = acc[0:V_DIM, :] / acc[V_DIM:V_DIM + 1, :]
        a = (o[:, 0:tq] - lam * o[:, tq:]).T
        ms = jnp.mean(a * a, axis=-1, keepdims=True)
        o_ref[0, qi * tq:(qi + 1) * tq, :] = (
            ((a * lax.rsqrt(ms + SUBLN_EPS)) * sw_ref[...])
            * (1.0 - LAMBDA_INIT)).astype(o_ref.dtype)

    def start(g):
        qi, t, _ = stages[g]
        if t == 0:
            prepare(qi)

    for g in range(ahead):
        start(g)
        for c in range(n_panels):
            scores(g, c)
    for g, (qi, t, _) in enumerate(stages):
        if g + ahead < len(stages):
            start(g + ahead)
        for c in range(n_panels):
            update(g, c)
            if g + ahead < len(stages):
                scores(g + ahead, c)
        if t == n_diag * (qi + 1) - 1:
            finish(qi)


def _attention(qkv, lam_vecs, subln_w, tq, tk, pw, ahead):
    b, s, w3 = qkv.shape
    w = w3 // 3
    h = w // V_DIM
    kern = functools.partial(_attn_kernel, tq=tq, tk=tk, pw=pw, ahead=ahead)

    def head_block(first):
        return pl.BlockSpec((1, s, V_DIM), lambda bi, hi: (bi, 0, first + hi))

    per_q_tile = ([pltpu.VMEM((V_DIM, 2 * tq), BF16)] * 2
                  + [pltpu.VMEM((1, 2 * tq), F32)] * 2
                  + [pltpu.VMEM((V_DIM + BF16_SUBLANES, 2 * tq), F32)] * 2)
    return pl.pallas_call(
        kern,
        grid=(b, h),
        in_specs=[head_block(0), head_block(h), head_block(2 * h),
                  pl.BlockSpec((4, HEAD_DIM), lambda bi, hi: (0, 0)),
                  pl.BlockSpec((1, V_DIM), lambda bi, hi: (0, 0))],
        out_specs=head_block(0),
        out_shape=jax.ShapeDtypeStruct((b, s, w), BF16),
        scratch_shapes=[pltpu.VMEM((V_DIM + BF16_SUBLANES, s), BF16),
                        pltpu.VMEM((tk, pw), F32)]
        + per_q_tile
        + [pltpu.VMEM((tk, 2 * tq), F32)] * (ahead + 1),
        compiler_params=_params(("parallel", "parallel")),
        name="diff_attention",
    )(qkv, qkv, qkv, lam_vecs, subln_w)


def _out_kernel(a_ref, c_ref, w_ref, x_ref, fnw_ref, o_ref, fn_ref, w_scr, *, rb):
    @pl.when(pl.program_id(0) == 0)
    def _():
        w_scr[...] = w_ref[...].astype(BF16)

    tm = o_ref.shape[0]
    ka = a_ref.shape[1]

    def matmul(r):
        rows = slice(r * rb, (r + 1) * rb)
        return (x_ref[rows, :]
                + jnp.dot(a_ref[rows, :], w_scr[0:ka, :], preferred_element_type=F32)
                + jnp.dot(c_ref[rows, :], w_scr[ka:, :], preferred_element_type=F32))

    def store(r, h):
        rows = slice(r * rb, (r + 1) * rb)
        o_ref[rows, :] = h
        fn_ref[rows, :] = _rms_scale(h, fnw_ref[...], EPS).astype(fn_ref.dtype)

    _row_block_pipeline(tm // rb, matmul, store)


def _out_proj(attn, conv, w_out, x, ffn_norm_w, tm, rb):
    t, ka = attn.shape
    kc = conv.shape[1]
    k, n = w_out.shape
    assert ka + kc == k
    return pl.pallas_call(
        functools.partial(_out_kernel, rb=rb),
        grid=(t // tm,),
        in_specs=[pl.BlockSpec((tm, ka), lambda i: (i, 0)),
                  pl.BlockSpec((tm, kc), lambda i: (i, 0)),
                  pl.BlockSpec((k, n), lambda i: (0, 0), pipeline_mode=pl.Buffered(1)),
                  pl.BlockSpec((tm, n), lambda i: (i, 0)),
                  pl.BlockSpec((1, n), lambda i: (0, 0))],
        out_specs=[pl.BlockSpec((tm, n), lambda i: (i, 0)),
                   pl.BlockSpec((tm, n), lambda i: (i, 0))],
        out_shape=[jax.ShapeDtypeStruct((t, n), F32),
                   jax.ShapeDtypeStruct((t, n), BF16)],
        scratch_shapes=[pltpu.VMEM((k, n), BF16)],
        compiler_params=_params(("arbitrary",)),
        name="out_proj",
    )(attn, conv, w_out, x, ffn_norm_w.reshape(1, n))


def _gate_up_kernel(x_ref, wg_ref, wu_ref, o_ref):
    x = x_ref[...]
    g = _mm(x, wg_ref[...])
    u = _mm(x, wu_ref[...])
    o_ref[...] = ((g * jax.nn.sigmoid(g)) * u).astype(o_ref.dtype)


def _gate_up(fn, w_gate, w_up, tm, tn):
    t, d = fn.shape
    f = w_gate.shape[1]
    return pl.pallas_call(
        _gate_up_kernel,
        grid=(t // tm, f // tn),
        in_specs=[pl.BlockSpec((tm, d), lambda i, j: (i, 0)),
                  pl.BlockSpec((d, tn), lambda i, j: (0, j)),
                  pl.BlockSpec((d, tn), lambda i, j: (0, j))],
        out_specs=pl.BlockSpec((tm, tn), lambda i, j: (i, j)),
        out_shape=jax.ShapeDtypeStruct((t, f), BF16),
        compiler_params=_params(("parallel", "parallel")),
        name="gate_up",
    )(fn, w_gate, w_up)


def _down_kernel(h_ref, w_ref, r_ref, o_ref, w_scr):
    @pl.when(pl.program_id(1) == 0)
    def _():
        w_scr[...] = w_ref[...].astype(BF16)

    o_ref[...] = r_ref[...] + jnp.dot(h_ref[...], w_scr[...],
                                      preferred_element_type=F32)


def _down(h, w_down, resid, tm, tn):
    t, f = h.shape
    n = w_down.shape[1]
    return pl.pallas_call(
        _down_kernel,
        grid=(n // tn, t // tm),
        in_specs=[pl.BlockSpec((tm, f), lambda j, i: (i, 0)),
                  pl.BlockSpec((f, tn), lambda j, i: (0, j)),
                  pl.BlockSpec((tm, tn), lambda j, i: (i, j))],
        out_specs=pl.BlockSpec((tm, tn), lambda j, i: (i, j)),
        out_shape=jax.ShapeDtypeStruct((t, n), F32),
        scratch_shapes=[pltpu.VMEM((f, tn), BF16)],
        compiler_params=_params(("arbitrary", "arbitrary")),
        name="down_proj",
    )(h, w_down, resid)


TILES = dict(
    proj_tm=1024, proj_tn=1024, proj_rb=256,
    conv_tm=1024, conv_tn=512, conv_rb=256,
    attn_tq=1024, attn_tk=256, attn_pw=256, attn_ahead=3,
    out_tm=512, out_rb=256,
    gu_tm=1024, gu_tn=512,
    down_tm=512, down_tn=512,
)


def _rope_tables(seq):
    pos = jnp.arange(seq, dtype=F32)
    inv_freq = ROPE_THETA ** (-jnp.arange(0, HEAD_DIM, 2, dtype=F32) / HEAD_DIM)
    ang = pos[:, None] * inv_freq[None, :]
    cos, sin = jnp.cos(ang), jnp.sin(ang)
    reps = LANES // (HEAD_DIM // 2)
    cos_l = jnp.tile(cos, (1, reps))
    sin_l = jnp.tile(jnp.concatenate([-sin, sin], axis=1), (1, reps // 2))
    return cos_l, sin_l


def kernel(x, attn_norm_w, w_in, q_norm_w, k_norm_w, lambda_q1, lambda_k1,
           lambda_q2, lambda_k2, subln_w, conv_w, w_out, ffn_norm_w, w_gate,
           w_up, w_down):
    b, s, d = x.shape
    t = b * s
    attn_w = N_HEADS * V_DIM
    qk_cols = N_HEADS * 2 * HEAD_DIM
    conv_width = d - attn_w
    col_v = 2 * qk_cols
    col_b = col_v + attn_w
    col_c = col_b + conv_width
    col_h = col_c + conv_width
    tl = TILES
    assert w_in.shape[0] == 1, "single-layer block"

    x2 = x.reshape(t, d)
    cos_l, sin_l = _rope_tables(s)
    scale = math.log2(math.e) / math.sqrt(HEAD_DIM)
    nw = jnp.concatenate([jnp.tile(q_norm_w[0] * scale, qk_cols // HEAD_DIM),
                          jnp.tile(k_norm_w[0], qk_cols // HEAD_DIM)]).reshape(1, -1)
    lam_vecs = jnp.concatenate([lambda_q1, lambda_k1, lambda_q2, lambda_k2], axis=0)

    qkv, xn = _qkv_proj(x2, attn_norm_w[0], w_in[0], nw, cos_l, sin_l, s, col_b,
                        tl["proj_tm"], tl["proj_tn"], tl["proj_rb"])
    conv = _conv_proj(xn, w_in[0], conv_w[0], col_b, col_c, col_h, conv_width, s,
                      tl["conv_tm"], tl["conv_tn"], tl["conv_rb"])

    attn = _attention(qkv.reshape(b, s, col_b), lam_vecs, subln_w[0].reshape(1, V_DIM),
                      tl["attn_tq"], tl["attn_tk"], tl["attn_pw"], tl["attn_ahead"])

    h1, fn = _out_proj(attn.reshape(t, attn_w), conv, w_out[0], x2, ffn_norm_w[0],
                       tl["out_tm"], tl["out_rb"])
    hid = _gate_up(fn, w_gate[0], w_up[0], tl["gu_tm"], tl["gu_tn"])
    out = _down(hid, w_down[0], h1, tl["down_tm"], tl["down_tn"])
    return out.reshape(b, s, d)
```

```python
import functools
import math

import jax
import jax.numpy as jnp
from jax import lax
from jax.experimental import pallas as pl
from jax.experimental.pallas import tpu as pltpu

F32 = jnp.float32
BF16 = jnp.bfloat16

CHUNK = 64
HEAD_DIM = 64
V_DIM = 2 * HEAD_DIM
N_HEADS = 8
CONV_K = 3
ROPE_THETA = 10000.0
EPS = 1e-6
SUBLN_EPS = 1e-5
LAMBDA_INIT = 0.8 - 0.6 * math.exp(-0.3 * 0)
MASKED_SCORE = -1e30

LANES = 128
SUBLANES = 8
BF16_SUBLANES = 16
VMEM_LIMIT = 60 * 1024 * 1024


def _params(semantics):
    return pltpu.CompilerParams(dimension_semantics=semantics,
                                vmem_limit_bytes=VMEM_LIMIT)


def _mm(a_bf16, w_f32):
    return jnp.dot(a_bf16, w_f32.astype(BF16), preferred_element_type=F32)


def _rms_scale(x, w, eps):
    ms = jnp.mean(x * x, axis=-1, keepdims=True)
    return (x * lax.rsqrt(ms + eps)) * w


def _row_block_pipeline(n_blocks, matmul, epilogue):
    acc = matmul(0)
    for r in range(n_blocks):
        nxt = matmul(r + 1) if r + 1 < n_blocks else None
        epilogue(r, acc)
        acc = nxt


def _rmsnorm_kernel(x_ref, w_ref, o_ref):
    o_ref[...] = _rms_scale(x_ref[...], w_ref[...], EPS).astype(o_ref.dtype)


def _rmsnorm(x, w, tm):
    t, d = x.shape
    return pl.pallas_call(
        _rmsnorm_kernel,
        grid=(t // tm,),
        in_specs=[pl.BlockSpec((tm, d), lambda i: (i, 0)),
                  pl.BlockSpec((1, d), lambda i: (0, 0))],
        out_specs=pl.BlockSpec((tm, d), lambda i: (i, 0)),
        out_shape=jax.ShapeDtypeStruct((t, d), BF16),
        compiler_params=_params(("parallel",)),
        name="rmsnorm",
    )(x, w.reshape(1, d))


def _qkv_kernel(xn_ref, w_ref, nw_ref, cos_ref, sin_ref, o_ref, w_scr,
                *, rb, n_qk_tiles):
    j = pl.program_id(0)
    tm, tn = o_ref.shape

    @pl.when(pl.program_id(1) == 0)
    def _():
        w_scr[...] = w_ref[...].astype(BF16)

    lane = lax.broadcasted_iota(jnp.int32, (1, LANES), 1)
    lo_half = lane < HEAD_DIM
    first = (lane % HEAD_DIM) < (HEAD_DIM // 2)

    def matmul(r):
        return jnp.dot(xn_ref[r * rb:(r + 1) * rb, :], w_scr[...],
                       preferred_element_type=F32)

    def plain(r, acc):
        o_ref[r * rb:(r + 1) * rb, :] = acc.astype(o_ref.dtype)

    def norm_rope(r, acc):
        rows = slice(r * rb, (r + 1) * rb)
        cos = cos_ref[rows, :]
        sin = sin_ref[rows, :]
        for c in range(tn // LANES):
            cols = slice(c * LANES, (c + 1) * LANES)
            y = acc[:, cols]
            sq = y * y
            s_lo = jnp.sum(jnp.where(lo_half, sq, 0.0), axis=-1, keepdims=True)
            s_hi = jnp.sum(jnp.where(lo_half, 0.0, sq), axis=-1, keepdims=True)
            ms = jnp.where(lo_half, s_lo, s_hi) * (1.0 / HEAD_DIM)
            yn = (y * lax.rsqrt(ms + EPS)) * nw_ref[:, cols]
            rot = jnp.where(first, pltpu.roll(yn, LANES - HEAD_DIM // 2, 1),
                            pltpu.roll(yn, HEAD_DIM // 2, 1))
            o_ref[rows, cols] = (yn * cos + rot * sin).astype(o_ref.dtype)

    @pl.when(j < n_qk_tiles)
    def _():
        _row_block_pipeline(tm // rb, matmul, norm_rope)

    @pl.when(j >= n_qk_tiles)
    def _():
        _row_block_pipeline(tm // rb, matmul, plain)


def _qkv_proj(xn, w_in, nw, cos, sin, seq, n_out, tm, tn, rb):
    t, d = xn.shape
    n_qk_tiles = nw.shape[1] // tn
    tiles_per_seq = seq // tm
    kern = functools.partial(_qkv_kernel, rb=rb, n_qk_tiles=n_qk_tiles)
    return pl.pallas_call(
        kern,
        grid=(n_out // tn, t // tm),
        in_specs=[pl.BlockSpec((tm, d), lambda j, i: (i, 0)),
                  pl.BlockSpec((d, tn), lambda j, i: (0, j)),
                  pl.BlockSpec((1, tn), lambda j, i: (0, jnp.minimum(j, n_qk_tiles - 1))),
                  pl.BlockSpec((tm, LANES), lambda j, i: (i % tiles_per_seq, 0)),
                  pl.BlockSpec((tm, LANES), lambda j, i: (i % tiles_per_seq, 0))],
        out_specs=pl.BlockSpec((tm, tn), lambda j, i: (i, j)),
        out_shape=jax.ShapeDtypeStruct((t, n_out), BF16),
        scratch_shapes=[pltpu.VMEM((d, tn), BF16)],
        compiler_params=_params(("arbitrary", "arbitrary")),
        name="qkv_proj",
    )(xn, w_in, nw, cos, sin)


def _conv_kernel(xn_ref, wb_ref, wc_ref, wh_ref, cw_ref, o_ref,
                 wb_scr, wc_scr, wh_scr, u_scr, carry_scr, *, tiles_per_seq, rb):
    i = pl.program_id(1)
    tm = xn_ref.shape[0]

    @pl.when(i == 0)
    def _():
        wc_scr[...] = wc_ref[...].astype(BF16)
        wh_scr[...] = wh_ref[...].astype(BF16)
        wb_scr[...] = wb_ref[...].astype(BF16)

    @pl.when(i % tiles_per_seq == 0)
    def _():
        u_scr[0:SUBLANES, :] = jnp.zeros((SUBLANES, u_scr.shape[1]), F32)

    @pl.when(i % tiles_per_seq != 0)
    def _():
        u_scr[0:SUBLANES, :] = carry_scr[...]

    def xn_rows(r):
        return xn_ref[r * rb:(r + 1) * rb, :]

    def gate_matmuls(r):
        return (jnp.dot(xn_rows(r), wc_scr[...], preferred_element_type=F32),
                jnp.dot(xn_rows(r), wh_scr[...], preferred_element_type=F32))

    def store_u(r, acc):
        u_scr[SUBLANES + r * rb:SUBLANES + (r + 1) * rb, :] = acc[0] * acc[1]

    _row_block_pipeline(tm // rb, gate_matmuls, store_u)
    carry_scr[...] = u_scr[tm:tm + SUBLANES, :]
    cw = cw_ref[...]

    def b_matmul(r):
        return jnp.dot(xn_rows(r), wb_scr[...], preferred_element_type=F32)

    def conv_out(r, gate_b):
        def shifted(back):
            start = SUBLANES + r * rb - back
            return u_scr[start:start + rb, :]
        y = cw[2:3, :] * shifted(0) + cw[1:2, :] * shifted(1) + cw[0:1, :] * shifted(2)
        o_ref[r * rb:(r + 1) * rb, :] = (gate_b * y).astype(o_ref.dtype)

    _row_block_pipeline(tm // rb, b_matmul, conv_out)


def _conv_proj(xn, w_in, conv_w, col_b, col_c, col_h, n, seq, tm, tn, rb):
    t, d = xn.shape
    jb, jc, jh = col_b // tn, col_c // tn, col_h // tn
    nj = n // tn
    kern = functools.partial(_conv_kernel, tiles_per_seq=seq // tm, rb=rb)
    w_scratch = [pltpu.VMEM((d, tn), BF16)] * 3
    return pl.pallas_call(
        kern,
        grid=(nj, t // tm),
        in_specs=[pl.BlockSpec((tm, d), lambda j, i: (i, 0)),
                  pl.BlockSpec((d, tn), lambda j, i: (0, jb + j)),
                  pl.BlockSpec((d, tn), lambda j, i: (0, jc + j)),
                  pl.BlockSpec((d, tn), lambda j, i: (0, jh + j)),
                  pl.BlockSpec((CONV_K, tn), lambda j, i: (0, j))],
        out_specs=pl.BlockSpec((tm, tn), lambda j, i: (i, j)),
        out_shape=jax.ShapeDtypeStruct((t, n), BF16),
        scratch_shapes=w_scratch + [pltpu.VMEM((tm + SUBLANES, tn), F32),
                                    pltpu.VMEM((SUBLANES, tn), F32)],
        compiler_params=_params(("arbitrary", "arbitrary")),
        name="conv_proj",
    )(xn, w_in, w_in, w_in, conv_w)


def _attn_kernel(q_ref, k_ref, v_ref, lam_ref, sw_ref, o_ref,
                 vt_scr, bias_scr, qq_a, qq_b, m_a, m_b, acc_a, acc_b, *s_bufs,
                 tq, tk, pw, ahead):
    seq = k_ref.shape[1]
    n_q = seq // tq
    n_diag = tq // tk
    n_buf = len(s_bufs)
    n_panels = 2 * tq // pw
    assert ahead < n_buf
    all_visible = ("full",) * n_panels
    state = ((qq_a, m_a, acc_a), (qq_b, m_b, acc_b))

    for c in range(seq // LANES):
        blk = v_ref[0, c * LANES:(c + 1) * LANES, :].astype(F32)
        vt_scr[0:V_DIM, c * LANES:(c + 1) * LANES] = blk.T.astype(BF16)
    vt_scr[V_DIM:, :] = jnp.ones((vt_scr.shape[0] - V_DIM, seq), BF16)
    key = lax.broadcasted_iota(jnp.int32, bias_scr.shape, 0)
    qry = lax.broadcasted_iota(jnp.int32, bias_scr.shape, 1)
    bias_scr[...] = jnp.where(key // CHUNK <= qry // CHUNK, 0.0, MASKED_SCORE)

    lq1, lk1, lq2, lk2 = (lam_ref[r:r + 1, :] for r in range(4))
    lam = (jnp.exp(jnp.sum(lq1 * lk1, axis=-1, keepdims=True))
           - jnp.exp(jnp.sum(lq2 * lk2, axis=-1, keepdims=True)) + LAMBDA_INIT)

    def diag_modes(key_off):
        modes = []
        for c in range(n_panels):
            q_off = (c * pw) % tq
            if key_off + tk <= q_off:
                modes.append("full")
            elif key_off >= q_off + pw:
                modes.append("skip")
            else:
                assert key_off == q_off and tk == pw
                modes.append("tri")
        return tuple(modes)

    def prepare(qi):
        qq_scr, m_scr, acc_scr = state[qi % 2]
        dim = lax.broadcasted_iota(jnp.int32, (V_DIM, 1), 0)
        for c in range(tq // LANES):
            rows = slice(qi * tq + c * LANES, qi * tq + (c + 1) * LANES)
            qt = q_ref[0, rows, :].astype(F32).T
            cols = slice(c * LANES, (c + 1) * LANES)
            qq_scr[:, cols] = jnp.where(dim < HEAD_DIM, qt, 0.0).astype(BF16)
            cols = slice(tq + c * LANES, tq + (c + 1) * LANES)
            qq_scr[:, cols] = jnp.where(dim < HEAD_DIM, 0.0, qt).astype(BF16)
        m_scr[...] = jnp.full(m_scr.shape, -jnp.inf, F32)
        acc_scr[...] = jnp.zeros(acc_scr.shape, F32)

    stages = []
    for qi in range(n_q):
        n_full = n_diag * qi
        for t in range(n_full + n_diag):
            modes = all_visible if t < n_full else diag_modes((t - n_full) * tk)
            stages.append((qi, t, modes))

    def scores(g, c):
        qi, t, modes = stages[g]
        if modes[c] == "skip":
            return
        cols = slice(c * pw, (c + 1) * pw)
        s_bufs[g % n_buf][:, cols] = jnp.dot(
            k_ref[0, t * tk:(t + 1) * tk, :], state[qi % 2][0][:, cols],
            preferred_element_type=F32)

    def update(g, c):
        qi, t, modes = stages[g]
        if modes[c] == "skip":
            return
        _, m_scr, acc_scr = state[qi % 2]
        cols = slice(c * pw, (c + 1) * pw)
        s = s_bufs[g % n_buf][:, cols]
        if modes[c] == "tri":
            s = s + bias_scr[...]
        m_prev = m_scr[:, cols]
        m_new = jnp.maximum(m_prev, jnp.max(s, axis=0, keepdims=True))
        alpha = jnp.exp2(m_prev - m_new)
        p = jnp.exp2(s - m_new).astype(BF16)
        m_scr[:, cols] = m_new
        acc_scr[:, cols] = alpha * acc_scr[:, cols] + jnp.dot(
            vt_scr[:, t * tk:(t + 1) * tk], p, preferred_element_type=F32)

    def finish(qi):
        acc = state[qi % 2][2][...]
        o = acc[0:V_DIM, :] / acc[V_DIM:V_DIM + 1, :]
        a = (o[:, 0:tq] - lam * o[:, tq:]).T
        ms = jnp.mean(a * a, axis=-1, keepdims=True)
        o_ref[0, qi * tq:(qi + 1) * tq, :] = (
            ((a * lax.rsqrt(ms + SUBLN_EPS)) * sw_ref[...])
            * (1.0 - LAMBDA_INIT)).astype(o_ref.dtype)

    def start(g):
        qi, t, _ = stages[g]
        if t == 0:
            prepare(qi)

    for g in range(ahead):
        start(g)
        for c in range(n_panels):
            scores(g, c)
    for g, (qi, t, _) in enumerate(stages):
        if g + ahead < len(stages):
            start(g + ahead)
        for c in range(n_panels):
            update(g, c)
            if g + ahead < len(stages):
                scores(g + ahead, c)
        if t == n_diag * (qi + 1) - 1:
            finish(qi)


def _attention(qkv, lam_vecs, subln_w, tq, tk, pw, ahead):
    b, s, w3 = qkv.shape
    w = w3 // 3
    h = w // V_DIM
    kern = functools.partial(_attn_kernel, tq=tq, tk=tk, pw=pw, ahead=ahead)

    def head_block(first):
        return pl.BlockSpec((1, s, V_DIM), lambda bi, hi: (bi, 0, first + hi))

    per_q_tile = ([pltpu.VMEM((V_DIM, 2 * tq), BF16)] * 2
                  + [pltpu.VMEM((1, 2 * tq), F32)] * 2
                  + [pltpu.VMEM((V_DIM + BF16_SUBLANES, 2 * tq), F32)] * 2)
    return pl.pallas_call(
        kern,
        grid=(b, h),
        in_specs=[head_block(0), head_block(h), head_block(2 * h),
                  pl.BlockSpec((4, HEAD_DIM), lambda bi, hi: (0, 0)),
                  pl.BlockSpec((1, V_DIM), lambda bi, hi: (0, 0))],
        out_specs=head_block(0),
        out_shape=jax.ShapeDtypeStruct((b, s, w), BF16),
        scratch_shapes=[pltpu.VMEM((V_DIM + BF16_SUBLANES, s), BF16),
                        pltpu.VMEM((tk, pw), F32)]
        + per_q_tile
        + [pltpu.VMEM((tk, 2 * tq), F32)] * (ahead + 1),
        compiler_params=_params(("parallel", "parallel")),
        name="diff_attention",
    )(qkv, qkv, qkv, lam_vecs, subln_w)


def _out_kernel(a_ref, c_ref, w_ref, x_ref, fnw_ref, o_ref, fn_ref, w_scr, *, rb):
    @pl.when(pl.program_id(0) == 0)
    def _():
        w_scr[...] = w_ref[...].astype(BF16)

    tm = o_ref.shape[0]
    ka = a_ref.shape[1]

    def matmul(r):
        rows = slice(r * rb, (r + 1) * rb)
        return (x_ref[rows, :]
                + jnp.dot(a_ref[rows, :], w_scr[0:ka, :], preferred_element_type=F32)
                + jnp.dot(c_ref[rows, :], w_scr[ka:, :], preferred_element_type=F32))

    def store(r, h):
        rows = slice(r * rb, (r + 1) * rb)
        o_ref[rows, :] = h
        fn_ref[rows, :] = _rms_scale(h, fnw_ref[...], EPS).astype(fn_ref.dtype)

    _row_block_pipeline(tm // rb, matmul, store)


def _out_proj(attn, conv, w_out, x, ffn_norm_w, tm, rb):
    t, ka = attn.shape
    kc = conv.shape[1]
    k, n = w_out.shape
    assert ka + kc == k
    return pl.pallas_call(
        functools.partial(_out_kernel, rb=rb),
        grid=(t // tm,),
        in_specs=[pl.BlockSpec((tm, ka), lambda i: (i, 0)),
                  pl.BlockSpec((tm, kc), lambda i: (i, 0)),
                  pl.BlockSpec((k, n), lambda i: (0, 0), pipeline_mode=pl.Buffered(1)),
                  pl.BlockSpec((tm, n), lambda i: (i, 0)),
                  pl.BlockSpec((1, n), lambda i: (0, 0))],
        out_specs=[pl.BlockSpec((tm, n), lambda i: (i, 0)),
                   pl.BlockSpec((tm, n), lambda i: (i, 0))],
        out_shape=[jax.ShapeDtypeStruct((t, n), F32),
                   jax.ShapeDtypeStruct((t, n), BF16)],
        scratch_shapes=[pltpu.VMEM((k, n), BF16)],
        compiler_params=_params(("arbitrary",)),
        name="out_proj",
    )(attn, conv, w_out, x, ffn_norm_w.reshape(1, n))


def _gate_up_kernel(x_ref, wg_ref, wu_ref, o_ref):
    x = x_ref[...]
    g = _mm(x, wg_ref[...])
    u = _mm(x, wu_ref[...])
    o_ref[...] = ((g * jax.nn.sigmoid(g)) * u).astype(o_ref.dtype)


def _gate_up(fn, w_gate, w_up, tm, tn):
    t, d = fn.shape
    f = w_gate.shape[1]
    return pl.pallas_call(
        _gate_up_kernel,
        grid=(t // tm, f // tn),
        in_specs=[pl.BlockSpec((tm, d), lambda i, j: (i, 0)),
                  pl.BlockSpec((d, tn), lambda i, j: (0, j)),
                  pl.BlockSpec((d, tn), lambda i, j: (0, j))],
        out_specs=pl.BlockSpec((tm, tn), lambda i, j: (i, j)),
        out_shape=jax.ShapeDtypeStruct((t, f), BF16),
        compiler_params=_params(("parallel", "parallel")),
        name="gate_up",
    )(fn, w_gate, w_up)


def _down_kernel(h_ref, w_ref, r_ref, o_ref, w_scr):
    @pl.when(pl.program_id(1) == 0)
    def _():
        w_scr[...] = w_ref[...].astype(BF16)

    o_ref[...] = r_ref[...] + jnp.dot(h_ref[...], w_scr[...],
                                      preferred_element_type=F32)


def _down(h, w_down, resid, tm, tn):
    t, f = h.shape
    n = w_down.shape[1]
    return pl.pallas_call(
        _down_kernel,
        grid=(n // tn, t // tm),
        in_specs=[pl.BlockSpec((tm, f), lambda j, i: (i, 0)),
                  pl.BlockSpec((f, tn), lambda j, i: (0, j)),
                  pl.BlockSpec((tm, tn), lambda j, i: (i, j))],
        out_specs=pl.BlockSpec((tm, tn), lambda j, i: (i, j)),
        out_shape=jax.ShapeDtypeStruct((t, n), F32),
        scratch_shapes=[pltpu.VMEM((f, tn), BF16)],
        compiler_params=_params(("arbitrary", "arbitrary")),
        name="down_proj",
    )(h, w_down, resid)


TILES = dict(
    norm_tm=512,
    proj_tm=1024, proj_tn=1024, proj_rb=256,
    conv_tm=1024, conv_tn=512, conv_rb=256,
    attn_tq=1024, attn_tk=256, attn_pw=256, attn_ahead=3,
    out_tm=512, out_rb=256,
    gu_tm=1024, gu_tn=512,
    down_tm=512, down_tn=512,
)


def _rope_tables(seq):
    pos = jnp.arange(seq, dtype=F32)
    inv_freq = ROPE_THETA ** (-jnp.arange(0, HEAD_DIM, 2, dtype=F32) / HEAD_DIM)
    ang = pos[:, None] * inv_freq[None, :]
    cos, sin = jnp.cos(ang), jnp.sin(ang)
    reps = LANES // (HEAD_DIM // 2)
    cos_l = jnp.tile(cos, (1, reps))
    sin_l = jnp.tile(jnp.concatenate([-sin, sin], axis=1), (1, reps // 2))
    return cos_l, sin_l


def kernel(x, attn_norm_w, w_in, q_norm_w, k_norm_w, lambda_q1, lambda_k1,
           lambda_q2, lambda_k2, subln_w, conv_w, w_out, ffn_norm_w, w_gate,
           w_up, w_down):
    b, s, d = x.shape
    t = b * s
    attn_w = N_HEADS * V_DIM
    qk_cols = N_HEADS * 2 * HEAD_DIM
    conv_width = d - attn_w
    col_v = 2 * qk_cols
    col_b = col_v + attn_w
    col_c = col_b + conv_width
    col_h = col_c + conv_width
    tl = TILES
    assert w_in.shape[0] == 1, "single-layer block"

    x2 = x.reshape(t, d)
    cos_l, sin_l = _rope_tables(s)
    scale = math.log2(math.e) / math.sqrt(HEAD_DIM)
    nw = jnp.concatenate([jnp.tile(q_norm_w[0] * scale, qk_cols // HEAD_DIM),
                          jnp.tile(k_norm_w[0], qk_cols // HEAD_DIM)]).reshape(1, -1)
    lam_vecs = jnp.concatenate([lambda_q1, lambda_k1, lambda_q2, lambda_k2], axis=0)

    xn = _rmsnorm(x2, attn_norm_w[0], tl["norm_tm"])
    qkv = _qkv_proj(xn, w_in[0], nw, cos_l, sin_l, s, col_b,
                    tl["proj_tm"], tl["proj_tn"], tl["proj_rb"])
    conv = _conv_proj(xn, w_in[0], conv_w[0], col_b, col_c, col_h, conv_width, s,
                      tl["conv_tm"], tl["conv_tn"], tl["conv_rb"])

    attn = _attention(qkv.reshape(b, s, col_b), lam_vecs, subln_w[0].reshape(1, V_DIM),
                      tl["attn_tq"], tl["attn_tk"], tl["attn_pw"], tl["attn_ahead"])

    h1, fn = _out_proj(attn.reshape(t, attn_w), conv, w_out[0], x2, ffn_norm_w[0],
                       tl["out_tm"], tl["out_rb"])
    hid = _gate_up(fn, w_gate[0], w_up[0], tl["gu_tm"], tl["gu_tn"])
    out = _down(hid, w_down[0], h1, tl["down_tm"], tl["down_tn"])
    return out.reshape(b, s, d)
```

```python
import functools
import math

import jax
import jax.numpy as jnp
from jax import lax
from jax.experimental import pallas as pl
from jax.experimental.pallas import tpu as pltpu

F32 = jnp.float32
BF16 = jnp.bfloat16

CHUNK = 64
HEAD_DIM = 64
V_DIM = 2 * HEAD_DIM
N_HEADS = 8
CONV_K = 3
ROPE_THETA = 10000.0
EPS = 1e-6
SUBLN_EPS = 1e-5
LAMBDA_INIT = 0.8 - 0.6 * math.exp(-0.3 * 0)
MASKED_SCORE = -1e30

LANES = 128
SUBLANES = 8
BF16_SUBLANES = 16
VMEM_LIMIT = 60 * 1024 * 1024


def _params(semantics):
    return pltpu.CompilerParams(dimension_semantics=semantics,
                                vmem_limit_bytes=VMEM_LIMIT)


def _mm(a_bf16, w_f32):
    return jnp.dot(a_bf16, w_f32.astype(BF16), preferred_element_type=F32)


def _rms_scale(x, w, eps):
    ms = jnp.mean(x * x, axis=-1, keepdims=True)
    return (x * lax.rsqrt(ms + eps)) * w


def _row_block_pipeline(n_blocks, matmul, epilogue):
    acc = matmul(0)
    for r in range(n_blocks):
        nxt = matmul(r + 1) if r + 1 < n_blocks else None
        epilogue(r, acc)
        acc = nxt


def _qkv_kernel(x_ref, anw_ref, w_ref, nw_ref, cos_ref, sin_ref, o_ref, xn_ref, w_scr,
                *, rb, n_qk_tiles):
    j = pl.program_id(0)
    tm, tn = o_ref.shape

    @pl.when(pl.program_id(1) == 0)
    def _():
        w_scr[...] = w_ref[...].astype(BF16)

    lane = lax.broadcasted_iota(jnp.int32, (1, LANES), 1)
    lo_half = lane < HEAD_DIM
    first = (lane % HEAD_DIM) < (HEAD_DIM // 2)

    def matmul(r):
        rows = slice(r * rb, (r + 1) * rb)
        xn = _rms_scale(x_ref[rows, :], anw_ref[...], EPS).astype(BF16)
        xn_ref[rows, :] = xn
        return jnp.dot(xn, w_scr[...], preferred_element_type=F32)

    def plain(r, acc):
        o_ref[r * rb:(r + 1) * rb, :] = acc.astype(o_ref.dtype)

    def norm_rope(r, acc):
        rows = slice(r * rb, (r + 1) * rb)
        cos = cos_ref[rows, :]
        sin = sin_ref[rows, :]
        for c in range(tn // LANES):
            cols = slice(c * LANES, (c + 1) * LANES)
            y = acc[:, cols]
            sq = y * y
            s_lo = jnp.sum(jnp.where(lo_half, sq, 0.0), axis=-1, keepdims=True)
            s_hi = jnp.sum(jnp.where(lo_half, 0.0, sq), axis=-1, keepdims=True)
            ms = jnp.where(lo_half, s_lo, s_hi) * (1.0 / HEAD_DIM)
            yn = (y * lax.rsqrt(ms + EPS)) * nw_ref[:, cols]
            rot = jnp.where(first, pltpu.roll(yn, LANES - HEAD_DIM // 2, 1),
                            pltpu.roll(yn, HEAD_DIM // 2, 1))
            o_ref[rows, cols] = (yn * cos + rot * sin).astype(o_ref.dtype)

    @pl.when(j < n_qk_tiles)
    def _():
        _row_block_pipeline(tm // rb, matmul, norm_rope)

    @pl.when(j >= n_qk_tiles)
    def _():
        _row_block_pipeline(tm // rb, matmul, plain)


def _qkv_proj(x, attn_norm_w, w_in, nw, cos, sin, seq, n_out, tm, tn, rb):
    t, d = x.shape
    n_i = t // tm
    n_qk_tiles = nw.shape[1] // tn
    tiles_per_seq = seq // tm
    kern = functools.partial(_qkv_kernel, rb=rb, n_qk_tiles=n_qk_tiles)
    return pl.pallas_call(
        kern,
        grid=(n_out // tn, n_i),
        in_specs=[pl.BlockSpec((tm, d), lambda j, i: (i, 0)),
                  pl.BlockSpec((1, d), lambda j, i: (0, 0)),
                  pl.BlockSpec((d, tn), lambda j, i: (0, j)),
                  pl.BlockSpec((1, tn), lambda j, i: (0, jnp.minimum(j, n_qk_tiles - 1))),
                  pl.BlockSpec((tm, LANES), lambda j, i: (i % tiles_per_seq, 0)),
                  pl.BlockSpec((tm, LANES), lambda j, i: (i % tiles_per_seq, 0))],
        out_specs=[pl.BlockSpec((tm, tn), lambda j, i: (i, j)),
                   pl.BlockSpec((tm, d), lambda j, i: (jnp.where(j == 0, i, n_i), 0))],
        out_shape=[jax.ShapeDtypeStruct((t, n_out), BF16),
                   jax.ShapeDtypeStruct((t + tm, d), BF16)],
        scratch_shapes=[pltpu.VMEM((d, tn), BF16)],
        compiler_params=_params(("arbitrary", "arbitrary")),
        name="qkv_proj",
    )(x, attn_norm_w.reshape(1, d), w_in, nw, cos, sin)


def _conv_kernel(xn_ref, wb_ref, wc_ref, wh_ref, cw_ref, o_ref,
                 wb_scr, wc_scr, wh_scr, u_scr, carry_scr, *, tiles_per_seq, rb):
    i = pl.program_id(1)
    tm = xn_ref.shape[0]

    @pl.when(i == 0)
    def _():
        wc_scr[...] = wc_ref[...].astype(BF16)
        wh_scr[...] = wh_ref[...].astype(BF16)
        wb_scr[...] = wb_ref[...].astype(BF16)

    @pl.when(i % tiles_per_seq == 0)
    def _():
        u_scr[0:SUBLANES, :] = jnp.zeros((SUBLANES, u_scr.shape[1]), F32)

    @pl.when(i % tiles_per_seq != 0)
    def _():
        u_scr[0:SUBLANES, :] = carry_scr[...]

    def xn_rows(r):
        return xn_ref[r * rb:(r + 1) * rb, :]

    def gate_matmuls(r):
        return (jnp.dot(xn_rows(r), wc_scr[...], preferred_element_type=F32),
                jnp.dot(xn_rows(r), wh_scr[...], preferred_element_type=F32))

    def store_u(r, acc):
        u_scr[SUBLANES + r * rb:SUBLANES + (r + 1) * rb, :] = acc[0] * acc[1]

    _row_block_pipeline(tm // rb, gate_matmuls, store_u)
    carry_scr[...] = u_scr[tm:tm + SUBLANES, :]
    cw = cw_ref[...]

    def b_matmul(r):
        return jnp.dot(xn_rows(r), wb_scr[...], preferred_element_type=F32)

    def conv_out(r, gate_b):
        def shifted(back):
            start = SUBLANES + r * rb - back
            return u_scr[start:start + rb, :]
        y = cw[2:3, :] * shifted(0) + cw[1:2, :] * shifted(1) + cw[0:1, :] * shifted(2)
        o_ref[r * rb:(r + 1) * rb, :] = (gate_b * y).astype(o_ref.dtype)

    _row_block_pipeline(tm // rb, b_matmul, conv_out)


def _conv_proj(xn, t, w_in, conv_w, col_b, col_c, col_h, n, seq, tm, tn, rb):
    d = xn.shape[1]
    jb, jc, jh = col_b // tn, col_c // tn, col_h // tn
    nj = n // tn
    kern = functools.partial(_conv_kernel, tiles_per_seq=seq // tm, rb=rb)
    w_scratch = [pltpu.VMEM((d, tn), BF16)] * 3
    return pl.pallas_call(
        kern,
        grid=(nj, t // tm),
        in_specs=[pl.BlockSpec((tm, d), lambda j, i: (i, 0)),
                  pl.BlockSpec((d, tn), lambda j, i: (0, jb + j)),
                  pl.BlockSpec((d, tn), lambda j, i: (0, jc + j)),
                  pl.BlockSpec((d, tn), lambda j, i: (0, jh + j)),
                  pl.BlockSpec((CONV_K, tn), lambda j, i: (0, j))],
        out_specs=pl.BlockSpec((tm, tn), lambda j, i: (i, j)),
        out_shape=jax.ShapeDtypeStruct((t, n), BF16),
        scratch_shapes=w_scratch + [pltpu.VMEM((tm + SUBLANES, tn), F32),
                                    pltpu.VMEM((SUBLANES, tn), F32)],
        compiler_params=_params(("arbitrary", "arbitrary")),
        name="conv_proj",
    )(xn, w_in, w_in, w_in, conv_w)


def _attn_kernel(q_ref, k_ref, v_ref, lam_ref, sw_ref, o_ref,
                 vt_scr, bias_scr, qq_a, qq_b, m_a, m_b, acc_a, acc_b, *s_bufs,
                 tq, tk, pw, ahead):
    seq = k_ref.shape[1]
    n_q = seq // tq
    n_diag = tq // tk
    n_buf = len(s_bufs)
    n_panels = 2 * tq // pw
    assert ahead < n_buf
    all_visible = ("full",) * n_panels
    state = ((qq_a, m_a, acc_a), (qq_b, m_b, acc_b))

    for c in range(seq // LANES):
        blk = v_ref[0, c * LANES:(c + 1) * LANES, :].astype(F32)
        vt_scr[0:V_DIM, c * LANES:(c + 1) * LANES] = blk.T.astype(BF16)
    vt_scr[V_DIM:, :] = jnp.ones((vt_scr.shape[0] - V_DIM, seq), BF16)
    key = lax.broadcasted_iota(jnp.int32, bias_scr.shape, 0)
    qry = lax.broadcasted_iota(jnp.int32, bias_scr.shape, 1)
    bias_scr[...] = jnp.where(key // CHUNK <= qry // CHUNK, 0.0, MASKED_SCORE)

    lq1, lk1, lq2, lk2 = (lam_ref[r:r + 1, :] for r in range(4))
    lam = (jnp.exp(jnp.sum(lq1 * lk1, axis=-1, keepdims=True))
           - jnp.exp(jnp.sum(lq2 * lk2, axis=-1, keepdims=True)) + LAMBDA_INIT)

    def diag_modes(key_off):
        modes = []
        for c in range(n_panels):
            q_off = (c * pw) % tq
            if key_off + tk <= q_off:
                modes.append("full")
            elif key_off >= q_off + pw:
                modes.append("skip")
            else:
                assert key_off == q_off and tk == pw
                modes.append("tri")
        return tuple(modes)

    def prepare(qi):
        qq_scr, m_scr, acc_scr = state[qi % 2]
        dim = lax.broadcasted_iota(jnp.int32, (V_DIM, 1), 0)
        for c in range(tq // LANES):
            rows = slice(qi * tq + c * LANES, qi * tq + (c + 1) * LANES)
            qt = q_ref[0, rows, :].astype(F32).T
            cols = slice(c * LANES, (c + 1) * LANES)
            qq_scr[:, cols] = jnp.where(dim < HEAD_DIM, qt, 0.0).astype(BF16)
            cols = slice(tq + c * LANES, tq + (c + 1) * LANES)
            qq_scr[:, cols] = jnp.where(dim < HEAD_DIM, 0.0, qt).astype(BF16)
        m_scr[...] = jnp.full(m_scr.shape, -jnp.inf, F32)
        acc_scr[...] = jnp.zeros(acc_scr.shape, F32)

    stages = []
    for qi in range(n_q):
        n_full = n_diag * qi
        for t in range(n_full + n_diag):
            modes = all_visible if t < n_full else diag_modes((t - n_full) * tk)
            stages.append((qi, t, modes))

    def scores(g, c):
        qi, t, modes = stages[g]
        if modes[c] == "skip":
            return
        cols = slice(c * pw, (c + 1) * pw)
        s_bufs[g % n_buf][:, cols] = jnp.dot(
            k_ref[0, t * tk:(t + 1) * tk, :], state[qi % 2][0][:, cols],
            preferred_element_type=F32)

    def update(g, c):
        qi, t, modes = stages[g]
        if modes[c] == "skip":
            return
        _, m_scr, acc_scr = state[qi % 2]
        cols = slice(c * pw, (c + 1) * pw)
        s = s_bufs[g % n_buf][:, cols]
        if modes[c] == "tri":
            s = s + bias_scr[...]
        m_prev = m_scr[:, cols]
        m_new = jnp.maximum(m_prev, jnp.max(s, axis=0, keepdims=True))
        alpha = jnp.exp2(m_prev - m_new)
        p = jnp.exp2(s - m_new).astype(BF16)
        m_scr[:, cols] = m_new
        acc_scr[:, cols] = alpha * acc_scr[:, cols] + jnp.dot(
            vt_scr[:, t * tk:(t + 1) * tk], p, preferred_element_type=F32)

    def finish(qi):
        acc = state[qi % 2][2][...]
        o = acc[0:V_DIM, :] / acc[V_DIM:V_DIM + 1, :]
        a = (o[:, 0:tq] - lam * o[:, tq:]).T
        ms = jnp.mean(a * a, axis=-1, keepdims=True)
        o_ref[0, qi * tq:(qi + 1) * tq, :] = (
            ((a * lax.rsqrt(ms + SUBLN_EPS)) * sw_ref[...])
            * (1.0 - LAMBDA_INIT)).astype(o_ref.dtype)

    def start(g):
        qi, t, _ = stages[g]
        if t == 0:
            prepare(qi)

    for g in range(ahead):
        start(g)
        for c in range(n_panels):
            scores(g, c)
    for g, (qi, t, _) in enumerate(stages):
        if g + ahead < len(stages):
            start(g + ahead)
        for c in range(n_panels):
            update(g, c)
            if g + ahead < len(stages):
                scores(g + ahead, c)
        if t == n_diag * (qi + 1) - 1:
            finish(qi)


def _attention(qkv, lam_vecs, subln_w, tq, tk, pw, ahead):
    b, s, w3 = qkv.shape
    w = w3 // 3
    h = w // V_DIM
    kern = functools.partial(_attn_kernel, tq=tq, tk=tk, pw=pw, ahead=ahead)

    def head_block(first):
        return pl.BlockSpec((1, s, V_DIM), lambda bi, hi: (bi, 0, first + hi))

    per_q_tile = ([pltpu.VMEM((V_DIM, 2 * tq), BF16)] * 2
                  + [pltpu.VMEM((1, 2 * tq), F32)] * 2
                  + [pltpu.VMEM((V_DIM + BF16_SUBLANES, 2 * tq), F32)] * 2)
    return pl.pallas_call(
        kern,
        grid=(b, h),
        in_specs=[head_block(0), head_block(h), head_block(2 * h),
                  pl.BlockSpec((4, HEAD_DIM), lambda bi, hi: (0, 0)),
                  pl.BlockSpec((1, V_DIM), lambda bi, hi: (0, 0))],
        out_specs=head_block(0),
        out_shape=jax.ShapeDtypeStruct((b, s, w), BF16),
        scratch_shapes=[pltpu.VMEM((V_DIM + BF16_SUBLANES, s), BF16),
                        pltpu.VMEM((tk, pw), F32)]
        + per_q_tile
        + [pltpu.VMEM((tk, 2 * tq), F32)] * (ahead + 1),
        compiler_params=_params(("parallel", "parallel")),
        name="diff_attention",
    )(qkv, qkv, qkv, lam_vecs, subln_w)


def _out_kernel(a_ref, c_ref, w_ref, x_ref, fnw_ref, o_ref, fn_ref, w_scr, *, rb):
    @pl.when(pl.program_id(0) == 0)
    def _():
        w_scr[...] = w_ref[...].astype(BF16)

    tm = o_ref.shape[0]
    ka = a_ref.shape[1]

    def matmul(r):
        rows = slice(r * rb, (r + 1) * rb)
        return (x_ref[rows, :]
                + jnp.dot(a_ref[rows, :], w_scr[0:ka, :], preferred_element_type=F32)
                + jnp.dot(c_ref[rows, :], w_scr[ka:, :], preferred_element_type=F32))

    def store(r, h):
        rows = slice(r * rb, (r + 1) * rb)
        o_ref[rows, :] = h
        fn_ref[rows, :] = _rms_scale(h, fnw_ref[...], EPS).astype(fn_ref.dtype)

    _row_block_pipeline(tm // rb, matmul, store)


def _out_proj(attn, conv, w_out, x, ffn_norm_w, tm, rb):
    t, ka = attn.shape
    kc = conv.shape[1]
    k, n = w_out.shape
    assert ka + kc == k
    return pl.pallas_call(
        functools.partial(_out_kernel, rb=rb),
        grid=(t // tm,),
        in_specs=[pl.BlockSpec((tm, ka), lambda i: (i, 0)),
                  pl.BlockSpec((tm, kc), lambda i: (i, 0)),
                  pl.BlockSpec((k, n), lambda i: (0, 0), pipeline_mode=pl.Buffered(1)),
                  pl.BlockSpec((tm, n), lambda i: (i, 0)),
                  pl.BlockSpec((1, n), lambda i: (0, 0))],
        out_specs=[pl.BlockSpec((tm, n), lambda i: (i, 0)),
                   pl.BlockSpec((tm, n), lambda i: (i, 0))],
        out_shape=[jax.ShapeDtypeStruct((t, n), F32),
                   jax.ShapeDtypeStruct((t, n), BF16)],
        scratch_shapes=[pltpu.VMEM((k, n), BF16)],
        compiler_params=_params(("arbitrary",)),
        name="out_proj",
    )(attn, conv, w_out, x, ffn_norm_w.reshape(1, n))


def _gate_up_kernel(x_ref, wg_ref, wu_ref, o_ref):
    x = x_ref[...]
    g = _mm(x, wg_ref[...])
    u = _mm(x, wu_ref[...])
    o_ref[...] = ((g * jax.nn.sigmoid(g)) * u).astype(o_ref.dtype)


def _gate_up(fn, w_gate, w_up, tm, tn):
    t, d = fn.shape
    f = w_gate.shape[1]
    return pl.pallas_call(
        _gate_up_kernel,
        grid=(t // tm, f // tn),
        in_specs=[pl.BlockSpec((tm, d), lambda i, j: (i, 0)),
                  pl.BlockSpec((d, tn), lambda i, j: (0, j)),
                  pl.BlockSpec((d, tn), lambda i, j: (0, j))],
        out_specs=pl.BlockSpec((tm, tn), lambda i, j: (i, j)),
        out_shape=jax.ShapeDtypeStruct((t, f), BF16),
        compiler_params=_params(("parallel", "parallel")),
        name="gate_up",
    )(fn, w_gate, w_up)


def _down_kernel(h_ref, w_ref, r_ref, o_ref, w_scr):
    @pl.when(pl.program_id(1) == 0)
    def _():
        w_scr[...] = w_ref[...].astype(BF16)

    o_ref[...] = r_ref[...] + jnp.dot(h_ref[...], w_scr[...],
                                      preferred_element_type=F32)


def _down(h, w_down, resid, tm, tn):
    t, f = h.shape
    n = w_down.shape[1]
    return pl.pallas_call(
        _down_kernel,
        grid=(n // tn, t // tm),
        in_specs=[pl.BlockSpec((tm, f), lambda j, i: (i, 0)),
                  pl.BlockSpec((f, tn), lambda j, i: (0, j)),
                  pl.BlockSpec((tm, tn), lambda j, i: (i, j))],
        out_specs=pl.BlockSpec((tm, tn), lambda j, i: (i, j)),
        out_shape=jax.ShapeDtypeStruct((t, n), F32),
        scratch_shapes=[pltpu.VMEM((f, tn), BF16)],
        compiler_params=_params(("arbitrary", "arbitrary")),
        name="down_proj",
    )(h, w_down, resid)


TILES = dict(
    proj_tm=1024, proj_tn=1024, proj_rb=256,
    conv_tm=1024, conv_tn=512, conv_rb=256,
    attn_tq=1024, attn_tk=256, attn_pw=256, attn_ahead=3,
    out_tm=512, out_rb=256,
    gu_tm=1024, gu_tn=512,
    down_tm=512, down_tn=512,
)


def _rope_tables(seq):
    pos = jnp.arange(seq, dtype=F32)
    inv_freq = ROPE_THETA ** (-jnp.arange(0, HEAD_DIM, 2, dtype=F32) / HEAD_DIM)
    ang = pos[:, None] * inv_freq[None, :]
    cos, sin = jnp.cos(ang), jnp.sin(ang)
    reps = LANES // (HEAD_DIM // 2)
    cos_l = jnp.tile(cos, (1, reps))
    sin_l = jnp.tile(jnp.concatenate([-sin, sin], axis=1), (1, reps // 2))
    return cos_l, sin_l


def kernel(x, attn_norm_w, w_in, q_norm_w, k_norm_w, lambda_q1, lambda_k1,
           lambda_q2, lambda_k2, subln_w, conv_w, w_out, ffn_norm_w, w_gate,
           w_up, w_down):
    b, s, d = x.shape
    t = b * s
    attn_w = N_HEADS * V_DIM
    qk_cols = N_HEADS * 2 * HEAD_DIM
    conv_width = d - attn_w
    col_v = 2 * qk_cols
    col_b = col_v + attn_w
    col_c = col_b + conv_width
    col_h = col_c + conv_width
    tl = TILES
    assert w_in.shape[0] == 1, "single-layer block"

    x2 = x.reshape(t, d)
    cos_l, sin_l = _rope_tables(s)
    scale = math.log2(math.e) / math.sqrt(HEAD_DIM)
    nw = jnp.concatenate([jnp.tile(q_norm_w[0] * scale, qk_cols // HEAD_DIM),
                          jnp.tile(k_norm_w[0], qk_cols // HEAD_DIM)]).reshape(1, -1)
    lam_vecs = jnp.concatenate([lambda_q1, lambda_k1, lambda_q2, lambda_k2], axis=0)

    qkv, xn = _qkv_proj(x2, attn_norm_w[0], w_in[0], nw, cos_l, sin_l, s, col_b,
                        tl["proj_tm"], tl["proj_tn"], tl["proj_rb"])
    conv = _conv_proj(xn, t, w_in[0], conv_w[0], col_b, col_c, col_h, conv_width, s,
                      tl["conv_tm"], tl["conv_tn"], tl["conv_rb"])

    attn = _attention(qkv.reshape(b, s, col_b), lam_vecs, subln_w[0].reshape(1, V_DIM),
                      tl["attn_tq"], tl["attn_tk"], tl["attn_pw"], tl["attn_ahead"])

    h1, fn = _out_proj(attn.reshape(t, attn_w), conv, w_out[0], x2, ffn_norm_w[0],
                       tl["out_tm"], tl["out_rb"])
    hid = _gate_up(fn, w_gate[0], w_up[0], tl["gu_tm"], tl["gu_tn"])
    out = _down(hid, w_down[0], h1, tl["down_tm"], tl["down_tn"])
    return out.reshape(b, s, d)
```

```python
import functools
import math

import jax
import jax.numpy as jnp
from jax import lax
from jax.experimental import pallas as pl
from jax.experimental.pallas import tpu as pltpu

F32 = jnp.float32
BF16 = jnp.bfloat16

CHUNK = 64
HEAD_DIM = 64
V_DIM = 2 * HEAD_DIM
N_HEADS = 8
CONV_K = 3
ROPE_THETA = 10000.0
EPS = 1e-6
SUBLN_EPS = 1e-5
LAMBDA_INIT = 0.8 - 0.6 * math.exp(-0.3 * 0)
MASKED_SCORE = -1e30

LANES = 128
SUBLANES = 8
BF16_SUBLANES = 16
VMEM_LIMIT = 60 * 1024 * 1024


def _params(semantics):
    return pltpu.CompilerParams(dimension_semantics=semantics,
                                vmem_limit_bytes=VMEM_LIMIT)


def _mm(a_bf16, w_f32):
    return jnp.dot(a_bf16, w_f32.astype(BF16), preferred_element_type=F32)


def _rms_scale(x, w, eps):
    ms = jnp.mean(x * x, axis=-1, keepdims=True)
    return (x * lax.rsqrt(ms + eps)) * w


def _row_block_pipeline(n_blocks, matmul, epilogue):
    acc = matmul(0)
    for r in range(n_blocks):
        nxt = matmul(r + 1) if r + 1 < n_blocks else None
        epilogue(r, acc)
        acc = nxt


def _qkv_kernel(x_ref, anw_ref, w_ref, nw_ref, cos_ref, sin_ref, o_ref, xn_ref, w_scr,
                *, rb, n_qk_tiles):
    j = pl.program_id(0)
    tm, tn = o_ref.shape

    @pl.when(pl.program_id(1) == 0)
    def _():
        w_scr[...] = w_ref[...].astype(BF16)

    lane = lax.broadcasted_iota(jnp.int32, (1, LANES), 1)
    lo_half = lane < HEAD_DIM
    first = (lane % HEAD_DIM) < (HEAD_DIM // 2)

    def matmul(r):
        rows = slice(r * rb, (r + 1) * rb)
        xn = _rms_scale(x_ref[rows, :], anw_ref[...], EPS).astype(BF16)
        xn_ref[rows, :] = xn
        return jnp.dot(xn, w_scr[...], preferred_element_type=F32)

    def plain(r, acc):
        o_ref[r * rb:(r + 1) * rb, :] = acc.astype(o_ref.dtype)

    def norm_rope(r, acc):
        rows = slice(r * rb, (r + 1) * rb)
        cos = cos_ref[rows, :]
        sin = sin_ref[rows, :]
        for c in range(tn // LANES):
            cols = slice(c * LANES, (c + 1) * LANES)
            y = acc[:, cols]
            sq = y * y
            s_lo = jnp.sum(jnp.where(lo_half, sq, 0.0), axis=-1, keepdims=True)
            s_hi = jnp.sum(jnp.where(lo_half, 0.0, sq), axis=-1, keepdims=True)
            ms = jnp.where(lo_half, s_lo, s_hi) * (1.0 / HEAD_DIM)
            yn = (y * lax.rsqrt(ms + EPS)) * nw_ref[:, cols]
            rot = jnp.where(first, pltpu.roll(yn, LANES - HEAD_DIM // 2, 1),
                            pltpu.roll(yn, HEAD_DIM // 2, 1))
            o_ref[rows, cols] = (yn * cos + rot * sin).astype(o_ref.dtype)

    @pl.when(j < n_qk_tiles)
    def _():
        _row_block_pipeline(tm // rb, matmul, norm_rope)

    @pl.when(j >= n_qk_tiles)
    def _():
        _row_block_pipeline(tm // rb, matmul, plain)


def _qkv_proj(x, attn_norm_w, w_in, nw, cos, sin, seq, n_out, tm, tn, rb):
    t, d = x.shape
    n_i = t // tm
    n_qk_tiles = nw.shape[1] // tn
    tiles_per_seq = seq // tm
    kern = functools.partial(_qkv_kernel, rb=rb, n_qk_tiles=n_qk_tiles)
    return pl.pallas_call(
        kern,
        grid=(n_out // tn, n_i),
        in_specs=[pl.BlockSpec((tm, d), lambda j, i: (i, 0)),
                  pl.BlockSpec((1, d), lambda j, i: (0, 0)),
                  pl.BlockSpec((d, tn), lambda j, i: (0, j)),
                  pl.BlockSpec((1, tn), lambda j, i: (0, jnp.minimum(j, n_qk_tiles - 1))),
                  pl.BlockSpec((tm, LANES), lambda j, i: (i % tiles_per_seq, 0)),
                  pl.BlockSpec((tm, LANES), lambda j, i: (i % tiles_per_seq, 0))],
        out_specs=[pl.BlockSpec((tm, tn), lambda j, i: (i, j)),
                   pl.BlockSpec((tm, d), lambda j, i: (jnp.where(j == 0, i, n_i), 0))],
        out_shape=[jax.ShapeDtypeStruct((t, n_out), BF16),
                   jax.ShapeDtypeStruct((t + tm, d), BF16)],
        scratch_shapes=[pltpu.VMEM((d, tn), BF16)],
        compiler_params=_params(("arbitrary", "arbitrary")),
        name="qkv_proj",
    )(x, attn_norm_w.reshape(1, d), w_in, nw, cos, sin)


def _conv_kernel(xn_ref, wb_ref, wc_ref, wh_ref, cw_ref, o_ref,
                 wb_scr, wc_scr, wh_scr, u_scr, carry_scr, *, tiles_per_seq, rb):
    i = pl.program_id(1)
    tm = xn_ref.shape[0]

    @pl.when(i == 0)
    def _():
        wc_scr[...] = wc_ref[...].astype(BF16)
        wh_scr[...] = wh_ref[...].astype(BF16)
        wb_scr[...] = wb_ref[...].astype(BF16)

    @pl.when(i % tiles_per_seq == 0)
    def _():
        u_scr[0:SUBLANES, :] = jnp.zeros((SUBLANES, u_scr.shape[1]), F32)

    @pl.when(i % tiles_per_seq != 0)
    def _():
        u_scr[0:SUBLANES, :] = carry_scr[...]

    def xn_rows(r):
        return xn_ref[r * rb:(r + 1) * rb, :]

    def gate_matmuls(r):
        return (jnp.dot(xn_rows(r), wc_scr[...], preferred_element_type=F32),
                jnp.dot(xn_rows(r), wh_scr[...], preferred_element_type=F32))

    def store_u(r, acc):
        u_scr[SUBLANES + r * rb:SUBLANES + (r + 1) * rb, :] = acc[0] * acc[1]

    _row_block_pipeline(tm // rb, gate_matmuls, store_u)
    carry_scr[...] = u_scr[tm:tm + SUBLANES, :]
    cw = cw_ref[...]

    def b_matmul(r):
        return jnp.dot(xn_rows(r), wb_scr[...], preferred_element_type=F32)

    def conv_out(r, gate_b):
        def shifted(back):
            start = SUBLANES + r * rb - back
            return u_scr[start:start + rb, :]
        y = cw[2:3, :] * shifted(0) + cw[1:2, :] * shifted(1) + cw[0:1, :] * shifted(2)
        o_ref[r * rb:(r + 1) * rb, :] = (gate_b * y).astype(o_ref.dtype)

    _row_block_pipeline(tm // rb, b_matmul, conv_out)


def _conv_proj(xn, t, w_in, conv_w, col_b, col_c, col_h, n, seq, tm, tn, rb):
    d = xn.shape[1]
    jb, jc, jh = col_b // tn, col_c // tn, col_h // tn
    nj = n // tn
    kern = functools.partial(_conv_kernel, tiles_per_seq=seq // tm, rb=rb)
    w_scratch = [pltpu.VMEM((d, tn), BF16)] * 3
    return pl.pallas_call(
        kern,
        grid=(nj, t // tm),
        in_specs=[pl.BlockSpec((tm, d), lambda j, i: (i, 0)),
                  pl.BlockSpec((d, tn), lambda j, i: (0, jb + j)),
                  pl.BlockSpec((d, tn), lambda j, i: (0, jc + j)),
                  pl.BlockSpec((d, tn), lambda j, i: (0, jh + j)),
                  pl.BlockSpec((CONV_K, tn), lambda j, i: (0, j))],
        out_specs=pl.BlockSpec((tm, tn), lambda j, i: (i, j)),
        out_shape=jax.ShapeDtypeStruct((t, n), BF16),
        scratch_shapes=w_scratch + [pltpu.VMEM((tm + SUBLANES, tn), F32),
                                    pltpu.VMEM((SUBLANES, tn), F32)],
        compiler_params=_params(("arbitrary", "arbitrary")),
        name="conv_proj",
    )(xn, w_in, w_in, w_in, conv_w)


def _attn_kernel(q_ref, k_ref, v_ref, lam_ref, sw_ref, o_ref,
                 vt_scr, bias_scr, qq_a, qq_b, m_a, m_b, acc_a, acc_b, *s_bufs,
                 tq, tk, pw, ahead):
    seq = k_ref.shape[1]
    n_q = seq // tq
    n_diag = tq // tk
    n_buf = len(s_bufs)
    n_panels = 2 * tq // pw
    assert ahead < n_buf
    all_visible = ("full",) * n_panels
    state = ((qq_a, m_a, acc_a), (qq_b, m_b, acc_b))

    for c in range(seq // LANES):
        blk = v_ref[0, c * LANES:(c + 1) * LANES, :].astype(F32)
        vt_scr[0:V_DIM, c * LANES:(c + 1) * LANES] = blk.T.astype(BF16)
    vt_scr[V_DIM:, :] = jnp.ones((vt_scr.shape[0] - V_DIM, seq), BF16)
    key = lax.broadcasted_iota(jnp.int32, bias_scr.shape, 0)
    qry = lax.broadcasted_iota(jnp.int32, bias_scr.shape, 1)
    bias_scr[...] = jnp.where(key // CHUNK <= qry // CHUNK, 0.0, MASKED_SCORE)

    lq1, lk1, lq2, lk2 = (lam_ref[r:r + 1, :] for r in range(4))
    lam = (jnp.exp(jnp.sum(lq1 * lk1, axis=-1, keepdims=True))
           - jnp.exp(jnp.sum(lq2 * lk2, axis=-1, keepdims=True)) + LAMBDA_INIT)

    def diag_modes(key_off):
        modes = []
        for c in range(n_panels):
            q_off = (c * pw) % tq
            if key_off + tk <= q_off:
                modes.append("full")
            elif key_off >= q_off + pw:
                modes.append("skip")
            else:
                assert key_off == q_off and tk == pw
                modes.append("tri")
        return tuple(modes)

    def prepare(qi):
        qq_scr, m_scr, acc_scr = state[qi % 2]
        dim = lax.broadcasted_iota(jnp.int32, (V_DIM, 1), 0)
        for c in range(tq // LANES):
            rows = slice(qi * tq + c * LANES, qi * tq + (c + 1) * LANES)
            qt = q_ref[0, rows, :].astype(F32).T
            cols = slice(c * LANES, (c + 1) * LANES)
            qq_scr[:, cols] = jnp.where(dim < HEAD_DIM, qt, 0.0).astype(BF16)
            cols = slice(tq + c * LANES, tq + (c + 1) * LANES)
            qq_scr[:, cols] = jnp.where(dim < HEAD_DIM, 0.0, qt).astype(BF16)
        m_scr[...] = jnp.full(m_scr.shape, -jnp.inf, F32)
        acc_scr[...] = jnp.zeros(acc_scr.shape, F32)

    stages = []
    for qi in range(n_q):
        n_full = n_diag * qi
        for t in range(n_full + n_diag):
            modes = all_visible if t < n_full else diag_modes((t - n_full) * tk)
            stages.append((qi, t, modes))

    def scores(g, c):
        qi, t, modes = stages[g]
        if modes[c] == "skip":
            return
        cols = slice(c * pw, (c + 1) * pw)
        s_bufs[g % n_buf][:, cols] = jnp.dot(
            k_ref[0, t * tk:(t + 1) * tk, :], state[qi % 2][0][:, cols],
            preferred_element_type=F32)

    def update(g, c):
        qi, t, modes = stages[g]
        if modes[c] == "skip":
            return
        _, m_scr, acc_scr = state[qi % 2]
        cols = slice(c * pw, (c + 1) * pw)
        s = s_bufs[g % n_buf][:, cols]
        if modes[c] == "tri":
            s = s + bias_scr[...]
        m_prev = m_scr[:, cols]
        m_new = jnp.maximum(m_prev, jnp.max(s, axis=0, keepdims=True))
        alpha = jnp.exp2(m_prev - m_new)
        p = jnp.exp2(s - m_new).astype(BF16)
        m_scr[:, cols] = m_new
        acc_scr[:, cols] = alpha * acc_scr[:, cols] + jnp.dot(
            vt_scr[:, t * tk:(t + 1) * tk], p, preferred_element_type=F32)

    def finish(qi):
        acc = state[qi % 2][2][...]
        o = acc[0:V_DIM, :] / acc[V_DIM:V_DIM + 1, :]
        a = (o[:, 0:tq] - lam * o[:, tq:]).T
        ms = jnp.mean(a * a, axis=-1, keepdims=True)
        o_ref[0, qi * tq:(qi + 1) * tq, :] = (
            ((a * lax.rsqrt(ms + SUBLN_EPS)) * sw_ref[...])
            * (1.0 - LAMBDA_INIT)).astype(o_ref.dtype)

    def start(g):
        qi, t, _ = stages[g]
        if t == 0:
            prepare(qi)

    for g in range(ahead):
        start(g)
        for c in range(n_panels):
            scores(g, c)
    for g, (qi, t, _) in enumerate(stages):
        if g + ahead < len(stages):
            start(g + ahead)
        for c in range(n_panels):
            update(g, c)
            if g + ahead < len(stages):
                scores(g + ahead, c)
        if t == n_diag * (qi + 1) - 1:
            finish(qi)


def _attention(qkv, lam_vecs, subln_w, tq, tk, pw, ahead):
    b, s, w3 = qkv.shape
    w = w3 // 3
    h = w // V_DIM
    kern = functools.partial(_attn_kernel, tq=tq, tk=tk, pw=pw, ahead=ahead)

    def head_block(first):
        return pl.BlockSpec((1, s, V_DIM), lambda bi, hi: (bi, 0, first + hi))

    per_q_tile = ([pltpu.VMEM((V_DIM, 2 * tq), BF16)] * 2
                  + [pltpu.VMEM((1, 2 * tq), F32)] * 2
                  + [pltpu.VMEM((V_DIM + BF16_SUBLANES, 2 * tq), F32)] * 2)
    return pl.pallas_call(
        kern,
        grid=(b, h),
        in_specs=[head_block(0), head_block(h), head_block(2 * h),
                  pl.BlockSpec((4, HEAD_DIM), lambda bi, hi: (0, 0)),
                  pl.BlockSpec((1, V_DIM), lambda bi, hi: (0, 0))],
        out_specs=head_block(0),
        out_shape=jax.ShapeDtypeStruct((b, s, w), BF16),
        scratch_shapes=[pltpu.VMEM((V_DIM + BF16_SUBLANES, s), BF16),
                        pltpu.VMEM((tk, pw), F32)]
        + per_q_tile
        + [pltpu.VMEM((tk, 2 * tq), F32)] * (ahead + 1),
        compiler_params=_params(("parallel", "parallel")),
        name="diff_attention",
    )(qkv, qkv, qkv, lam_vecs, subln_w)


def _out_kernel(a_ref, c_ref, w_ref, x_ref, fnw_ref, o_ref, fn_ref, w_scr, *, rb):
    @pl.when(pl.program_id(0) == 0)
    def _():
        w_scr[...] = w_ref[...].astype(BF16)

    tm = o_ref.shape[0]
    ka = a_ref.shape[1]

    def matmul(r):
        rows = slice(r * rb, (r + 1) * rb)
        return (x_ref[rows, :]
                + jnp.dot(a_ref[rows, :], w_scr[0:ka, :], preferred_element_type=F32)
                + jnp.dot(c_ref[rows, :], w_scr[ka:, :], preferred_element_type=F32))

    def store(r, h):
        rows = slice(r * rb, (r + 1) * rb)
        o_ref[rows, :] = h
        fn_ref[rows, :] = _rms_scale(h, fnw_ref[...], EPS).astype(fn_ref.dtype)

    _row_block_pipeline(tm // rb, matmul, store)


def _out_proj(attn, conv, w_out, x, ffn_norm_w, tm, rb):
    t, ka = attn.shape
    kc = conv.shape[1]
    k, n = w_out.shape
    assert ka + kc == k
    return pl.pallas_call(
        functools.partial(_out_kernel, rb=rb),
        grid=(t // tm,),
        in_specs=[pl.BlockSpec((tm, ka), lambda i: (i, 0)),
                  pl.BlockSpec((tm, kc), lambda i: (i, 0)),
                  pl.BlockSpec((k, n), lambda i: (0, 0), pipeline_mode=pl.Buffered(1)),
                  pl.BlockSpec((tm, n), lambda i: (i, 0)),
                  pl.BlockSpec((1, n), lambda i: (0, 0))],
        out_specs=[pl.BlockSpec((tm, n), lambda i: (i, 0)),
                   pl.BlockSpec((tm, n), lambda i: (i, 0))],
        out_shape=[jax.ShapeDtypeStruct((t, n), F32),
                   jax.ShapeDtypeStruct((t, n), BF16)],
        scratch_shapes=[pltpu.VMEM((k, n), BF16)],
        compiler_params=_params(("arbitrary",)),
        name="out_proj",
    )(attn, conv, w_out, x, ffn_norm_w.reshape(1, n))


def _gate_up_kernel(x_ref, wg_ref, wu_ref, o_ref):
    x = x_ref[...]
    g = _mm(x, wg_ref[...])
    u = _mm(x, wu_ref[...])
    o_ref[...] = ((g * jax.nn.sigmoid(g)) * u).astype(o_ref.dtype)


def _gate_up(fn, w_gate, w_up, tm, tn):
    t, d = fn.shape
    f = w_gate.shape[1]
    return pl.pallas_call(
        _gate_up_kernel,
        grid=(t // tm, f // tn),
        in_specs=[pl.BlockSpec((tm, d), lambda i, j: (i, 0)),
                  pl.BlockSpec((d, tn), lambda i, j: (0, j)),
                  pl.BlockSpec((d, tn), lambda i, j: (0, j))],
        out_specs=pl.BlockSpec((tm, tn), lambda i, j: (i, j)),
        out_shape=jax.ShapeDtypeStruct((t, f), BF16),
        compiler_params=_params(("parallel", "parallel")),
        name="gate_up",
    )(fn, w_gate, w_up)


def _down_kernel(h_ref, w_ref, r_ref, o_ref, w_scr):
    @pl.when(pl.program_id(1) == 0)
    def _():
        w_scr[...] = w_ref[...].astype(BF16)

    o_ref[...] = r_ref[...] + jnp.dot(h_ref[...], w_scr[...],
                                      preferred_element_type=F32)


def _down(h, w_down, resid, tm, tn):
    t, f = h.shape
    n = w_down.shape[1]
    return pl.pallas_call(
        _down_kernel,
        grid=(n // tn, t // tm),
        in_specs=[pl.BlockSpec((tm, f), lambda j, i: (i, 0)),
                  pl.BlockSpec((f, tn), lambda j, i: (0, j), pipeline_mode=pl.Buffered(1)),
                  pl.BlockSpec((tm, tn), lambda j, i: (i, j))],
        out_specs=pl.BlockSpec((tm, tn), lambda j, i: (i, j)),
        out_shape=jax.ShapeDtypeStruct((t, n), F32),
        scratch_shapes=[pltpu.VMEM((f, tn), BF16)],
        compiler_params=_params(("arbitrary", "arbitrary")),
        name="down_proj",
    )(h, w_down, resid)


TILES = dict(
    proj_tm=1024, proj_tn=1024, proj_rb=256,
    conv_tm=1024, conv_tn=512, conv_rb=256,
    attn_tq=1024, attn_tk=256, attn_pw=256, attn_ahead=3,
    out_tm=512, out_rb=256,
    gu_tm=1024, gu_tn=512,
    down_tm=512, down_tn=1024,
)


def _rope_tables(seq):
    pos = jnp.arange(seq, dtype=F32)
    inv_freq = ROPE_THETA ** (-jnp.arange(0, HEAD_DIM, 2, dtype=F32) / HEAD_DIM)
    ang = pos[:, None] * inv_freq[None, :]
    cos, sin = jnp.cos(ang), jnp.sin(ang)
    reps = LANES // (HEAD_DIM // 2)
    cos_l = jnp.tile(cos, (1, reps))
    sin_l = jnp.tile(jnp.concatenate([-sin, sin], axis=1), (1, reps // 2))
    return cos_l, sin_l


def kernel(x, attn_norm_w, w_in, q_norm_w, k_norm_w, lambda_q1, lambda_k1,
           lambda_q2, lambda_k2, subln_w, conv_w, w_out, ffn_norm_w, w_gate,
           w_up, w_down):
    b, s, d = x.shape
    t = b * s
    attn_w = N_HEADS * V_DIM
    qk_cols = N_HEADS * 2 * HEAD_DIM
    conv_width = d - attn_w
    col_v = 2 * qk_cols
    col_b = col_v + attn_w
    col_c = col_b + conv_width
    col_h = col_c + conv_width
    tl = TILES
    assert w_in.shape[0] == 1, "single-layer block"

    x2 = x.reshape(t, d)
    cos_l, sin_l = _rope_tables(s)
    scale = math.log2(math.e) / math.sqrt(HEAD_DIM)
    nw = jnp.concatenate([jnp.tile(q_norm_w[0] * scale, qk_cols // HEAD_DIM),
                          jnp.tile(k_norm_w[0], qk_cols // HEAD_DIM)]).reshape(1, -1)
    lam_vecs = jnp.concatenate([lambda_q1, lambda_k1, lambda_q2, lambda_k2], axis=0)

    qkv, xn = _qkv_proj(x2, attn_norm_w[0], w_in[0], nw, cos_l, sin_l, s, col_b,
                        tl["proj_tm"], tl["proj_tn"], tl["proj_rb"])
    conv = _conv_proj(xn, t, w_in[0], conv_w[0], col_b, col_c, col_h, conv_width, s,
                      tl["conv_tm"], tl["conv_tn"], tl["conv_rb"])

    attn = _attention(qkv.reshape(b, s, col_b), lam_vecs, subln_w[0].reshape(1, V_DIM),
                      tl["attn_tq"], tl["attn_tk"], tl["attn_pw"], tl["attn_ahead"])

    h1, fn = _out_proj(attn.reshape(t, attn_w), conv, w_out[0], x2, ffn_norm_w[0],
                       tl["out_tm"], tl["out_rb"])
    hid = _gate_up(fn, w_gate[0], w_up[0], tl["gu_tm"], tl["gu_tn"])
    out = _down(hid, w_down[0], h1, tl["down_tm"], tl["down_tn"])
    return out.reshape(b, s, d)
```

```python
import functools
import math

import jax
import jax.numpy as jnp
from jax import lax
from jax.experimental import pallas as pl
from jax.experimental.pallas import tpu as pltpu

F32 = jnp.float32
BF16 = jnp.bfloat16

CHUNK = 64
HEAD_DIM = 64
V_DIM = 2 * HEAD_DIM
N_HEADS = 8
CONV_K = 3
ROPE_THETA = 10000.0
EPS = 1e-6
SUBLN_EPS = 1e-5
LAMBDA_INIT = 0.8 - 0.6 * math.exp(-0.3 * 0)
MASKED_SCORE = -1e30

LANES = 128
SUBLANES = 8
BF16_SUBLANES = 16
VMEM_LIMIT = 60 * 1024 * 1024


def _params(semantics):
    return pltpu.CompilerParams(dimension_semantics=semantics,
                                vmem_limit_bytes=VMEM_LIMIT)


def _mm(a_bf16, w_f32):
    return jnp.dot(a_bf16, w_f32.astype(BF16), preferred_element_type=F32)


def _rms_scale(x, w, eps):
    ms = jnp.mean(x * x, axis=-1, keepdims=True)
    return (x * lax.rsqrt(ms + eps)) * w


def _row_block_pipeline(n_blocks, matmul, epilogue):
    acc = matmul(0)
    for r in range(n_blocks):
        nxt = matmul(r + 1) if r + 1 < n_blocks else None
        epilogue(r, acc)
        acc = nxt


def _qkv_kernel(x_ref, anw_ref, w_ref, nw_ref, cos_ref, sin_ref, o_ref, xn_ref, w_scr,
                *, rb, n_qk_tiles):
    j = pl.program_id(0)
    tm, tn = o_ref.shape

    @pl.when(pl.program_id(1) == 0)
    def _():
        w_scr[...] = w_ref[...].astype(BF16)

    lane = lax.broadcasted_iota(jnp.int32, (1, LANES), 1)
    lo_half = lane < HEAD_DIM
    first = (lane % HEAD_DIM) < (HEAD_DIM // 2)

    def matmul(r):
        rows = slice(r * rb, (r + 1) * rb)
        xn = _rms_scale(x_ref[rows, :], anw_ref[...], EPS).astype(BF16)
        xn_ref[rows, :] = xn
        return jnp.dot(xn, w_scr[...], preferred_element_type=F32)

    def plain(r, acc):
        o_ref[r * rb:(r + 1) * rb, :] = acc.astype(o_ref.dtype)

    def norm_rope(r, acc):
        rows = slice(r * rb, (r + 1) * rb)
        cos = cos_ref[rows, :]
        sin = sin_ref[rows, :]
        for c in range(tn // LANES):
            cols = slice(c * LANES, (c + 1) * LANES)
            y = acc[:, cols]
            sq = y * y
            s_lo = jnp.sum(jnp.where(lo_half, sq, 0.0), axis=-1, keepdims=True)
            s_hi = jnp.sum(jnp.where(lo_half, 0.0, sq), axis=-1, keepdims=True)
            ms = jnp.where(lo_half, s_lo, s_hi) * (1.0 / HEAD_DIM)
            yn = (y * lax.rsqrt(ms + EPS)) * nw_ref[:, cols]
            rot = jnp.where(first, pltpu.roll(yn, LANES - HEAD_DIM // 2, 1),
                            pltpu.roll(yn, HEAD_DIM // 2, 1))
            o_ref[rows, cols] = (yn * cos + rot * sin).astype(o_ref.dtype)

    @pl.when(j < n_qk_tiles)
    def _():
        _row_block_pipeline(tm // rb, matmul, norm_rope)

    @pl.when(j >= n_qk_tiles)
    def _():
        _row_block_pipeline(tm // rb, matmul, plain)


def _qkv_proj(x, attn_norm_w, w_in, nw, cos, sin, seq, n_out, tm, tn, rb):
    t, d = x.shape
    n_i = t // tm
    n_qk_tiles = nw.shape[1] // tn
    tiles_per_seq = seq // tm
    kern = functools.partial(_qkv_kernel, rb=rb, n_qk_tiles=n_qk_tiles)
    return pl.pallas_call(
        kern,
        grid=(n_out // tn, n_i),
        in_specs=[pl.BlockSpec((tm, d), lambda j, i: (i, 0)),
                  pl.BlockSpec((1, d), lambda j, i: (0, 0)),
                  pl.BlockSpec((d, tn), lambda j, i: (0, j)),
                  pl.BlockSpec((1, tn), lambda j, i: (0, jnp.minimum(j, n_qk_tiles - 1))),
                  pl.BlockSpec((tm, LANES), lambda j, i: (i % tiles_per_seq, 0)),
                  pl.BlockSpec((tm, LANES), lambda j, i: (i % tiles_per_seq, 0))],
        out_specs=[pl.BlockSpec((tm, tn), lambda j, i: (i, j)),
                   pl.BlockSpec((tm, d), lambda j, i: (jnp.where(j == 0, i, n_i), 0))],
        out_shape=[jax.ShapeDtypeStruct((t, n_out), BF16),
                   jax.ShapeDtypeStruct((t + tm, d), BF16)],
        scratch_shapes=[pltpu.VMEM((d, tn), BF16)],
        compiler_params=_params(("arbitrary", "arbitrary")),
        name="qkv_proj",
    )(x, attn_norm_w.reshape(1, d), w_in, nw, cos, sin)


def _conv_kernel(xn_ref, wb_ref, wc_ref, wh_ref, cw_ref, o_ref,
                 wb_scr, wc_scr, wh_scr, u_scr, carry_scr, *, tiles_per_seq, rb):
    i = pl.program_id(1)
    tm = xn_ref.shape[0]

    @pl.when(i == 0)
    def _():
        wc_scr[...] = wc_ref[...].astype(BF16)
        wh_scr[...] = wh_ref[...].astype(BF16)
        wb_scr[...] = wb_ref[...].astype(BF16)

    @pl.when(i % tiles_per_seq == 0)
    def _():
        u_scr[0:SUBLANES, :] = jnp.zeros((SUBLANES, u_scr.shape[1]), F32)

    @pl.when(i % tiles_per_seq != 0)
    def _():
        u_scr[0:SUBLANES, :] = carry_scr[...]

    def xn_rows(r):
        return xn_ref[r * rb:(r + 1) * rb, :]

    def gate_matmuls(r):
        return (jnp.dot(xn_rows(r), wc_scr[...], preferred_element_type=F32),
                jnp.dot(xn_rows(r), wh_scr[...], preferred_element_type=F32))

    def store_u(r, acc):
        u_scr[SUBLANES + r * rb:SUBLANES + (r + 1) * rb, :] = acc[0] * acc[1]

    _row_block_pipeline(tm // rb, gate_matmuls, store_u)
    carry_scr[...] = u_scr[tm:tm + SUBLANES, :]
    cw = cw_ref[...]

    def b_matmul(r):
        return jnp.dot(xn_rows(r), wb_scr[...], preferred_element_type=F32)

    def conv_out(r, gate_b):
        def shifted(back):
            start = SUBLANES + r * rb - back
            return u_scr[start:start + rb, :]
        y = cw[2:3, :] * shifted(0) + cw[1:2, :] * shifted(1) + cw[0:1, :] * shifted(2)
        o_ref[r * rb:(r + 1) * rb, :] = (gate_b * y).astype(o_ref.dtype)

    _row_block_pipeline(tm // rb, b_matmul, conv_out)


def _conv_proj(xn, t, w_in, conv_w, col_b, col_c, col_h, n, seq, tm, tn, rb):
    d = xn.shape[1]
    jb, jc, jh = col_b // tn, col_c // tn, col_h // tn
    nj = n // tn
    kern = functools.partial(_conv_kernel, tiles_per_seq=seq // tm, rb=rb)
    w_scratch = [pltpu.VMEM((d, tn), BF16)] * 3
    return pl.pallas_call(
        kern,
        grid=(nj, t // tm),
        in_specs=[pl.BlockSpec((tm, d), lambda j, i: (i, 0)),
                  pl.BlockSpec((d, tn), lambda j, i: (0, jb + j)),
                  pl.BlockSpec((d, tn), lambda j, i: (0, jc + j)),
                  pl.BlockSpec((d, tn), lambda j, i: (0, jh + j)),
                  pl.BlockSpec((CONV_K, tn), lambda j, i: (0, j))],
        out_specs=pl.BlockSpec((tm, tn), lambda j, i: (i, j)),
        out_shape=jax.ShapeDtypeStruct((t, n), BF16),
        scratch_shapes=w_scratch + [pltpu.VMEM((tm + SUBLANES, tn), F32),
                                    pltpu.VMEM((SUBLANES, tn), F32)],
        compiler_params=_params(("arbitrary", "arbitrary")),
        name="conv_proj",
    )(xn, w_in, w_in, w_in, conv_w)


def _attn_kernel(q_ref, k_ref, v_ref, lam_ref, sw_ref, o_ref,
                 vt_scr, bias_scr, qq_a, qq_b, m_a, m_b, acc_a, acc_b, *s_bufs,
                 tq, tk, pw, ahead):
    seq = k_ref.shape[1]
    n_q = seq // tq
    n_diag = tq // tk
    n_buf = len(s_bufs)
    n_panels = 2 * tq // pw
    assert ahead < n_buf
    all_visible = ("full",) * n_panels
    state = ((qq_a, m_a, acc_a), (qq_b, m_b, acc_b))

    for c in range(seq // LANES):
        blk = v_ref[0, c * LANES:(c + 1) * LANES, :].astype(F32)
        vt_scr[0:V_DIM, c * LANES:(c + 1) * LANES] = blk.T.astype(BF16)
    vt_scr[V_DIM:, :] = jnp.ones((vt_scr.shape[0] - V_DIM, seq), BF16)
    key = lax.broadcasted_iota(jnp.int32, bias_scr.shape, 0)
    qry = lax.broadcasted_iota(jnp.int32, bias_scr.shape, 1)
    bias_scr[...] = jnp.where(key // CHUNK <= qry // CHUNK, 0.0, MASKED_SCORE)

    lq1, lk1, lq2, lk2 = (lam_ref[r:r + 1, :] for r in range(4))
    lam = (jnp.exp(jnp.sum(lq1 * lk1, axis=-1, keepdims=True))
           - jnp.exp(jnp.sum(lq2 * lk2, axis=-1, keepdims=True)) + LAMBDA_INIT)

    def diag_modes(key_off):
        modes = []
        for c in range(n_panels):
            q_off = (c * pw) % tq
            if key_off + tk <= q_off:
                modes.append("full")
            elif key_off >= q_off + pw:
                modes.append("skip")
            else:
                assert key_off == q_off and tk == pw
                modes.append("tri")
        return tuple(modes)

    def prepare(qi):
        qq_scr, m_scr, acc_scr = state[qi % 2]
        dim = lax.broadcasted_iota(jnp.int32, (V_DIM, 1), 0)
        for c in range(tq // LANES):
            rows = slice(qi * tq + c * LANES, qi * tq + (c + 1) * LANES)
            qt = q_ref[0, rows, :].astype(F32).T
            cols = slice(c * LANES, (c + 1) * LANES)
            qq_scr[:, cols] = jnp.where(dim < HEAD_DIM, qt, 0.0).astype(BF16)
            cols = slice(tq + c * LANES, tq + (c + 1) * LANES)
            qq_scr[:, cols] = jnp.where(dim < HEAD_DIM, 0.0, qt).astype(BF16)
        m_scr[...] = jnp.full(m_scr.shape, -jnp.inf, F32)
        acc_scr[...] = jnp.zeros(acc_scr.shape, F32)

    stages = []
    for qi in range(n_q):
        n_full = n_diag * qi
        for t in range(n_full + n_diag):
            modes = all_visible if t < n_full else diag_modes((t - n_full) * tk)
            stages.append((qi, t, modes))

    def scores(g, c):
        qi, t, modes = stages[g]
        if modes[c] == "skip":
            return
        cols = slice(c * pw, (c + 1) * pw)
        s_bufs[g % n_buf][:, cols] = jnp.dot(
            k_ref[0, t * tk:(t + 1) * tk, :], state[qi % 2][0][:, cols],
            preferred_element_type=F32)

    def update(g, c):
        qi, t, modes = stages[g]
        if modes[c] == "skip":
            return
        _, m_scr, acc_scr = state[qi % 2]
        cols = slice(c * pw, (c + 1) * pw)
        s = s_bufs[g % n_buf][:, cols]
        if modes[c] == "tri":
            s = s + bias_scr[...]
        m_prev = m_scr[:, cols]
        m_new = jnp.maximum(m_prev, jnp.max(s, axis=0, keepdims=True))
        alpha = jnp.exp2(m_prev - m_new)
        p = jnp.exp2(s - m_new).astype(BF16)
        m_scr[:, cols] = m_new
        acc_scr[:, cols] = alpha * acc_scr[:, cols] + jnp.dot(
            vt_scr[:, t * tk:(t + 1) * tk], p, preferred_element_type=F32)

    def finish(qi):
        acc = state[qi % 2][2][...]
        o = acc[0:V_DIM, :] / acc[V_DIM:V_DIM + 1, :]
        a = (o[:, 0:tq] - lam * o[:, tq:]).T
        ms = jnp.mean(a * a, axis=-1, keepdims=True)
        o_ref[0, qi * tq:(qi + 1) * tq, :] = (
            ((a * lax.rsqrt(ms + SUBLN_EPS)) * sw_ref[...])
            * (1.0 - LAMBDA_INIT)).astype(o_ref.dtype)

    def start(g):
        qi, t, _ = stages[g]
        if t == 0:
            prepare(qi)

    for g in range(ahead):
        start(g)
        for c in range(n_panels):
            scores(g, c)
    for g, (qi, t, _) in enumerate(stages):
        if g + ahead < len(stages):
            start(g + ahead)
        for c in range(n_panels):
            update(g, c)
            if g + ahead < len(stages):
                scores(g + ahead, c)
        if t == n_diag * (qi + 1) - 1:
            finish(qi)


def _attention(qkv, lam_vecs, subln_w, tq, tk, pw, ahead):
    b, s, w3 = qkv.shape
    w = w3 // 3
    h = w // V_DIM
    kern = functools.partial(_attn_kernel, tq=tq, tk=tk, pw=pw, ahead=ahead)

    def head_block(first):
        return pl.BlockSpec((1, s, V_DIM), lambda bi, hi: (bi, 0, first + hi))

    per_q_tile = ([pltpu.VMEM((V_DIM, 2 * tq), BF16)] * 2
                  + [pltpu.VMEM((1, 2 * tq), F32)] * 2
                  + [pltpu.VMEM((V_DIM + BF16_SUBLANES, 2 * tq), F32)] * 2)
    return pl.pallas_call(
        kern,
        grid=(b, h),
        in_specs=[head_block(0), head_block(h), head_block(2 * h),
                  pl.BlockSpec((4, HEAD_DIM), lambda bi, hi: (0, 0)),
                  pl.BlockSpec((1, V_DIM), lambda bi, hi: (0, 0))],
        out_specs=head_block(0),
        out_shape=jax.ShapeDtypeStruct((b, s, w), BF16),
        scratch_shapes=[pltpu.VMEM((V_DIM + BF16_SUBLANES, s), BF16),
                        pltpu.VMEM((tk, pw), F32)]
        + per_q_tile
        + [pltpu.VMEM((tk, 2 * tq), F32)] * (ahead + 1),
        compiler_params=_params(("parallel", "parallel")),
        name="diff_attention",
    )(qkv, qkv, qkv, lam_vecs, subln_w)


def _out_kernel(a_ref, c_ref, w_ref, x_ref, fnw_ref, o_ref, fn_ref, w_scr, *, rb):
    @pl.when(pl.program_id(0) == 0)
    def _():
        w_scr[...] = w_ref[...].astype(BF16)

    tm = o_ref.shape[0]
    ka = a_ref.shape[1]

    def matmul(r):
        rows = slice(r * rb, (r + 1) * rb)
        return (x_ref[rows, :]
                + jnp.dot(a_ref[rows, :], w_scr[0:ka, :], preferred_element_type=F32)
                + jnp.dot(c_ref[rows, :], w_scr[ka:, :], preferred_element_type=F32))

    def store(r, h):
        rows = slice(r * rb, (r + 1) * rb)
        o_ref[rows, :] = h
        fn_ref[rows, :] = _rms_scale(h, fnw_ref[...], EPS).astype(fn_ref.dtype)

    _row_block_pipeline(tm // rb, matmul, store)


def _out_proj(attn, conv, w_out, x, ffn_norm_w, tm, rb):
    t, ka = attn.shape
    kc = conv.shape[1]
    k, n = w_out.shape
    assert ka + kc == k
    return pl.pallas_call(
        functools.partial(_out_kernel, rb=rb),
        grid=(t // tm,),
        in_specs=[pl.BlockSpec((tm, ka), lambda i: (i, 0)),
                  pl.BlockSpec((tm, kc), lambda i: (i, 0)),
                  pl.BlockSpec((k, n), lambda i: (0, 0), pipeline_mode=pl.Buffered(1)),
                  pl.BlockSpec((tm, n), lambda i: (i, 0)),
                  pl.BlockSpec((1, n), lambda i: (0, 0))],
        out_specs=[pl.BlockSpec((tm, n), lambda i: (i, 0)),
                   pl.BlockSpec((tm, n), lambda i: (i, 0))],
        out_shape=[jax.ShapeDtypeStruct((t, n), F32),
                   jax.ShapeDtypeStruct((t, n), BF16)],
        scratch_shapes=[pltpu.VMEM((k, n), BF16)],
        compiler_params=_params(("arbitrary",)),
        name="out_proj",
    )(attn, conv, w_out, x, ffn_norm_w.reshape(1, n))


def _gate_up_kernel(x_ref, wg_ref, wu_ref, o_ref):
    x = x_ref[...]
    g = _mm(x, wg_ref[...])
    u = _mm(x, wu_ref[...])
    o_ref[...] = ((g * jax.nn.sigmoid(g)) * u).astype(o_ref.dtype)


def _gate_up(fn, w_gate, w_up, tm, tn):
    t, d = fn.shape
    f = w_gate.shape[1]
    return pl.pallas_call(
        _gate_up_kernel,
        grid=(t // tm, f // tn),
        in_specs=[pl.BlockSpec((tm, d), lambda i, j: (i, 0)),
                  pl.BlockSpec((d, tn), lambda i, j: (0, j)),
                  pl.BlockSpec((d, tn), lambda i, j: (0, j))],
        out_specs=pl.BlockSpec((tm, tn), lambda i, j: (i, j)),
        out_shape=jax.ShapeDtypeStruct((t, f), BF16),
        compiler_params=_params(("parallel", "parallel")),
        name="gate_up",
    )(fn, w_gate, w_up)


def _down_kernel(h_ref, w_ref, r_ref, o_ref, w_scr):
    @pl.when(pl.program_id(1) == 0)
    def _():
        w_scr[...] = w_ref[...].astype(BF16)

    o_ref[...] = r_ref[...] + jnp.dot(h_ref[...], w_scr[...],
                                      preferred_element_type=F32)


def _down(h, w_down, resid, tm, tn):
    t, f = h.shape
    n = w_down.shape[1]
    return pl.pallas_call(
        _down_kernel,
        grid=(n // tn, t // tm),
        in_specs=[pl.BlockSpec((tm, f), lambda j, i: (i, 0)),
                  pl.BlockSpec((f, tn), lambda j, i: (0, j), pipeline_mode=pl.Buffered(1)),
                  pl.BlockSpec((tm, tn), lambda j, i: (i, j))],
        out_specs=pl.BlockSpec((tm, tn), lambda j, i: (i, j)),
        out_shape=jax.ShapeDtypeStruct((t, n), F32),
        scratch_shapes=[pltpu.VMEM((f, tn), BF16)],
        compiler_params=_params(("arbitrary", "arbitrary")),
        name="down_proj",
    )(h, w_down, resid)


TILES = dict(
    proj_tm=1024, proj_tn=1024, proj_rb=256,
    conv_tm=1024, conv_tn=512, conv_rb=256,
    attn_tq=2048, attn_tk=256, attn_pw=256, attn_ahead=3,
    out_tm=512, out_rb=256,
    gu_tm=1024, gu_tn=512,
    down_tm=512, down_tn=1024,
)


def _rope_tables(seq):
    pos = jnp.arange(seq, dtype=F32)
    inv_freq = ROPE_THETA ** (-jnp.arange(0, HEAD_DIM, 2, dtype=F32) / HEAD_DIM)
    ang = pos[:, None] * inv_freq[None, :]
    cos, sin = jnp.cos(ang), jnp.sin(ang)
    reps = LANES // (HEAD_DIM // 2)
    cos_l = jnp.tile(cos, (1, reps))
    sin_l = jnp.tile(jnp.concatenate([-sin, sin], axis=1), (1, reps // 2))
    return cos_l, sin_l


def kernel(x, attn_norm_w, w_in, q_norm_w, k_norm_w, lambda_q1, lambda_k1,
           lambda_q2, lambda_k2, subln_w, conv_w, w_out, ffn_norm_w, w_gate,
           w_up, w_down):
    b, s, d = x.shape
    t = b * s
    attn_w = N_HEADS * V_DIM
    qk_cols = N_HEADS * 2 * HEAD_DIM
    conv_width = d - attn_w
    col_v = 2 * qk_cols
    col_b = col_v + attn_w
    col_c = col_b + conv_width
    col_h = col_c + conv_width
    tl = TILES
    assert w_in.shape[0] == 1, "single-layer block"

    x2 = x.reshape(t, d)
    cos_l, sin_l = _rope_tables(s)
    scale = math.log2(math.e) / math.sqrt(HEAD_DIM)
    nw = jnp.concatenate([jnp.tile(q_norm_w[0] * scale, qk_cols // HEAD_DIM),
                          jnp.tile(k_norm_w[0], qk_cols // HEAD_DIM)]).reshape(1, -1)
    lam_vecs = jnp.concatenate([lambda_q1, lambda_k1, lambda_q2, lambda_k2], axis=0)

    qkv, xn = _qkv_proj(x2, attn_norm_w[0], w_in[0], nw, cos_l, sin_l, s, col_b,
                        tl["proj_tm"], tl["proj_tn"], tl["proj_rb"])
    conv = _conv_proj(xn, t, w_in[0], conv_w[0], col_b, col_c, col_h, conv_width, s,
                      tl["conv_tm"], tl["conv_tn"], tl["conv_rb"])

    attn = _attention(qkv.reshape(b, s, col_b), lam_vecs, subln_w[0].reshape(1, V_DIM),
                      tl["attn_tq"], tl["attn_tk"], tl["attn_pw"], tl["attn_ahead"])

    h1, fn = _out_proj(attn.reshape(t, attn_w), conv, w_out[0], x2, ffn_norm_w[0],
                       tl["out_tm"], tl["out_rb"])
    hid = _gate_up(fn, w_gate[0], w_up[0], tl["gu_tm"], tl["gu_tn"])
    out = _down(hid, w_down[0], h1, tl["down_tm"], tl["down_tn"])
    return out.reshape(b, s, d)
```

```python
import functools
import math

import jax
import jax.numpy as jnp
from jax import lax
from jax.experimental import pallas as pl
from jax.experimental.pallas import tpu as pltpu

F32 = jnp.float32
BF16 = jnp.bfloat16

CHUNK = 64
HEAD_DIM = 64
V_DIM = 2 * HEAD_DIM
N_HEADS = 8
CONV_K = 3
ROPE_THETA = 10000.0
EPS = 1e-6
SUBLN_EPS = 1e-5
LAMBDA_INIT = 0.8 - 0.6 * math.exp(-0.3 * 0)
MASKED_SCORE = -1e30

LANES = 128
SUBLANES = 8
BF16_SUBLANES = 16
VMEM_LIMIT = 60 * 1024 * 1024


def _params(semantics):
    return pltpu.CompilerParams(dimension_semantics=semantics,
                                vmem_limit_bytes=VMEM_LIMIT)


def _mm(a_bf16, w_f32):
    return jnp.dot(a_bf16, w_f32.astype(BF16), preferred_element_type=F32)


def _rms_scale(x, w, eps):
    ms = jnp.mean(x * x, axis=-1, keepdims=True)
    return (x * lax.rsqrt(ms + eps)) * w


def _row_block_pipeline(n_blocks, matmul, epilogue):
    acc = matmul(0)
    for r in range(n_blocks):
        nxt = matmul(r + 1) if r + 1 < n_blocks else None
        epilogue(r, acc)
        acc = nxt


def _qkv_kernel(x_ref, anw_ref, w_ref, nw_ref, cos_ref, sin_ref, o_ref, xn_ref, w_scr,
                *, rb, n_qk_tiles):
    j = pl.program_id(0)
    tm, tn = o_ref.shape

    @pl.when(pl.program_id(1) == 0)
    def _():
        w_scr[...] = w_ref[...].astype(BF16)

    lane = lax.broadcasted_iota(jnp.int32, (1, LANES), 1)
    lo_half = lane < HEAD_DIM
    first = (lane % HEAD_DIM) < (HEAD_DIM // 2)

    def matmul(r):
        rows = slice(r * rb, (r + 1) * rb)
        xn = _rms_scale(x_ref[rows, :], anw_ref[...], EPS).astype(BF16)
        xn_ref[rows, :] = xn
        return jnp.dot(xn, w_scr[...], preferred_element_type=F32)

    def plain(r, acc):
        o_ref[r * rb:(r + 1) * rb, :] = acc.astype(o_ref.dtype)

    def norm_rope(r, acc):
        rows = slice(r * rb, (r + 1) * rb)
        cos = cos_ref[rows, :]
        sin = sin_ref[rows, :]
        for c in range(tn // LANES):
            cols = slice(c * LANES, (c + 1) * LANES)
            y = acc[:, cols]
            sq = y * y
            s_lo = jnp.sum(jnp.where(lo_half, sq, 0.0), axis=-1, keepdims=True)
            s_hi = jnp.sum(jnp.where(lo_half, 0.0, sq), axis=-1, keepdims=True)
            ms = jnp.where(lo_half, s_lo, s_hi) * (1.0 / HEAD_DIM)
            yn = (y * lax.rsqrt(ms + EPS)) * nw_ref[:, cols]
            rot = jnp.where(first, pltpu.roll(yn, LANES - HEAD_DIM // 2, 1),
                            pltpu.roll(yn, HEAD_DIM // 2, 1))
            o_ref[rows, cols] = (yn * cos + rot * sin).astype(o_ref.dtype)

    @pl.when(j < n_qk_tiles)
    def _():
        _row_block_pipeline(tm // rb, matmul, norm_rope)

    @pl.when(j >= n_qk_tiles)
    def _():
        _row_block_pipeline(tm // rb, matmul, plain)


def _qkv_proj(x, attn_norm_w, w_in, nw, cos, sin, seq, n_out, tm, tn, rb):
    t, d = x.shape
    n_i = t // tm
    n_qk_tiles = nw.shape[1] // tn
    tiles_per_seq = seq // tm
    kern = functools.partial(_qkv_kernel, rb=rb, n_qk_tiles=n_qk_tiles)
    return pl.pallas_call(
        kern,
        grid=(n_out // tn, n_i),
        in_specs=[pl.BlockSpec((tm, d), lambda j, i: (i, 0)),
                  pl.BlockSpec((1, d), lambda j, i: (0, 0)),
                  pl.BlockSpec((d, tn), lambda j, i: (0, j)),
                  pl.BlockSpec((1, tn), lambda j, i: (0, jnp.minimum(j, n_qk_tiles - 1))),
                  pl.BlockSpec((tm, LANES), lambda j, i: (i % tiles_per_seq, 0)),
                  pl.BlockSpec((tm, LANES), lambda j, i: (i % tiles_per_seq, 0))],
        out_specs=[pl.BlockSpec((tm, tn), lambda j, i: (i, j)),
                   pl.BlockSpec((tm, d), lambda j, i: (jnp.where(j == 0, i, n_i), 0))],
        out_shape=[jax.ShapeDtypeStruct((t, n_out), BF16),
                   jax.ShapeDtypeStruct((t + tm, d), BF16)],
        scratch_shapes=[pltpu.VMEM((d, tn), BF16)],
        compiler_params=_params(("arbitrary", "arbitrary")),
        name="qkv_proj",
    )(x, attn_norm_w.reshape(1, d), w_in, nw, cos, sin)


def _conv_kernel(xn_ref, wb_ref, wc_ref, wh_ref, cw_ref, o_ref,
                 wb_scr, wc_scr, wh_scr, u_scr, carry_scr, *, tiles_per_seq, rb):
    i = pl.program_id(1)
    tm = xn_ref.shape[0]

    @pl.when(i == 0)
    def _():
        wc_scr[...] = wc_ref[...].astype(BF16)
        wh_scr[...] = wh_ref[...].astype(BF16)
        wb_scr[...] = wb_ref[...].astype(BF16)

    @pl.when(i % tiles_per_seq == 0)
    def _():
        u_scr[0:SUBLANES, :] = jnp.zeros((SUBLANES, u_scr.shape[1]), F32)

    @pl.when(i % tiles_per_seq != 0)
    def _():
        u_scr[0:SUBLANES, :] = carry_scr[...]

    def xn_rows(r):
        return xn_ref[r * rb:(r + 1) * rb, :]

    def gate_matmuls(r):
        return (jnp.dot(xn_rows(r), wc_scr[...], preferred_element_type=F32),
                jnp.dot(xn_rows(r), wh_scr[...], preferred_element_type=F32))

    def store_u(r, acc):
        u_scr[SUBLANES + r * rb:SUBLANES + (r + 1) * rb, :] = acc[0] * acc[1]

    _row_block_pipeline(tm // rb, gate_matmuls, store_u)
    carry_scr[...] = u_scr[tm:tm + SUBLANES, :]
    cw = cw_ref[...]

    def b_matmul(r):
        return jnp.dot(xn_rows(r), wb_scr[...], preferred_element_type=F32)

    def conv_out(r, gate_b):
        def shifted(back):
            start = SUBLANES + r * rb - back
            return u_scr[start:start + rb, :]
        y = cw[2:3, :] * shifted(0) + cw[1:2, :] * shifted(1) + cw[0:1, :] * shifted(2)
        o_ref[r * rb:(r + 1) * rb, :] = (gate_b * y).astype(o_ref.dtype)

    _row_block_pipeline(tm // rb, b_matmul, conv_out)


def _conv_proj(xn, t, w_in, conv_w, col_b, col_c, col_h, n, seq, tm, tn, rb):
    d = xn.shape[1]
    jb, jc, jh = col_b // tn, col_c // tn, col_h // tn
    nj = n // tn
    kern = functools.partial(_conv_kernel, tiles_per_seq=seq // tm, rb=rb)
    w_scratch = [pltpu.VMEM((d, tn), BF16)] * 3
    return pl.pallas_call(
        kern,
        grid=(nj, t // tm),
        in_specs=[pl.BlockSpec((tm, d), lambda j, i: (i, 0)),
                  pl.BlockSpec((d, tn), lambda j, i: (0, jb + j)),
                  pl.BlockSpec((d, tn), lambda j, i: (0, jc + j)),
                  pl.BlockSpec((d, tn), lambda j, i: (0, jh + j)),
                  pl.BlockSpec((CONV_K, tn), lambda j, i: (0, j))],
        out_specs=pl.BlockSpec((tm, tn), lambda j, i: (i, j)),
        out_shape=jax.ShapeDtypeStruct((t, n), BF16),
        scratch_shapes=w_scratch + [pltpu.VMEM((tm + SUBLANES, tn), F32),
                                    pltpu.VMEM((SUBLANES, tn), F32)],
        compiler_params=_params(("arbitrary", "arbitrary")),
        name="conv_proj",
    )(xn, w_in, w_in, w_in, conv_w)


def _attn_kernel(q_ref, k_ref, v_ref, lam_ref, sw_ref, o_ref,
                 vt_scr, bias_scr, qq_a, qq_b, m_a, m_b, acc_a, acc_b, *s_bufs,
                 tq, tk, pw, ahead):
    seq = k_ref.shape[1]
    n_q = seq // tq
    n_diag = tq // tk
    n_buf = len(s_bufs)
    n_panels = 2 * tq // pw
    assert ahead < n_buf
    assert 2 * n_diag >= ahead
    assert n_diag >= 2, "query tiles of a single key tile fail on the chip; not supported"
    all_visible = ("full",) * n_panels
    state = ((qq_a, m_a, acc_a), (qq_b, m_b, acc_b))

    for c in range(seq // LANES):
        blk = v_ref[0, c * LANES:(c + 1) * LANES, :].astype(F32)
        vt_scr[0:V_DIM, c * LANES:(c + 1) * LANES] = blk.T.astype(BF16)
    vt_scr[V_DIM:, :] = jnp.ones((vt_scr.shape[0] - V_DIM, seq), BF16)
    key = lax.broadcasted_iota(jnp.int32, bias_scr.shape, 0)
    qry = lax.broadcasted_iota(jnp.int32, bias_scr.shape, 1)
    bias_scr[...] = jnp.where(key // CHUNK <= qry // CHUNK, 0.0, MASKED_SCORE)

    lq1, lk1, lq2, lk2 = (lam_ref[r:r + 1, :] for r in range(4))
    lam = (jnp.exp(jnp.sum(lq1 * lk1, axis=-1, keepdims=True))
           - jnp.exp(jnp.sum(lq2 * lk2, axis=-1, keepdims=True)) + LAMBDA_INIT)

    def diag_modes(key_off):
        modes = []
        for c in range(n_panels):
            q_off = (c * pw) % tq
            if key_off + tk <= q_off:
                modes.append("full")
            elif key_off >= q_off + pw:
                modes.append("skip")
            else:
                assert key_off == q_off and tk == pw
                modes.append("tri")
        return tuple(modes)

    def prepare(qi):
        qq_scr, m_scr, acc_scr = state[qi % 2]
        dim = lax.broadcasted_iota(jnp.int32, (V_DIM, 1), 0)
        for c in range(tq // LANES):
            rows = slice(qi * tq + c * LANES, qi * tq + (c + 1) * LANES)
            qt = q_ref[0, rows, :].astype(F32).T
            cols = slice(c * LANES, (c + 1) * LANES)
            qq_scr[:, cols] = jnp.where(dim < HEAD_DIM, qt, 0.0).astype(BF16)
            cols = slice(tq + c * LANES, tq + (c + 1) * LANES)
            qq_scr[:, cols] = jnp.where(dim < HEAD_DIM, 0.0, qt).astype(BF16)
        m_scr[...] = jnp.full(m_scr.shape, -jnp.inf, F32)
        acc_scr[...] = jnp.zeros(acc_scr.shape, F32)

    stages = []
    for qi in range(n_q):
        n_full = n_diag * qi
        for t in range(n_full + n_diag):
            modes = all_visible if t < n_full else diag_modes((t - n_full) * tk)
            stages.append((qi, t, modes))

    def scores(g, c):
        qi, t, modes = stages[g]
        if modes[c] == "skip":
            return
        cols = slice(c * pw, (c + 1) * pw)
        s_bufs[g % n_buf][:, cols] = jnp.dot(
            k_ref[0, t * tk:(t + 1) * tk, :], state[qi % 2][0][:, cols],
            preferred_element_type=F32)

    def update(g, c):
        qi, t, modes = stages[g]
        if modes[c] == "skip":
            return
        _, m_scr, acc_scr = state[qi % 2]
        cols = slice(c * pw, (c + 1) * pw)
        s = s_bufs[g % n_buf][:, cols]
        if modes[c] == "tri":
            s = s + bias_scr[...]
        m_prev = m_scr[:, cols]
        m_new = jnp.maximum(m_prev, jnp.max(s, axis=0, keepdims=True))
        alpha = jnp.exp2(m_prev - m_new)
        p = jnp.exp2(s - m_new).astype(BF16)
        m_scr[:, cols] = m_new
        acc_scr[:, cols] = alpha * acc_scr[:, cols] + jnp.dot(
            vt_scr[:, t * tk:(t + 1) * tk], p, preferred_element_type=F32)

    def finish(qi):
        acc = state[qi % 2][2][...]
        o = acc[0:V_DIM, :] / acc[V_DIM:V_DIM + 1, :]
        a = (o[:, 0:tq] - lam * o[:, tq:]).T
        ms = jnp.mean(a * a, axis=-1, keepdims=True)
        o_ref[0, qi * tq:(qi + 1) * tq, :] = (
            ((a * lax.rsqrt(ms + SUBLN_EPS)) * sw_ref[...])
            * (1.0 - LAMBDA_INIT)).astype(o_ref.dtype)

    def start(g):
        qi, t, _ = stages[g]
        if t == 0:
            prepare(qi)

    for g in range(ahead):
        start(g)
        for c in range(n_panels):
            scores(g, c)
    for g, (qi, t, _) in enumerate(stages):
        if g + ahead < len(stages):
            start(g + ahead)
        for c in range(n_panels):
            update(g, c)
            if g + ahead < len(stages):
                scores(g + ahead, c)
        if t == n_diag * (qi + 1) - 1:
            finish(qi)


def _attention(qkv, lam_vecs, subln_w, tq, tk, pw, ahead):
    b, s, w3 = qkv.shape
    w = w3 // 3
    h = w // V_DIM
    kern = functools.partial(_attn_kernel, tq=tq, tk=tk, pw=pw, ahead=ahead)

    def head_block(first):
        return pl.BlockSpec((1, s, V_DIM), lambda bi, hi: (bi, 0, first + hi))

    per_q_tile = ([pltpu.VMEM((V_DIM, 2 * tq), BF16)] * 2
                  + [pltpu.VMEM((1, 2 * tq), F32)] * 2
                  + [pltpu.VMEM((V_DIM + BF16_SUBLANES, 2 * tq), F32)] * 2)
    return pl.pallas_call(
        kern,
        grid=(b, h),
        in_specs=[head_block(0), head_block(h), head_block(2 * h),
                  pl.BlockSpec((4, HEAD_DIM), lambda bi, hi: (0, 0)),
                  pl.BlockSpec((1, V_DIM), lambda bi, hi: (0, 0))],
        out_specs=head_block(0),
        out_shape=jax.ShapeDtypeStruct((b, s, w), BF16),
        scratch_shapes=[pltpu.VMEM((V_DIM + BF16_SUBLANES, s), BF16),
                        pltpu.VMEM((tk, pw), F32)]
        + per_q_tile
        + [pltpu.VMEM((tk, 2 * tq), F32)] * (ahead + 1),
        compiler_params=_params(("parallel", "parallel")),
        name="diff_attention",
    )(qkv, qkv, qkv, lam_vecs, subln_w)


def _out_kernel(a_ref, c_ref, w_ref, x_ref, fnw_ref, o_ref, fn_ref, w_scr, *, rb):
    @pl.when(pl.program_id(0) == 0)
    def _():
        w_scr[...] = w_ref[...].astype(BF16)

    tm = o_ref.shape[0]
    ka = a_ref.shape[1]

    def matmul(r):
        rows = slice(r * rb, (r + 1) * rb)
        return (x_ref[rows, :]
                + jnp.dot(a_ref[rows, :], w_scr[0:ka, :], preferred_element_type=F32)
                + jnp.dot(c_ref[rows, :], w_scr[ka:, :], preferred_element_type=F32))

    def store(r, h):
        rows = slice(r * rb, (r + 1) * rb)
        o_ref[rows, :] = h
        fn_ref[rows, :] = _rms_scale(h, fnw_ref[...], EPS).astype(fn_ref.dtype)

    _row_block_pipeline(tm // rb, matmul, store)


def _out_proj(attn, conv, w_out, x, ffn_norm_w, tm, rb):
    t, ka = attn.shape
    kc = conv.shape[1]
    k, n = w_out.shape
    assert ka + kc == k
    return pl.pallas_call(
        functools.partial(_out_kernel, rb=rb),
        grid=(t // tm,),
        in_specs=[pl.BlockSpec((tm, ka), lambda i: (i, 0)),
                  pl.BlockSpec((tm, kc), lambda i: (i, 0)),
                  pl.BlockSpec((k, n), lambda i: (0, 0), pipeline_mode=pl.Buffered(1)),
                  pl.BlockSpec((tm, n), lambda i: (i, 0)),
                  pl.BlockSpec((1, n), lambda i: (0, 0))],
        out_specs=[pl.BlockSpec((tm, n), lambda i: (i, 0)),
                   pl.BlockSpec((tm, n), lambda i: (i, 0))],
        out_shape=[jax.ShapeDtypeStruct((t, n), F32),
                   jax.ShapeDtypeStruct((t, n), BF16)],
        scratch_shapes=[pltpu.VMEM((k, n), BF16)],
        compiler_params=_params(("arbitrary",)),
        name="out_proj",
    )(attn, conv, w_out, x, ffn_norm_w.reshape(1, n))


def _gate_up_kernel(x_ref, wg_ref, wu_ref, o_ref):
    x = x_ref[...]
    g = _mm(x, wg_ref[...])
    u = _mm(x, wu_ref[...])
    o_ref[...] = ((g * jax.nn.sigmoid(g)) * u).astype(o_ref.dtype)


def _gate_up(fn, w_gate, w_up, tm, tn):
    t, d = fn.shape
    f = w_gate.shape[1]
    return pl.pallas_call(
        _gate_up_kernel,
        grid=(t // tm, f // tn),
        in_specs=[pl.BlockSpec((tm, d), lambda i, j: (i, 0)),
                  pl.BlockSpec((d, tn), lambda i, j: (0, j)),
                  pl.BlockSpec((d, tn), lambda i, j: (0, j))],
        out_specs=pl.BlockSpec((tm, tn), lambda i, j: (i, j)),
        out_shape=jax.ShapeDtypeStruct((t, f), BF16),
        compiler_params=_params(("parallel", "parallel")),
        name="gate_up",
    )(fn, w_gate, w_up)


def _down_kernel(h_ref, w_ref, r_ref, o_ref, w_scr):
    @pl.when(pl.program_id(1) == 0)
    def _():
        w_scr[...] = w_ref[...].astype(BF16)

    o_ref[...] = r_ref[...] + jnp.dot(h_ref[...], w_scr[...],
                                      preferred_element_type=F32)


def _down(h, w_down, resid, tm, tn):
    t, f = h.shape
    n = w_down.shape[1]
    return pl.pallas_call(
        _down_kernel,
        grid=(n // tn, t // tm),
        in_specs=[pl.BlockSpec((tm, f), lambda j, i: (i, 0)),
                  pl.BlockSpec((f, tn), lambda j, i: (0, j), pipeline_mode=pl.Buffered(1)),
                  pl.BlockSpec((tm, tn), lambda j, i: (i, j))],
        out_specs=pl.BlockSpec((tm, tn), lambda j, i: (i, j)),
        out_shape=jax.ShapeDtypeStruct((t, n), F32),
        scratch_shapes=[pltpu.VMEM((f, tn), BF16)],
        compiler_params=_params(("arbitrary", "arbitrary")),
        name="down_proj",
    )(h, w_down, resid)


TILES = dict(
    proj_tm=1024, proj_tn=1024, proj_rb=256,
    conv_tm=1024, conv_tn=512, conv_rb=256,
    attn_tq=512, attn_tk=256, attn_pw=256, attn_ahead=3,
    out_tm=512, out_rb=256,
    gu_tm=1024, gu_tn=512,
    down_tm=512, down_tn=1024,
)


def _rope_tables(seq):
    pos = jnp.arange(seq, dtype=F32)
    inv_freq = ROPE_THETA ** (-jnp.arange(0, HEAD_DIM, 2, dtype=F32) / HEAD_DIM)
    ang = pos[:, None] * inv_freq[None, :]
    cos, sin = jnp.cos(ang), jnp.sin(ang)
    reps = LANES // (HEAD_DIM // 2)
    cos_l = jnp.tile(cos, (1, reps))
    sin_l = jnp.tile(jnp.concatenate([-sin, sin], axis=1), (1, reps // 2))
    return cos_l, sin_l


def kernel(x, attn_norm_w, w_in, q_norm_w, k_norm_w, lambda_q1, lambda_k1,
           lambda_q2, lambda_k2, subln_w, conv_w, w_out, ffn_norm_w, w_gate,
           w_up, w_down):
    b, s, d = x.shape
    t = b * s
    attn_w = N_HEADS * V_DIM
    qk_cols = N_HEADS * 2 * HEAD_DIM
    conv_width = d - attn_w
    col_v = 2 * qk_cols
    col_b = col_v + attn_w
    col_c = col_b + conv_width
    col_h = col_c + conv_width
    tl = TILES
    assert w_in.shape[0] == 1, "single-layer block"

    x2 = x.reshape(t, d)
    cos_l, sin_l = _rope_tables(s)
    scale = math.log2(math.e) / math.sqrt(HEAD_DIM)
    nw = jnp.concatenate([jnp.tile(q_norm_w[0] * scale, qk_cols // HEAD_DIM),
                          jnp.tile(k_norm_w[0], qk_cols // HEAD_DIM)]).reshape(1, -1)
    lam_vecs = jnp.concatenate([lambda_q1, lambda_k1, lambda_q2, lambda_k2], axis=0)

    qkv, xn = _qkv_proj(x2, attn_norm_w[0], w_in[0], nw, cos_l, sin_l, s, col_b,
                        tl["proj_tm"], tl["proj_tn"], tl["proj_rb"])
    conv = _conv_proj(xn, t, w_in[0], conv_w[0], col_b, col_c, col_h, conv_width, s,
                      tl["conv_tm"], tl["conv_tn"], tl["conv_rb"])

    attn = _attention(qkv.reshape(b, s, col_b), lam_vecs, subln_w[0].reshape(1, V_DIM),
                      tl["attn_tq"], tl["attn_tk"], tl["attn_pw"], tl["attn_ahead"])

    h1, fn = _out_proj(attn.reshape(t, attn_w), conv, w_out[0], x2, ffn_norm_w[0],
                       tl["out_tm"], tl["out_rb"])
    hid = _gate_up(fn, w_gate[0], w_up[0], tl["gu_tm"], tl["gu_tn"])
    out = _down(hid, w_down[0], h1, tl["down_tm"], tl["down_tn"])
    return out.reshape(b, s, d)
```

```python
import functools
import math

import jax
import jax.numpy as jnp
from jax import lax
from jax.experimental import pallas as pl
from jax.experimental.pallas import tpu as pltpu

F32 = jnp.float32
BF16 = jnp.bfloat16

CHUNK = 64
HEAD_DIM = 64
V_DIM = 2 * HEAD_DIM
N_HEADS = 8
CONV_K = 3
ROPE_THETA = 10000.0
EPS = 1e-6
SUBLN_EPS = 1e-5
LAMBDA_INIT = 0.8 - 0.6 * math.exp(-0.3 * 0)
MASKED_SCORE = -1e30

LANES = 128
SUBLANES = 8
BF16_SUBLANES = 16
VMEM_LIMIT = 60 * 1024 * 1024


def _params(semantics):
    return pltpu.CompilerParams(dimension_semantics=semantics,
                                vmem_limit_bytes=VMEM_LIMIT)


def _mm(a_bf16, w_f32):
    return jnp.dot(a_bf16, w_f32.astype(BF16), preferred_element_type=F32)


def _rms_scale(x, w, eps):
    ms = jnp.mean(x * x, axis=-1, keepdims=True)
    return (x * lax.rsqrt(ms + eps)) * w


def _row_block_pipeline(n_blocks, matmul, epilogue):
    acc = matmul(0)
    for r in range(n_blocks):
        nxt = matmul(r + 1) if r + 1 < n_blocks else None
        epilogue(r, acc)
        acc = nxt


def _qkv_kernel(x_ref, anw_ref, w_ref, nw_ref, cos_ref, sin_ref, o_ref, xn_ref, w_scr,
                *, rb, n_qk_tiles):
    j = pl.program_id(0)
    tm, tn = o_ref.shape

    @pl.when(pl.program_id(1) == 0)
    def _():
        w_scr[...] = w_ref[...].astype(BF16)

    lane = lax.broadcasted_iota(jnp.int32, (1, LANES), 1)
    lo_half = lane < HEAD_DIM
    first = (lane % HEAD_DIM) < (HEAD_DIM // 2)

    def matmul(r):
        rows = slice(r * rb, (r + 1) * rb)
        xn = _rms_scale(x_ref[rows, :], anw_ref[...], EPS).astype(BF16)
        xn_ref[rows, :] = xn
        return jnp.dot(xn, w_scr[...], preferred_element_type=F32)

    def plain(r, acc):
        o_ref[r * rb:(r + 1) * rb, :] = acc.astype(o_ref.dtype)

    def norm_rope(r, acc):
        rows = slice(r * rb, (r + 1) * rb)
        cos = cos_ref[rows, :]
        sin = sin_ref[rows, :]
        for c in range(tn // LANES):
            cols = slice(c * LANES, (c + 1) * LANES)
            y = acc[:, cols]
            sq = y * y
            s_lo = jnp.sum(jnp.where(lo_half, sq, 0.0), axis=-1, keepdims=True)
            s_hi = jnp.sum(jnp.where(lo_half, 0.0, sq), axis=-1, keepdims=True)
            ms = jnp.where(lo_half, s_lo, s_hi) * (1.0 / HEAD_DIM)
            yn = (y * lax.rsqrt(ms + EPS)) * nw_ref[:, cols]
            rot = jnp.where(first, pltpu.roll(yn, LANES - HEAD_DIM // 2, 1),
                            pltpu.roll(yn, HEAD_DIM // 2, 1))
            o_ref[rows, cols] = (yn * cos + rot * sin).astype(o_ref.dtype)

    @pl.when(j < n_qk_tiles)
    def _():
        _row_block_pipeline(tm // rb, matmul, norm_rope)

    @pl.when(j >= n_qk_tiles)
    def _():
        _row_block_pipeline(tm // rb, matmul, plain)


def _qkv_proj(x, attn_norm_w, w_in, nw, cos, sin, seq, n_out, tm, tn, rb):
    t, d = x.shape
    n_i = t // tm
    n_qk_tiles = nw.shape[1] // tn
    tiles_per_seq = seq // tm
    kern = functools.partial(_qkv_kernel, rb=rb, n_qk_tiles=n_qk_tiles)
    return pl.pallas_call(
        kern,
        grid=(n_out // tn, n_i),
        in_specs=[pl.BlockSpec((tm, d), lambda j, i: (i, 0)),
                  pl.BlockSpec((1, d), lambda j, i: (0, 0)),
                  pl.BlockSpec((d, tn), lambda j, i: (0, j)),
                  pl.BlockSpec((1, tn), lambda j, i: (0, jnp.minimum(j, n_qk_tiles - 1))),
                  pl.BlockSpec((tm, LANES), lambda j, i: (i % tiles_per_seq, 0)),
                  pl.BlockSpec((tm, LANES), lambda j, i: (i % tiles_per_seq, 0))],
        out_specs=[pl.BlockSpec((tm, tn), lambda j, i: (i, j)),
                   pl.BlockSpec((tm, d), lambda j, i: (jnp.where(j == 0, i, n_i), 0))],
        out_shape=[jax.ShapeDtypeStruct((t, n_out), BF16),
                   jax.ShapeDtypeStruct((t + tm, d), BF16)],
        scratch_shapes=[pltpu.VMEM((d, tn), BF16)],
        compiler_params=_params(("arbitrary", "arbitrary")),
        name="qkv_proj",
    )(x, attn_norm_w.reshape(1, d), w_in, nw, cos, sin)


def _conv_kernel(xn_ref, wb_ref, wc_ref, wh_ref, cw_ref, o_ref,
                 wb_scr, wc_scr, wh_scr, u_scr, carry_scr, *, tiles_per_seq, rb):
    i = pl.program_id(1)
    tm = xn_ref.shape[0]

    @pl.when(i == 0)
    def _():
        wc_scr[...] = wc_ref[...].astype(BF16)
        wh_scr[...] = wh_ref[...].astype(BF16)
        wb_scr[...] = wb_ref[...].astype(BF16)

    @pl.when(i % tiles_per_seq == 0)
    def _():
        u_scr[0:SUBLANES, :] = jnp.zeros((SUBLANES, u_scr.shape[1]), F32)

    @pl.when(i % tiles_per_seq != 0)
    def _():
        u_scr[0:SUBLANES, :] = carry_scr[...]

    def xn_rows(r):
        return xn_ref[r * rb:(r + 1) * rb, :]

    def gate_matmuls(r):
        return (jnp.dot(xn_rows(r), wc_scr[...], preferred_element_type=F32),
                jnp.dot(xn_rows(r), wh_scr[...], preferred_element_type=F32))

    def store_u(r, acc):
        u_scr[SUBLANES + r * rb:SUBLANES + (r + 1) * rb, :] = acc[0] * acc[1]

    _row_block_pipeline(tm // rb, gate_matmuls, store_u)
    carry_scr[...] = u_scr[tm:tm + SUBLANES, :]
    cw = cw_ref[...]

    def b_matmul(r):
        return jnp.dot(xn_rows(r), wb_scr[...], preferred_element_type=F32)

    def conv_out(r, gate_b):
        def shifted(back):
            start = SUBLANES + r * rb - back
            return u_scr[start:start + rb, :]
        y = cw[2:3, :] * shifted(0) + cw[1:2, :] * shifted(1) + cw[0:1, :] * shifted(2)
        o_ref[r * rb:(r + 1) * rb, :] = (gate_b * y).astype(o_ref.dtype)

    _row_block_pipeline(tm // rb, b_matmul, conv_out)


def _conv_proj(xn, t, w_in, conv_w, col_b, col_c, col_h, n, seq, tm, tn, rb):
    d = xn.shape[1]
    jb, jc, jh = col_b // tn, col_c // tn, col_h // tn
    nj = n // tn
    kern = functools.partial(_conv_kernel, tiles_per_seq=seq // tm, rb=rb)
    w_scratch = [pltpu.VMEM((d, tn), BF16)] * 3
    return pl.pallas_call(
        kern,
        grid=(nj, t // tm),
        in_specs=[pl.BlockSpec((tm, d), lambda j, i: (i, 0)),
                  pl.BlockSpec((d, tn), lambda j, i: (0, jb + j)),
                  pl.BlockSpec((d, tn), lambda j, i: (0, jc + j)),
                  pl.BlockSpec((d, tn), lambda j, i: (0, jh + j)),
                  pl.BlockSpec((CONV_K, tn), lambda j, i: (0, j))],
        out_specs=pl.BlockSpec((tm, tn), lambda j, i: (i, j)),
        out_shape=jax.ShapeDtypeStruct((t, n), BF16),
        scratch_shapes=w_scratch + [pltpu.VMEM((tm + SUBLANES, tn), F32),
                                    pltpu.VMEM((SUBLANES, tn), F32)],
        compiler_params=_params(("arbitrary", "arbitrary")),
        name="conv_proj",
    )(xn, w_in, w_in, w_in, conv_w)


def _attn_kernel(q_ref, k_ref, v_ref, lam_ref, sw_ref, o_ref,
                 vt_scr, bias_scr, qq_a, qq_b, m_a, m_b, acc_a, acc_b, *s_bufs,
                 tq, tk, pw, ahead):
    seq = k_ref.shape[1]
    n_q = seq // tq
    n_diag = tq // tk
    n_buf = len(s_bufs)
    n_panels = 2 * tq // pw
    assert ahead < n_buf
    assert 2 * n_diag >= ahead
    assert n_diag >= 2, "query tiles of a single key tile fail on the chip; not supported"
    all_visible = ("full",) * n_panels
    state = ((qq_a, m_a, acc_a), (qq_b, m_b, acc_b))

    for c in range(seq // LANES):
        blk = v_ref[0, c * LANES:(c + 1) * LANES, :].astype(F32)
        vt_scr[0:V_DIM, c * LANES:(c + 1) * LANES] = blk.T.astype(BF16)
    vt_scr[V_DIM:, :] = jnp.ones((vt_scr.shape[0] - V_DIM, seq), BF16)
    key = lax.broadcasted_iota(jnp.int32, bias_scr.shape, 0)
    qry = lax.broadcasted_iota(jnp.int32, bias_scr.shape, 1)
    bias_scr[...] = jnp.where(key // CHUNK <= qry // CHUNK, 0.0, MASKED_SCORE)

    lq1, lk1, lq2, lk2 = (lam_ref[r:r + 1, :] for r in range(4))
    lam = (jnp.exp(jnp.sum(lq1 * lk1, axis=-1, keepdims=True))
           - jnp.exp(jnp.sum(lq2 * lk2, axis=-1, keepdims=True)) + LAMBDA_INIT)

    def diag_modes(key_off):
        modes = []
        for c in range(n_panels):
            q_off = (c * pw) % tq
            if key_off + tk <= q_off:
                modes.append("full")
            elif key_off >= q_off + pw:
                modes.append("skip")
            else:
                assert key_off == q_off and tk == pw
                modes.append("tri")
        return tuple(modes)

    def prepare(qi):
        qq_scr, m_scr, acc_scr = state[qi % 2]
        dim = lax.broadcasted_iota(jnp.int32, (V_DIM, 1), 0)
        for c in range(tq // LANES):
            rows = slice(qi * tq + c * LANES, qi * tq + (c + 1) * LANES)
            qt = q_ref[0, rows, :].astype(F32).T
            cols = slice(c * LANES, (c + 1) * LANES)
            qq_scr[:, cols] = jnp.where(dim < HEAD_DIM, qt, 0.0).astype(BF16)
            cols = slice(tq + c * LANES, tq + (c + 1) * LANES)
            qq_scr[:, cols] = jnp.where(dim < HEAD_DIM, 0.0, qt).astype(BF16)
        m_scr[...] = jnp.full(m_scr.shape, -jnp.inf, F32)
        acc_scr[...] = jnp.zeros(acc_scr.shape, F32)

    stages = []
    for qi in range(n_q):
        n_full = n_diag * qi
        for t in range(n_full + n_diag):
            modes = all_visible if t < n_full else diag_modes((t - n_full) * tk)
            stages.append((qi, t, modes))

    def scores(g, c):
        qi, t, modes = stages[g]
        if modes[c] == "skip":
            return
        cols = slice(c * pw, (c + 1) * pw)
        s_bufs[g % n_buf][:, cols] = jnp.dot(
            k_ref[0, t * tk:(t + 1) * tk, :], state[qi % 2][0][:, cols],
            preferred_element_type=F32)

    def update(g, c):
        qi, t, modes = stages[g]
        if modes[c] == "skip":
            return
        _, m_scr, acc_scr = state[qi % 2]
        cols = slice(c * pw, (c + 1) * pw)
        s = s_bufs[g % n_buf][:, cols]
        if modes[c] == "tri":
            s = s + bias_scr[...]
        m_prev = m_scr[:, cols]
        m_new = jnp.maximum(m_prev, jnp.max(s, axis=0, keepdims=True))
        alpha = jnp.exp2(m_prev - m_new)
        p = jnp.exp2(s - m_new).astype(BF16)
        m_scr[:, cols] = m_new
        acc_scr[:, cols] = alpha * acc_scr[:, cols] + jnp.dot(
            vt_scr[:, t * tk:(t + 1) * tk], p, preferred_element_type=F32)

    def finish(qi):
        acc = state[qi % 2][2][...]
        o = acc[0:V_DIM, :] / acc[V_DIM:V_DIM + 1, :]
        a = (o[:, 0:tq] - lam * o[:, tq:]).T
        ms = jnp.mean(a * a, axis=-1, keepdims=True)
        o_ref[0, qi * tq:(qi + 1) * tq, :] = (
            ((a * lax.rsqrt(ms + SUBLN_EPS)) * sw_ref[...])
            * (1.0 - LAMBDA_INIT)).astype(o_ref.dtype)

    def start(g):
        qi, t, _ = stages[g]
        if t == 0:
            prepare(qi)

    for g in range(ahead):
        start(g)
        for c in range(n_panels):
            scores(g, c)
    for g, (qi, t, _) in enumerate(stages):
        if g + ahead < len(stages):
            start(g + ahead)
        for c in range(n_panels):
            update(g, c)
            if g + ahead < len(stages):
                scores(g + ahead, c)
        if t == n_diag * (qi + 1) - 1:
            finish(qi)


def _attention(qkv, lam_vecs, subln_w, tq, tk, pw, ahead):
    b, s, w3 = qkv.shape
    w = w3 // 3
    h = w // V_DIM
    kern = functools.partial(_attn_kernel, tq=tq, tk=tk, pw=pw, ahead=ahead)

    def head_block(first):
        return pl.BlockSpec((1, s, V_DIM), lambda bi, hi: (bi, 0, first + hi))

    per_q_tile = ([pltpu.VMEM((V_DIM, 2 * tq), BF16)] * 2
                  + [pltpu.VMEM((1, 2 * tq), F32)] * 2
                  + [pltpu.VMEM((V_DIM + BF16_SUBLANES, 2 * tq), F32)] * 2)
    return pl.pallas_call(
        kern,
        grid=(b, h),
        in_specs=[head_block(0), head_block(h), head_block(2 * h),
                  pl.BlockSpec((4, HEAD_DIM), lambda bi, hi: (0, 0)),
                  pl.BlockSpec((1, V_DIM), lambda bi, hi: (0, 0))],
        out_specs=head_block(0),
        out_shape=jax.ShapeDtypeStruct((b, s, w), BF16),
        scratch_shapes=[pltpu.VMEM((V_DIM + BF16_SUBLANES, s), BF16),
                        pltpu.VMEM((tk, pw), F32)]
        + per_q_tile
        + [pltpu.VMEM((tk, 2 * tq), F32)] * (ahead + 1),
        compiler_params=_params(("parallel", "parallel")),
        name="diff_attention",
    )(qkv, qkv, qkv, lam_vecs, subln_w)


def _out_kernel(a_ref, c_ref, w_ref, x_ref, fnw_ref, o_ref, fn_ref, w_scr, *, rb):
    @pl.when(pl.program_id(0) == 0)
    def _():
        w_scr[...] = w_ref[...].astype(BF16)

    tm = o_ref.shape[0]
    ka = a_ref.shape[1]

    def matmul(r):
        rows = slice(r * rb, (r + 1) * rb)
        return (x_ref[rows, :]
                + jnp.dot(a_ref[rows, :], w_scr[0:ka, :], preferred_element_type=F32)
                + jnp.dot(c_ref[rows, :], w_scr[ka:, :], preferred_element_type=F32))

    def store(r, h):
        rows = slice(r * rb, (r + 1) * rb)
        o_ref[rows, :] = h
        fn_ref[rows, :] = _rms_scale(h, fnw_ref[...], EPS).astype(fn_ref.dtype)

    _row_block_pipeline(tm // rb, matmul, store)


def _out_proj(attn, conv, w_out, x, ffn_norm_w, tm, rb):
    t, ka = attn.shape
    kc = conv.shape[1]
    k, n = w_out.shape
    assert ka + kc == k
    return pl.pallas_call(
        functools.partial(_out_kernel, rb=rb),
        grid=(t // tm,),
        in_specs=[pl.BlockSpec((tm, ka), lambda i: (i, 0)),
                  pl.BlockSpec((tm, kc), lambda i: (i, 0)),
                  pl.BlockSpec((k, n), lambda i: (0, 0), pipeline_mode=pl.Buffered(1)),
                  pl.BlockSpec((tm, n), lambda i: (i, 0)),
                  pl.BlockSpec((1, n), lambda i: (0, 0))],
        out_specs=[pl.BlockSpec((tm, n), lambda i: (i, 0)),
                   pl.BlockSpec((tm, n), lambda i: (i, 0))],
        out_shape=[jax.ShapeDtypeStruct((t, n), F32),
                   jax.ShapeDtypeStruct((t, n), BF16)],
        scratch_shapes=[pltpu.VMEM((k, n), BF16)],
        compiler_params=_params(("arbitrary",)),
        name="out_proj",
    )(attn, conv, w_out, x, ffn_norm_w.reshape(1, n))


def _gate_up_kernel(x_ref, wg_ref, wu_ref, o_ref):
    x = x_ref[...]
    g = _mm(x, wg_ref[...])
    u = _mm(x, wu_ref[...])
    o_ref[...] = ((g * jax.nn.sigmoid(g)) * u).astype(o_ref.dtype)


def _gate_up(fn, w_gate, w_up, tm, tn):
    t, d = fn.shape
    f = w_gate.shape[1]
    return pl.pallas_call(
        _gate_up_kernel,
        grid=(t // tm, f // tn),
        in_specs=[pl.BlockSpec((tm, d), lambda i, j: (i, 0)),
                  pl.BlockSpec((d, tn), lambda i, j: (0, j)),
                  pl.BlockSpec((d, tn), lambda i, j: (0, j))],
        out_specs=pl.BlockSpec((tm, tn), lambda i, j: (i, j)),
        out_shape=jax.ShapeDtypeStruct((t, f), BF16),
        compiler_params=_params(("parallel", "parallel")),
        name="gate_up",
    )(fn, w_gate, w_up)


def _down_kernel(h_ref, w_ref, r_ref, o_ref, w_scr):
    @pl.when(pl.program_id(1) == 0)
    def _():
        w_scr[...] = w_ref[...].astype(BF16)

    o_ref[...] = r_ref[...] + jnp.dot(h_ref[...], w_scr[...],
                                      preferred_element_type=F32)


def _down(h, w_down, resid, tm, tn):
    t, f = h.shape
    n = w_down.shape[1]
    return pl.pallas_call(
        _down_kernel,
        grid=(n // tn, t // tm),
        in_specs=[pl.BlockSpec((tm, f), lambda j, i: (i, 0)),
                  pl.BlockSpec((f, tn), lambda j, i: (0, j), pipeline_mode=pl.Buffered(1)),
                  pl.BlockSpec((tm, tn), lambda j, i: (i, j))],
        out_specs=pl.BlockSpec((tm, tn), lambda j, i: (i, j)),
        out_shape=jax.ShapeDtypeStruct((t, n), F32),
        scratch_shapes=[pltpu.VMEM((f, tn), BF16)],
        compiler_params=_params(("arbitrary", "arbitrary")),
        name="down_proj",
    )(h, w_down, resid)


TILES = dict(
    proj_tm=1024, proj_tn=1024, proj_rb=256,
    conv_tm=1024, conv_tn=512, conv_rb=256,
    attn_tq=512, attn_tk=256, attn_pw=256, attn_ahead=2,
    out_tm=512, out_rb=256,
    gu_tm=1024, gu_tn=512,
    down_tm=512, down_tn=1024,
)


def _rope_tables(seq):
    pos = jnp.arange(seq, dtype=F32)
    inv_freq = ROPE_THETA ** (-jnp.arange(0, HEAD_DIM, 2, dtype=F32) / HEAD_DIM)
    ang = pos[:, None] * inv_freq[None, :]
    cos, sin = jnp.cos(ang), jnp.sin(ang)
    reps = LANES // (HEAD_DIM // 2)
    cos_l = jnp.tile(cos, (1, reps))
    sin_l = jnp.tile(jnp.concatenate([-sin, sin], axis=1), (1, reps // 2))
    return cos_l, sin_l


def kernel(x, attn_norm_w, w_in, q_norm_w, k_norm_w, lambda_q1, lambda_k1,
           lambda_q2, lambda_k2, subln_w, conv_w, w_out, ffn_norm_w, w_gate,
           w_up, w_down):
    b, s, d = x.shape
    t = b * s
    attn_w = N_HEADS * V_DIM
    qk_cols = N_HEADS * 2 * HEAD_DIM
    conv_width = d - attn_w
    col_v = 2 * qk_cols
    col_b = col_v + attn_w
    col_c = col_b + conv_width
    col_h = col_c + conv_width
    tl = TILES
    assert w_in.shape[0] == 1, "single-layer block"

    x2 = x.reshape(t, d)
    cos_l, sin_l = _rope_tables(s)
    scale = math.log2(math.e) / math.sqrt(HEAD_DIM)
    nw = jnp.concatenate([jnp.tile(q_norm_w[0] * scale, qk_cols // HEAD_DIM),
                          jnp.tile(k_norm_w[0], qk_cols // HEAD_DIM)]).reshape(1, -1)
    lam_vecs = jnp.concatenate([lambda_q1, lambda_k1, lambda_q2, lambda_k2], axis=0)

    qkv, xn = _qkv_proj(x2, attn_norm_w[0], w_in[0], nw, cos_l, sin_l, s, col_b,
                        tl["proj_tm"], tl["proj_tn"], tl["proj_rb"])
    conv = _conv_proj(xn, t, w_in[0], conv_w[0], col_b, col_c, col_h, conv_width, s,
                      tl["conv_tm"], tl["conv_tn"], tl["conv_rb"])

    attn = _attention(qkv.reshape(b, s, col_b), lam_vecs, subln_w[0].reshape(1, V_DIM),
                      tl["attn_tq"], tl["attn_tk"], tl["attn_pw"], tl["attn_ahead"])

    h1, fn = _out_proj(attn.reshape(t, attn_w), conv, w_out[0], x2, ffn_norm_w[0],
                       tl["out_tm"], tl["out_rb"])
    hid = _gate_up(fn, w_gate[0], w_up[0], tl["gu_tm"], tl["gu_tn"])
    out = _down(hid, w_down[0], h1, tl["down_tm"], tl["down_tn"])
    return out.reshape(b, s, d)
```

```python
import functools
import math

import jax
import jax.numpy as jnp
from jax import lax
from jax.experimental import pallas as pl
from jax.experimental.pallas import tpu as pltpu

F32 = jnp.float32
BF16 = jnp.bfloat16

CHUNK = 64
HEAD_DIM = 64
V_DIM = 2 * HEAD_DIM
N_HEADS = 8
CONV_K = 3
ROPE_THETA = 10000.0
EPS = 1e-6
SUBLN_EPS = 1e-5
LAMBDA_INIT = 0.8 - 0.6 * math.exp(-0.3 * 0)
MASKED_SCORE = -1e30

LANES = 128
SUBLANES = 8
BF16_SUBLANES = 16
VMEM_LIMIT = 60 * 1024 * 1024


def _params(semantics):
    return pltpu.CompilerParams(dimension_semantics=semantics,
                                vmem_limit_bytes=VMEM_LIMIT)


def _mm(a_bf16, w_f32):
    return jnp.dot(a_bf16, w_f32.astype(BF16), preferred_element_type=F32)


def _rms_scale(x, w, eps):
    ms = jnp.mean(x * x, axis=-1, keepdims=True)
    return (x * lax.rsqrt(ms + eps)) * w


def _row_block_pipeline(n_blocks, matmul, epilogue):
    acc = matmul(0)
    for r in range(n_blocks):
        nxt = matmul(r + 1) if r + 1 < n_blocks else None
        epilogue(r, acc)
        acc = nxt


def _qkv_kernel(x_ref, anw_ref, w_ref, nw_ref, cos_ref, sin_ref, o_ref, xn_ref, w_scr,
                *, rb, n_qk_tiles):
    j = pl.program_id(0)
    tm, tn = o_ref.shape

    @pl.when(pl.program_id(1) == 0)
    def _():
        w_scr[...] = w_ref[...].astype(BF16)

    lane = lax.broadcasted_iota(jnp.int32, (1, LANES), 1)
    lo_half = lane < HEAD_DIM
    first = (lane % HEAD_DIM) < (HEAD_DIM // 2)

    def matmul(r):
        rows = slice(r * rb, (r + 1) * rb)
        xn = _rms_scale(x_ref[rows, :], anw_ref[...], EPS).astype(BF16)
        xn_ref[rows, :] = xn
        return jnp.dot(xn, w_scr[...], preferred_element_type=F32)

    def plain(r, acc):
        o_ref[r * rb:(r + 1) * rb, :] = acc.astype(o_ref.dtype)

    def norm_rope(r, acc):
        rows = slice(r * rb, (r + 1) * rb)
        cos = cos_ref[rows, :]
        sin = sin_ref[rows, :]
        for c in range(tn // LANES):
            cols = slice(c * LANES, (c + 1) * LANES)
            y = acc[:, cols]
            sq = y * y
            s_lo = jnp.sum(jnp.where(lo_half, sq, 0.0), axis=-1, keepdims=True)
            s_hi = jnp.sum(jnp.where(lo_half, 0.0, sq), axis=-1, keepdims=True)
            ms = jnp.where(lo_half, s_lo, s_hi) * (1.0 / HEAD_DIM)
            yn = (y * lax.rsqrt(ms + EPS)) * nw_ref[:, cols]
            rot = jnp.where(first, pltpu.roll(yn, LANES - HEAD_DIM // 2, 1),
                            pltpu.roll(yn, HEAD_DIM // 2, 1))
            o_ref[rows, cols] = (yn * cos + rot * sin).astype(o_ref.dtype)

    @pl.when(j < n_qk_tiles)
    def _():
        _row_block_pipeline(tm // rb, matmul, norm_rope)

    @pl.when(j >= n_qk_tiles)
    def _():
        _row_block_pipeline(tm // rb, matmul, plain)


def _qkv_proj(x, attn_norm_w, w_in, nw, cos, sin, seq, n_out, tm, tn, rb):
    t, d = x.shape
    n_i = t // tm
    n_qk_tiles = nw.shape[1] // tn
    tiles_per_seq = seq // tm
    kern = functools.partial(_qkv_kernel, rb=rb, n_qk_tiles=n_qk_tiles)
    return pl.pallas_call(
        kern,
        grid=(n_out // tn, n_i),
        in_specs=[pl.BlockSpec((tm, d), lambda j, i: (i, 0)),
                  pl.BlockSpec((1, d), lambda j, i: (0, 0)),
                  pl.BlockSpec((d, tn), lambda j, i: (0, j)),
                  pl.BlockSpec((1, tn), lambda j, i: (0, jnp.minimum(j, n_qk_tiles - 1))),
                  pl.BlockSpec((tm, LANES), lambda j, i: (i % tiles_per_seq, 0)),
                  pl.BlockSpec((tm, LANES), lambda j, i: (i % tiles_per_seq, 0))],
        out_specs=[pl.BlockSpec((tm, tn), lambda j, i: (i, j)),
                   pl.BlockSpec((tm, d), lambda j, i: (jnp.where(j == 0, i, n_i), 0))],
        out_shape=[jax.ShapeDtypeStruct((t, n_out), BF16),
                   jax.ShapeDtypeStruct((t + tm, d), BF16)],
        scratch_shapes=[pltpu.VMEM((d, tn), BF16)],
        compiler_params=_params(("arbitrary", "arbitrary")),
        name="qkv_proj",
    )(x, attn_norm_w.reshape(1, d), w_in, nw, cos, sin)


def _conv_kernel(xn_ref, wb_ref, wc_ref, wh_ref, cw_ref, o_ref,
                 wb_scr, wc_scr, wh_scr, u_scr, carry_scr, *, tiles_per_seq, rb):
    i = pl.program_id(1)
    tm = xn_ref.shape[0]

    @pl.when(i == 0)
    def _():
        wc_scr[...] = wc_ref[...].astype(BF16)
        wh_scr[...] = wh_ref[...].astype(BF16)
        wb_scr[...] = wb_ref[...].astype(BF16)

    @pl.when(i % tiles_per_seq == 0)
    def _():
        u_scr[0:SUBLANES, :] = jnp.zeros((SUBLANES, u_scr.shape[1]), F32)

    @pl.when(i % tiles_per_seq != 0)
    def _():
        u_scr[0:SUBLANES, :] = carry_scr[...]

    def xn_rows(r):
        return xn_ref[r * rb:(r + 1) * rb, :]

    cw = cw_ref[...]

    def matmuls(r):
        x = xn_rows(r)
        return (jnp.dot(x, wc_scr[...], preferred_element_type=F32),
                jnp.dot(x, wh_scr[...], preferred_element_type=F32),
                jnp.dot(x, wb_scr[...], preferred_element_type=F32))

    def conv_out(r, acc):
        gate_c, h, gate_b = acc
        u = gate_c * h
        u_scr[SUBLANES + r * rb:SUBLANES + (r + 1) * rb, :] = u

        def shifted(back):
            start = SUBLANES + r * rb - back
            return u_scr[start:start + rb, :]
        y = cw[2:3, :] * u + cw[1:2, :] * shifted(1) + cw[0:1, :] * shifted(2)
        o_ref[r * rb:(r + 1) * rb, :] = (gate_b * y).astype(o_ref.dtype)

    _row_block_pipeline(tm // rb, matmuls, conv_out)
    carry_scr[...] = u_scr[tm:tm + SUBLANES, :]


def _conv_proj(xn, t, w_in, conv_w, col_b, col_c, col_h, n, seq, tm, tn, rb):
    d = xn.shape[1]
    jb, jc, jh = col_b // tn, col_c // tn, col_h // tn
    nj = n // tn
    kern = functools.partial(_conv_kernel, tiles_per_seq=seq // tm, rb=rb)
    w_scratch = [pltpu.VMEM((d, tn), BF16)] * 3
    return pl.pallas_call(
        kern,
        grid=(nj, t // tm),
        in_specs=[pl.BlockSpec((tm, d), lambda j, i: (i, 0)),
                  pl.BlockSpec((d, tn), lambda j, i: (0, jb + j)),
                  pl.BlockSpec((d, tn), lambda j, i: (0, jc + j)),
                  pl.BlockSpec((d, tn), lambda j, i: (0, jh + j)),
                  pl.BlockSpec((CONV_K, tn), lambda j, i: (0, j))],
        out_specs=pl.BlockSpec((tm, tn), lambda j, i: (i, j)),
        out_shape=jax.ShapeDtypeStruct((t, n), BF16),
        scratch_shapes=w_scratch + [pltpu.VMEM((tm + SUBLANES, tn), F32),
                                    pltpu.VMEM((SUBLANES, tn), F32)],
        compiler_params=_params(("arbitrary", "arbitrary")),
        name="conv_proj",
    )(xn, w_in, w_in, w_in, conv_w)


def _attn_kernel(q_ref, k_ref, v_ref, lam_ref, sw_ref, o_ref,
                 vt_scr, bias_scr, qq_a, qq_b, m_a, m_b, acc_a, acc_b, *s_bufs,
                 tq, tk, pw, ahead):
    seq = k_ref.shape[1]
    n_q = seq // tq
    n_diag = tq // tk
    n_buf = len(s_bufs)
    n_panels = 2 * tq // pw
    assert ahead < n_buf
    assert 2 * n_diag >= ahead
    assert n_diag >= 2, "query tiles of a single key tile fail on the chip; not supported"
    all_visible = ("full",) * n_panels
    state = ((qq_a, m_a, acc_a), (qq_b, m_b, acc_b))

    for c in range(seq // LANES):
        blk = v_ref[0, c * LANES:(c + 1) * LANES, :].astype(F32)
        vt_scr[0:V_DIM, c * LANES:(c + 1) * LANES] = blk.T.astype(BF16)
    vt_scr[V_DIM:, :] = jnp.ones((vt_scr.shape[0] - V_DIM, seq), BF16)
    key = lax.broadcasted_iota(jnp.int32, bias_scr.shape, 0)
    qry = lax.broadcasted_iota(jnp.int32, bias_scr.shape, 1)
    bias_scr[...] = jnp.where(key // CHUNK <= qry // CHUNK, 0.0, MASKED_SCORE)

    lq1, lk1, lq2, lk2 = (lam_ref[r:r + 1, :] for r in range(4))
    lam = (jnp.exp(jnp.sum(lq1 * lk1, axis=-1, keepdims=True))
           - jnp.exp(jnp.sum(lq2 * lk2, axis=-1, keepdims=True)) + LAMBDA_INIT)

    def diag_modes(key_off):
        modes = []
        for c in range(n_panels):
            q_off = (c * pw) % tq
            if key_off + tk <= q_off:
                modes.append("full")
            elif key_off >= q_off + pw:
                modes.append("skip")
            else:
                assert key_off == q_off and tk == pw
                modes.append("tri")
        return tuple(modes)

    def prepare(qi):
        qq_scr, m_scr, acc_scr = state[qi % 2]
        dim = lax.broadcasted_iota(jnp.int32, (V_DIM, 1), 0)
        for c in range(tq // LANES):
            rows = slice(qi * tq + c * LANES, qi * tq + (c + 1) * LANES)
            qt = q_ref[0, rows, :].astype(F32).T
            cols = slice(c * LANES, (c + 1) * LANES)
            qq_scr[:, cols] = jnp.where(dim < HEAD_DIM, qt, 0.0).astype(BF16)
            cols = slice(tq + c * LANES, tq + (c + 1) * LANES)
            qq_scr[:, cols] = jnp.where(dim < HEAD_DIM, 0.0, qt).astype(BF16)
        m_scr[...] = jnp.full(m_scr.shape, -jnp.inf, F32)
        acc_scr[...] = jnp.zeros(acc_scr.shape, F32)

    stages = []
    for qi in range(n_q):
        n_full = n_diag * qi
        for t in range(n_full + n_diag):
            modes = all_visible if t < n_full else diag_modes((t - n_full) * tk)
            stages.append((qi, t, modes))

    def scores(g, c):
        qi, t, modes = stages[g]
        if modes[c] == "skip":
            return
        cols = slice(c * pw, (c + 1) * pw)
        s_bufs[g % n_buf][:, cols] = jnp.dot(
            k_ref[0, t * tk:(t + 1) * tk, :], state[qi % 2][0][:, cols],
            preferred_element_type=F32)

    def update(g, c):
        qi, t, modes = stages[g]
        if modes[c] == "skip":
            return
        _, m_scr, acc_scr = state[qi % 2]
        cols = slice(c * pw, (c + 1) * pw)
        s = s_bufs[g % n_buf][:, cols]
        if modes[c] == "tri":
            s = s + bias_scr[...]
        m_prev = m_scr[:, cols]
        m_new = jnp.maximum(m_prev, jnp.max(s, axis=0, keepdims=True))
        alpha = jnp.exp2(m_prev - m_new)
        p = jnp.exp2(s - m_new).astype(BF16)
        m_scr[:, cols] = m_new
        acc_scr[:, cols] = alpha * acc_scr[:, cols] + jnp.dot(
            vt_scr[:, t * tk:(t + 1) * tk], p, preferred_element_type=F32)

    def finish(qi):
        acc = state[qi % 2][2][...]
        o = acc[0:V_DIM, :] / acc[V_DIM:V_DIM + 1, :]
        a = (o[:, 0:tq] - lam * o[:, tq:]).T
        ms = jnp.mean(a * a, axis=-1, keepdims=True)
        o_ref[0, qi * tq:(qi + 1) * tq, :] = (
            ((a * lax.rsqrt(ms + SUBLN_EPS)) * sw_ref[...])
            * (1.0 - LAMBDA_INIT)).astype(o_ref.dtype)

    def start(g):
        qi, t, _ = stages[g]
        if t == 0:
            prepare(qi)

    for g in range(ahead):
        start(g)
        for c in range(n_panels):
            scores(g, c)
    for g, (qi, t, _) in enumerate(stages):
        if g + ahead < len(stages):
            start(g + ahead)
        for c in range(n_panels):
            update(g, c)
            if g + ahead < len(stages):
                scores(g + ahead, c)
        if t == n_diag * (qi + 1) - 1:
            finish(qi)


def _attention(qkv, lam_vecs, subln_w, tq, tk, pw, ahead):
    b, s, w3 = qkv.shape
    w = w3 // 3
    h = w // V_DIM
    kern = functools.partial(_attn_kernel, tq=tq, tk=tk, pw=pw, ahead=ahead)

    def head_block(first):
        return pl.BlockSpec((1, s, V_DIM), lambda bi, hi: (bi, 0, first + hi))

    per_q_tile = ([pltpu.VMEM((V_DIM, 2 * tq), BF16)] * 2
                  + [pltpu.VMEM((1, 2 * tq), F32)] * 2
                  + [pltpu.VMEM((V_DIM + BF16_SUBLANES, 2 * tq), F32)] * 2)
    return pl.pallas_call(
        kern,
        grid=(b, h),
        in_specs=[head_block(0), head_block(h), head_block(2 * h),
                  pl.BlockSpec((4, HEAD_DIM), lambda bi, hi: (0, 0)),
                  pl.BlockSpec((1, V_DIM), lambda bi, hi: (0, 0))],
        out_specs=head_block(0),
        out_shape=jax.ShapeDtypeStruct((b, s, w), BF16),
        scratch_shapes=[pltpu.VMEM((V_DIM + BF16_SUBLANES, s), BF16),
                        pltpu.VMEM((tk, pw), F32)]
        + per_q_tile
        + [pltpu.VMEM((tk, 2 * tq), F32)] * (ahead + 1),
        compiler_params=_params(("parallel", "parallel")),
        name="diff_attention",
    )(qkv, qkv, qkv, lam_vecs, subln_w)


def _out_kernel(a_ref, c_ref, w_ref, x_ref, fnw_ref, o_ref, fn_ref, w_scr, *, rb):
    @pl.when(pl.program_id(0) == 0)
    def _():
        w_scr[...] = w_ref[...].astype(BF16)

    tm = o_ref.shape[0]
    ka = a_ref.shape[1]

    def matmul(r):
        rows = slice(r * rb, (r + 1) * rb)
        return (x_ref[rows, :]
                + jnp.dot(a_ref[rows, :], w_scr[0:ka, :], preferred_element_type=F32)
                + jnp.dot(c_ref[rows, :], w_scr[ka:, :], preferred_element_type=F32))

    def store(r, h):
        rows = slice(r * rb, (r + 1) * rb)
        o_ref[rows, :] = h
        fn_ref[rows, :] = _rms_scale(h, fnw_ref[...], EPS).astype(fn_ref.dtype)

    _row_block_pipeline(tm // rb, matmul, store)


def _out_proj(attn, conv, w_out, x, ffn_norm_w, tm, rb):
    t, ka = attn.shape
    kc = conv.shape[1]
    k, n = w_out.shape
    assert ka + kc == k
    return pl.pallas_call(
        functools.partial(_out_kernel, rb=rb),
        grid=(t // tm,),
        in_specs=[pl.BlockSpec((tm, ka), lambda i: (i, 0)),
                  pl.BlockSpec((tm, kc), lambda i: (i, 0)),
                  pl.BlockSpec((k, n), lambda i: (0, 0), pipeline_mode=pl.Buffered(1)),
                  pl.BlockSpec((tm, n), lambda i: (i, 0)),
                  pl.BlockSpec((1, n), lambda i: (0, 0))],
        out_specs=[pl.BlockSpec((tm, n), lambda i: (i, 0)),
                   pl.BlockSpec((tm, n), lambda i: (i, 0))],
        out_shape=[jax.ShapeDtypeStruct((t, n), F32),
                   jax.ShapeDtypeStruct((t, n), BF16)],
        scratch_shapes=[pltpu.VMEM((k, n), BF16)],
        compiler_params=_params(("arbitrary",)),
        name="out_proj",
    )(attn, conv, w_out, x, ffn_norm_w.reshape(1, n))


def _gate_up_kernel(x_ref, wg_ref, wu_ref, o_ref):
    x = x_ref[...]
    g = _mm(x, wg_ref[...])
    u = _mm(x, wu_ref[...])
    o_ref[...] = ((g * jax.nn.sigmoid(g)) * u).astype(o_ref.dtype)


def _gate_up(fn, w_gate, w_up, tm, tn):
    t, d = fn.shape
    f = w_gate.shape[1]
    return pl.pallas_call(
        _gate_up_kernel,
        grid=(t // tm, f // tn),
        in_specs=[pl.BlockSpec((tm, d), lambda i, j: (i, 0)),
                  pl.BlockSpec((d, tn), lambda i, j: (0, j)),
                  pl.BlockSpec((d, tn), lambda i, j: (0, j))],
        out_specs=pl.BlockSpec((tm, tn), lambda i, j: (i, j)),
        out_shape=jax.ShapeDtypeStruct((t, f), BF16),
        compiler_params=_params(("parallel", "parallel")),
        name="gate_up",
    )(fn, w_gate, w_up)


def _down_kernel(h_ref, w_ref, r_ref, o_ref, w_scr):
    @pl.when(pl.program_id(1) == 0)
    def _():
        w_scr[...] = w_ref[...].astype(BF16)

    o_ref[...] = r_ref[...] + jnp.dot(h_ref[...], w_scr[...],
                                      preferred_element_type=F32)


def _down(h, w_down, resid, tm, tn):
    t, f = h.shape
    n = w_down.shape[1]
    return pl.pallas_call(
        _down_kernel,
        grid=(n // tn, t // tm),
        in_specs=[pl.BlockSpec((tm, f), lambda j, i: (i, 0)),
                  pl.BlockSpec((f, tn), lambda j, i: (0, j), pipeline_mode=pl.Buffered(1)),
                  pl.BlockSpec((tm, tn), lambda j, i: (i, j))],
        out_specs=pl.BlockSpec((tm, tn), lambda j, i: (i, j)),
        out_shape=jax.ShapeDtypeStruct((t, n), F32),
        scratch_shapes=[pltpu.VMEM((f, tn), BF16)],
        compiler_params=_params(("arbitrary", "arbitrary")),
        name="down_proj",
    )(h, w_down, resid)


TILES = dict(
    proj_tm=1024, proj_tn=1024, proj_rb=256,
    conv_tm=1024, conv_tn=512, conv_rb=256,
    attn_tq=512, attn_tk=256, attn_pw=256, attn_ahead=3,
    out_tm=256, out_rb=256,
    gu_tm=1024, gu_tn=512,
    down_tm=256, down_tn=1024,
)


def _rope_tables(seq):
    pos = jnp.arange(seq, dtype=F32)
    inv_freq = ROPE_THETA ** (-jnp.arange(0, HEAD_DIM, 2, dtype=F32) / HEAD_DIM)
    ang = pos[:, None] * inv_freq[None, :]
    cos, sin = jnp.cos(ang), jnp.sin(ang)
    reps = LANES // (HEAD_DIM // 2)
    cos_l = jnp.tile(cos, (1, reps))
    sin_l = jnp.tile(jnp.concatenate([-sin, sin], axis=1), (1, reps // 2))
    return cos_l, sin_l


def kernel(x, attn_norm_w, w_in, q_norm_w, k_norm_w, lambda_q1, lambda_k1,
           lambda_q2, lambda_k2, subln_w, conv_w, w_out, ffn_norm_w, w_gate,
           w_up, w_down):
    b, s, d = x.shape
    t = b * s
    attn_w = N_HEADS * V_DIM
    qk_cols = N_HEADS * 2 * HEAD_DIM
    conv_width = d - attn_w
    col_v = 2 * qk_cols
    col_b = col_v + attn_w
    col_c = col_b + conv_width
    col_h = col_c + conv_width
    tl = TILES
    assert w_in.shape[0] == 1, "single-layer block"

    x2 = x.reshape(t, d)
    cos_l, sin_l = _rope_tables(s)
    scale = math.log2(math.e) / math.sqrt(HEAD_DIM)
    nw = jnp.concatenate([jnp.tile(q_norm_w[0] * scale, qk_cols // HEAD_DIM),
                          jnp.tile(k_norm_w[0], qk_cols // HEAD_DIM)]).reshape(1, -1)
    lam_vecs = jnp.concatenate([lambda_q1, lambda_k1, lambda_q2, lambda_k2], axis=0)

    qkv, xn = _qkv_proj(x2, attn_norm_w[0], w_in[0], nw, cos_l, sin_l, s, col_b,
                        tl["proj_tm"], tl["proj_tn"], tl["proj_rb"])
    conv = _conv_proj(xn, t, w_in[0], conv_w[0], col_b, col_c, col_h, conv_width, s,
                      tl["conv_tm"], tl["conv_tn"], tl["conv_rb"])

    attn = _attention(qkv.reshape(b, s, col_b), lam_vecs, subln_w[0].reshape(1, V_DIM),
                      tl["attn_tq"], tl["attn_tk"], tl["attn_pw"], tl["attn_ahead"])

    h1, fn = _out_proj(attn.reshape(t, attn_w), conv, w_out[0], x2, ffn_norm_w[0],
                       tl["out_tm"], tl["out_rb"])
    hid = _gate_up(fn, w_gate[0], w_up[0], tl["gu_tm"], tl["gu_tn"])
    out = _down(hid, w_down[0], h1, tl["down_tm"], tl["down_tn"])
    return out.reshape(b, s, d)
```
